```python
import math, functools
import jax, jax.numpy as jnp
from jax import lax
import numpy as np

D_MODEL = 1024
BATCH = 8
SEQ = 2048
DEPTH = 1
DEC_BATCH = 128
DEC_SEQ = 4
PAST_LEN = 2048
PAGE_SIZE = 128

D_SSM = D_MODEL // 2
D_ATTN = D_MODEL // 2
D_MIX = D_SSM + D_ATTN
SSM_GROUP = 16
N_SSM_GROUPS = D_SSM // SSM_GROUP
SSM_STATE = 64
HEAD_DIM = 64
N_HEADS = D_ATTN // HEAD_DIM
DILATED_PATTERNS = ((128, 1), (512, 4), (2048, 16))
WINDOW_MAX = 2048
BAND_BLOCK = 128
D_IN_PROJ = 2 * D_SSM + 4 * D_ATTN
RMS_EPS = 1e-6
DT_MIN = 1e-3
DT_MAX = 1e-1

kernel_name = "hymba_s5_longnet_decode_step"


def _rmsnorm(x, g):
    xf = x.astype(jnp.float32)
    xf = xf * lax.rsqrt(jnp.mean(xf * xf, axis=-1, keepdims=True) + RMS_EPS)
    return (xf * g.astype(jnp.float32)).astype(x.dtype)


def _complex_affine_combine(e1, e2):
    ar1, ai1, br1, bi1 = e1
    ar2, ai2, br2, bi2 = e2
    ar = ar2 * ar1 - ai2 * ai1
    ai = ar2 * ai1 + ai2 * ar1
    br = ar2 * br1 - ai2 * bi1 + br2
    bi = ar2 * bi1 + ai2 * br1 + bi2
    return ar, ai, br, bi


def _s5_scan(u, h0_re, h0_im, A_re, A_im, log_dt, B_re, B_im, C_re, C_im, D):
    f32 = jnp.float32
    n_, t_ = u.shape[0], u.shape[1]
    u = u.astype(f32).reshape(n_, t_, N_SSM_GROUPS, SSM_GROUP)
    A_re = A_re.astype(f32)
    A_im = A_im.astype(f32)
    dt = jnp.exp(log_dt.astype(f32))[:, None]
    mag = jnp.exp(A_re * dt)
    ang = A_im * dt
    lam_re = mag * jnp.cos(ang)
    lam_im = mag * jnp.sin(ang)
    den = A_re * A_re + A_im * A_im
    f_re = ((lam_re - 1.0) * A_re + lam_im * A_im) / den
    f_im = (lam_im * A_re - (lam_re - 1.0) * A_im) / den
    B_re = B_re.astype(f32)
    B_im = B_im.astype(f32)
    Bb_re = f_re[..., None] * B_re - f_im[..., None] * B_im
    Bb_im = f_re[..., None] * B_im + f_im[..., None] * B_re
    bu_re = jnp.einsum('btgc,gpc->btgp', u, Bb_re)
    bu_im = jnp.einsum('btgc,gpc->btgp', u, Bb_im)
    h0_re = h0_re.astype(f32)
    h0_im = h0_im.astype(f32)
    bu_re = bu_re.at[:, 0].add(lam_re * h0_re - lam_im * h0_im)
    bu_im = bu_im.at[:, 0].add(lam_re * h0_im + lam_im * h0_re)
    a_re = jnp.broadcast_to(lam_re, bu_re.shape)
    a_im = jnp.broadcast_to(lam_im, bu_im.shape)
    _, _, x_re, x_im = lax.associative_scan(_complex_affine_combine, (a_re, a_im, bu_re, bu_im), axis=1)
    y = (jnp.einsum('btgp,gcp->btgc', x_re, C_re.astype(f32))
         - jnp.einsum('btgp,gcp->btgc', x_im, C_im.astype(f32))
         + D.astype(f32).reshape(N_SSM_GROUPS, SSM_GROUP) * u)
    return y.reshape(n_, t_, D_SSM), x_re[:, -1], x_im[:, -1]


def _dilated_band(q, k, v, window, dil):
    b_, s_, h_, e_ = q.shape
    n = s_ // dil
    nb = -(-n // BAND_BLOCK)
    npad = nb * BAND_BLOCK
    reach = window // dil

    def to_blocks(t):
        t = t.reshape(b_, n, dil, h_, e_).transpose(0, 2, 1, 3, 4)
        t = jnp.pad(t, ((0, 0), (0, 0), (0, npad - n), (0, 0), (0, 0)))
        return t.reshape(b_, dil, nb, BAND_BLOCK, h_, e_)

    def with_prev(t):
        prev = jnp.pad(t[:, :, :-1], ((0, 0), (0, 0), (1, 0), (0, 0), (0, 0), (0, 0)))
        return jnp.concatenate([prev, t], axis=3)

    qb = to_blocks(q)
    kk = with_prev(to_blocks(k))
    vv = with_prev(to_blocks(v))
    s = jnp.einsum('brnqhe,brnkhe->brnhqk', qb, kk) * (HEAD_DIM ** -0.5)
    qi = jnp.arange(BAND_BLOCK)[None, :, None]
    ki = jnp.arange(2 * BAND_BLOCK)[None, None, :]
    blk = jnp.arange(nb)[:, None, None]
    dist = qi + BAND_BLOCK - ki
    u_k = blk * BAND_BLOCK - BAND_BLOCK + ki
    valid = (dist >= 0) & (dist <= reach) & (u_k >= 0)
    s = jnp.where(valid[None, None, :, None], s, -jnp.inf)
    m = jnp.max(s, axis=-1)
    p = jnp.exp(s - m[..., None])
    l = jnp.sum(p, axis=-1)
    o = jnp.einsum('brnhqk,brnkhe->brnqhe', p, vv)
    o = o.reshape(b_, dil, npad, h_, e_)[:, :, :n].transpose(0, 2, 1, 3, 4).reshape(b_, s_, h_, e_)

    def stat_back(t):
        t = t.transpose(0, 1, 2, 4, 3).reshape(b_, dil, npad, h_)[:, :, :n]
        return t.transpose(0, 2, 1, 3).reshape(b_, s_, h_)

    return o, stat_back(m), stat_back(l)


def _dilated_gather(q, k_all, v_all, window, dil):
    t_ = q.shape[1]
    l_ = k_all.shape[1]
    offs = jnp.arange(window // dil + 1) * dil
    idx = (l_ - t_ + jnp.arange(t_))[:, None] - offs[None, :]
    valid = idx >= 0
    idx_c = jnp.clip(idx, 0, l_ - 1)
    kg = k_all[:, idx_c]
    vg = v_all[:, idx_c]
    s = jnp.einsum('bthe,btkhe->bhtk', q, kg) * (HEAD_DIM ** -0.5)
    s = jnp.where(valid[None, None], s, -jnp.inf)
    m = jnp.max(s, axis=-1)
    p = jnp.exp(s - m[..., None])
    l = jnp.sum(p, axis=-1)
    o = jnp.einsum('bhtk,btkhe->bthe', p, vg)
    return o, m.transpose(0, 2, 1), l.transpose(0, 2, 1)


def _combine(parts):
    big_m = functools.reduce(jnp.maximum, [m for _, m, _ in parts])
    ws = [jnp.exp(m - big_m) for _, m, _ in parts]
    num = functools.reduce(jnp.add, [w[..., None] * o for w, (o, _, _) in zip(ws, parts)])
    den = functools.reduce(jnp.add, [w * l for w, (_, _, l) in zip(ws, parts)])
    return num / den[..., None]


def _prompt_attention(q, k, v):
    return _combine([_dilated_band(q, k, v, w, d) for w, d in DILATED_PATTERNS])


def _sample_attention(q, k, v, k_past, v_past):
    k_all = jnp.concatenate([k_past.astype(jnp.float32), k], axis=1)
    v_all = jnp.concatenate([v_past.astype(jnp.float32), v], axis=1)
    return _combine([_dilated_gather(q, k_all, v_all, w, d) for w, d in DILATED_PATTERNS])


def _layer(h, h0_re, h0_im, attend, norm_in_g, w_in, A_re, A_im, log_dt, B_re, B_im, C_re, C_im, D,
           w_glu, b_glu, norm_ssm_g, norm_attn_g, w_out):
    n_, t_ = h.shape[0], h.shape[1]
    proj = _rmsnorm(h, norm_in_g) @ w_in
    o1 = D_SSM
    o2 = o1 + D_SSM
    o3 = o2 + D_ATTN
    o4 = o3 + D_ATTN
    o5 = o4 + D_ATTN
    x_s, z_s = proj[..., :o1], proj[..., o1:o2]
    q, k, v, z_a = proj[..., o2:o3], proj[..., o3:o4], proj[..., o4:o5], proj[..., o5:]
    y_s, st_re, st_im = _s5_scan(x_s, h0_re, h0_im, A_re, A_im, log_dt, B_re, B_im, C_re, C_im, D)
    g = jax.nn.gelu(y_s)
    gl = g @ w_glu.astype(jnp.float32) + b_glu.astype(jnp.float32)
    y_s = gl[..., :D_SSM] * jax.nn.sigmoid(gl[..., D_SSM:])
    y_s = (y_s * jax.nn.silu(z_s.astype(jnp.float32))).astype(h.dtype)
    y_s = _rmsnorm(y_s, norm_ssm_g)
    qh = q.astype(jnp.float32).reshape(n_, t_, N_HEADS, HEAD_DIM)
    kh = k.astype(jnp.float32).reshape(n_, t_, N_HEADS, HEAD_DIM)
    vh = v.astype(jnp.float32).reshape(n_, t_, N_HEADS, HEAD_DIM)
    y_a = attend(qh, kh, vh).reshape(n_, t_, D_ATTN)
    y_a = (y_a * jax.nn.silu(z_a.astype(jnp.float32))).astype(h.dtype)
    y_a = _rmsnorm(y_a, norm_attn_g)
    out = jnp.concatenate([y_s, y_a], axis=-1) @ w_out
    return (h + out).astype(h.dtype), kh.astype(h.dtype), vh.astype(h.dtype), st_re, st_im


def setup_inputs(seed: int = 0) -> dict:
    key = jax.random.key(seed)
    ks = jax.random.split(key, 24)
    f32 = jnp.float32
    wbuf = min(WINDOW_MAX, PAST_LEN)
    nrm = lambda k, shp: jax.random.normal(k, shp, f32)
    return {
        "x_prompt": nrm(ks[0], (BATCH, SEQ, D_MODEL)),
        "x_sample": nrm(ks[1], (DEC_BATCH, DEC_SEQ, D_MODEL)),
        "cache_k": nrm(ks[2], (DEPTH, DEC_BATCH, wbuf, N_HEADS, HEAD_DIM)),
        "cache_v": nrm(ks[3], (DEPTH, DEC_BATCH, wbuf, N_HEADS, HEAD_DIM)),
        "state_ssm_re": 0.3 * nrm(ks[4], (DEPTH, DEC_BATCH, N_SSM_GROUPS, SSM_STATE)),
        "state_ssm_im": 0.3 * nrm(ks[5], (DEPTH, DEC_BATCH, N_SSM_GROUPS, SSM_STATE)),
        "norm_in_g": 1.0 + 0.02 * nrm(ks[6], (DEPTH, D_MODEL)),
        "w_in": nrm(ks[7], (DEPTH, D_MODEL, D_IN_PROJ)) * D_MODEL ** -0.5,
        "ssm_A_re": -0.5 + 0.01 * nrm(ks[8], (DEPTH, N_SSM_GROUPS, SSM_STATE)),
        "ssm_A_im": math.pi * jnp.arange(SSM_STATE, dtype=f32) + 0.01 * nrm(ks[9], (DEPTH, N_SSM_GROUPS, SSM_STATE)),
        "ssm_log_dt": jax.random.uniform(ks[10], (DEPTH, N_SSM_GROUPS), f32, math.log(DT_MIN), math.log(DT_MAX)),
        "ssm_B_re": nrm(ks[11], (DEPTH, N_SSM_GROUPS, SSM_STATE, SSM_GROUP)) * (2 * SSM_GROUP) ** -0.5,
        "ssm_B_im": nrm(ks[12], (DEPTH, N_SSM_GROUPS, SSM_STATE, SSM_GROUP)) * (2 * SSM_GROUP) ** -0.5,
        "ssm_C_re": nrm(ks[13], (DEPTH, N_SSM_GROUPS, SSM_GROUP, SSM_STATE)) * SSM_STATE ** -0.5,
        "ssm_C_im": nrm(ks[14], (DEPTH, N_SSM_GROUPS, SSM_GROUP, SSM_STATE)) * SSM_STATE ** -0.5,
        "ssm_D": nrm(ks[15], (DEPTH, D_SSM)),
        "w_glu": nrm(ks[16], (DEPTH, D_SSM, 2 * D_SSM)) * D_SSM ** -0.5,
        "b_glu": 0.01 * nrm(ks[17], (DEPTH, 2 * D_SSM)),
        "norm_ssm_g": 1.0 + 0.02 * nrm(ks[18], (DEPTH, D_SSM)),
        "norm_attn_g": 1.0 + 0.02 * nrm(ks[19], (DEPTH, D_ATTN)),
        "w_out": nrm(ks[20], (DEPTH, D_MIX, D_MODEL)) * D_MIX ** -0.5,
        "final_norm_g": 1.0 + 0.02 * nrm(ks[21], (D_MODEL,)),
    }


def reference(x_prompt, x_sample, cache_k, cache_v, state_ssm_re, state_ssm_im, norm_in_g, w_in,
              ssm_A_re, ssm_A_im, ssm_log_dt, ssm_B_re, ssm_B_im, ssm_C_re, ssm_C_im, ssm_D,
              w_glu, b_glu, norm_ssm_g, norm_attn_g, w_out, final_norm_g):
    hp, hs = x_prompt, x_sample
    wbuf_p = min(WINDOW_MAX, x_prompt.shape[1])
    kp_l, vp_l, rp_l, ip_l = [], [], [], []
    ks_l, vs_l, rs_l, is_l = [], [], [], []
    zero_state = jnp.zeros((x_prompt.shape[0], N_SSM_GROUPS, SSM_STATE), jnp.float32)
    for l in range(DEPTH):
        params = (norm_in_g[l], w_in[l], ssm_A_re[l], ssm_A_im[l], ssm_log_dt[l], ssm_B_re[l], ssm_B_im[l],
                  ssm_C_re[l], ssm_C_im[l], ssm_D[l], w_glu[l], b_glu[l], norm_ssm_g[l], norm_attn_g[l], w_out[l])
        hp, kp, vp, rp, ip = _layer(hp, zero_state, zero_state, _prompt_attention, *params)
        attend_s = functools.partial(_sample_attention, k_past=cache_k[l], v_past=cache_v[l])
        hs, kn, vn, rs, is_ = _layer(hs, state_ssm_re[l], state_ssm_im[l], attend_s, *params)
        kp_l.append(kp[:, -wbuf_p:])
        vp_l.append(vp[:, -wbuf_p:])
        rp_l.append(rp)
        ip_l.append(ip)
        ks_l.append(kn)
        vs_l.append(vn)
        rs_l.append(rs)
        is_l.append(is_)
    y_prompt = _rmsnorm(hp, final_norm_g)
    y_sample = _rmsnorm(hs, final_norm_g)
    return (y_prompt, y_sample,
            jnp.stack(kp_l), jnp.stack(vp_l), jnp.stack(rp_l), jnp.stack(ip_l),
            jnp.stack(ks_l), jnp.stack(vs_l), jnp.stack(rs_l), jnp.stack(is_l))
```

```python
import functools

import jax
import jax.numpy as jnp
from jax import lax
from jax.experimental import pallas as pl
from jax.experimental.pallas import tpu as pltpu

F32 = jnp.float32
BF16 = jnp.bfloat16

D_MODEL = 1024
D_SSM = 512
D_ATTN = 512
SSM_GROUP = 16
N_SSM_GROUPS = D_SSM // SSM_GROUP
SSM_STATE = 64
D_STATE = N_SSM_GROUPS * SSM_STATE
HEAD_DIM = 64
N_HEADS = D_ATTN // HEAD_DIM
D_IN_PROJ = 2 * D_SSM + 4 * D_ATTN
RMS_EPS = 1e-6
DILATED_PATTERNS = ((128, 1), (512, 4), (2048, 16))
BAND = 128
WINDOW_MAX = 2048
LANES = 128
SUBLANES = 8
VMEM_LIMIT = 56 * 1024 * 1024
ROW_TILE = 512
SSM_STEPS = 64
NEG_INF = float("-inf")


def _sigmoid(x):
    return 1.0 / (1.0 + jnp.exp(-x))


def _rms_scale(x):
    return x * lax.rsqrt(jnp.mean(x * x, axis=-1, keepdims=True) + RMS_EPS)


def _ssm_prep_kernel(are_ref, aim_ref, ldt_ref, bre_ref, bim_ref, lam_ref, bbre_ref, bbim_ref):
    a_re = are_ref[...]
    a_im = aim_ref[...]
    dt = jnp.exp(ldt_ref[...])
    mag = jnp.exp(a_re * dt)
    ang = a_im * dt
    lam_re = mag * jnp.cos(ang)
    lam_im = mag * jnp.sin(ang)
    den = a_re * a_re + a_im * a_im
    f_re = ((lam_re - 1.0) * a_re + lam_im * a_im) / den
    f_im = (lam_im * a_re - (lam_re - 1.0) * a_im) / den
    lam_ref[:, 0:D_STATE] = jnp.broadcast_to(lam_re, (SUBLANES, D_STATE))
    lam_ref[:, D_STATE:2 * D_STATE] = jnp.broadcast_to(lam_im, (SUBLANES, D_STATE))
    b_re = bre_ref[...]
    b_im = bim_ref[...]
    bbre_ref[...] = f_re * b_re - f_im * b_im
    bbim_ref[...] = f_re * b_im + f_im * b_re


def _ssm_prep(a_re, a_im, log_dt, b_re, b_im):
    flat = lambda a: a.reshape(1, D_STATE)
    ldt = jnp.repeat(log_dt, SSM_STATE).reshape(1, D_STATE)
    bt = lambda b: b.transpose(2, 0, 1).reshape(SSM_GROUP, D_STATE)
    return pl.pallas_call(
        _ssm_prep_kernel,
        out_shape=(jax.ShapeDtypeStruct((SUBLANES, 2 * D_STATE), F32),
                   jax.ShapeDtypeStruct((SSM_GROUP, D_STATE), F32),
                   jax.ShapeDtypeStruct((SSM_GROUP, D_STATE), F32)),
        name="ssm_prep",
    )(flat(a_re), flat(a_im), ldt, bt(b_re), bt(b_im))


def _block_diag_weights(bb_re, bb_im, c_re, c_im):
    eye = jnp.eye(8, dtype=F32)
    def place_b(bb):
        b4 = bb.reshape(SSM_GROUP, 4, 8, SSM_STATE).transpose(1, 2, 0, 3)
        return jnp.einsum("kgcp,gh->kgchp", b4, eye).reshape(4, 128, 512)
    def place_c(c):
        c4 = c.reshape(4, 8, SSM_GROUP, SSM_STATE)
        return jnp.einsum("kgcp,gh->khpgc", c4, eye).reshape(4, 512, 128)
    wb = jnp.concatenate([place_b(bb_re), place_b(bb_im)], axis=2).astype(BF16)
    cc = jnp.concatenate([place_c(c_re), -place_c(c_im)], axis=1).astype(BF16)
    return wb, cc


def _in_proj_kernel(x_ref, g_ref, w_ref, xz_ref, q_ref, k_ref, v_ref, za_ref, *t_refs):
    xn = (_rms_scale(x_ref[...]) * g_ref[...]).astype(BF16)
    outs = (None, None, q_ref, k_ref, v_ref, za_ref)
    for j in range(D_IN_PROJ // 512):
        r = jnp.dot(xn, w_ref[:, j * 512:(j + 1) * 512], preferred_element_type=F32)
        if j < 2:
            for s in range(4):
                xz_ref[j * 4 + s] = r[:, s * LANES:(s + 1) * LANES]
        else:
            outs[j][...] = r
            if t_refs and j in (3, 4):
                t_refs[j - 3][...] = r.T


def _in_proj(x, norm_g, w_in_bf, *, seq_len=None):
    rows = x.shape[0]
    row_spec = lambda width: pl.BlockSpec((ROW_TILE, width), lambda i: (i, 0))
    out_specs = [pl.BlockSpec((8, ROW_TILE, LANES), lambda i: (0, i, 0))] + [row_spec(512)] * 4
    out_shape = [jax.ShapeDtypeStruct((8, rows, LANES), F32)] + [jax.ShapeDtypeStruct((rows, 512), F32)] * 4
    if seq_len is not None:
        per_seq = seq_len // ROW_TILE
        out_specs += [pl.BlockSpec((None, 512, ROW_TILE), lambda i: (i // per_seq, 0, i % per_seq))] * 2
        out_shape += [jax.ShapeDtypeStruct((rows // seq_len, 512, seq_len), F32)] * 2
    return pl.pallas_call(
        _in_proj_kernel,
        grid=(rows // ROW_TILE,),
        in_specs=[row_spec(D_MODEL),
                  pl.BlockSpec((1, D_MODEL), lambda i: (0, 0)),
                  pl.BlockSpec((D_MODEL, D_IN_PROJ), lambda i: (0, 0))],
        out_specs=out_specs,
        out_shape=out_shape,
        compiler_params=pltpu.CompilerParams(dimension_semantics=("parallel",), vmem_limit_bytes=VMEM_LIMIT),
        name="in_proj",
    )(x, norm_g.reshape(1, D_MODEL), w_in_bf)


def _ssm_kernel(xz_ref, h0_ref, lam_ref, wb_ref, cc_ref, d_ref, wglu_ref, bglu_ref, g_ref,
                ys_ref, st_ref, u_scr, bu_scr, ys_scr, *, nb, tt):
    rows = nb * tt
    prompt = xz_ref.ndim == 4

    @pl.when(pl.program_id(0) == 0)
    def _():
        st_ref[...] = h0_ref[...]

    for s in range(8):
        if prompt:
            for b in range(nb):
                u_scr[s, pl.ds(b, tt, stride=nb), :] = xz_ref[s, b]
        else:
            for t in range(tt):
                u_scr[s, t * nb:(t + 1) * nb, :] = xz_ref[s, pl.ds(t, nb, stride=tt), :]

    for kt in range(4):
        r = jnp.dot(u_scr[kt].astype(BF16), wb_ref[kt], preferred_element_type=F32)
        bu_scr[:, kt * 512:(kt + 1) * 512] = r[:, 0:512]
        bu_scr[:, D_STATE + kt * 512:D_STATE + (kt + 1) * 512] = r[:, 512:1024]

    width = (SUBLANES * LANES * 8) // nb if nb <= 64 else LANES
    for c in range(D_STATE // width):
        re_l = slice(c * width, (c + 1) * width)
        im_l = slice(D_STATE + c * width, D_STATE + (c + 1) * width)
        if nb == SUBLANES:
            lam_re, lam_im = lam_ref[:, re_l], lam_ref[:, im_l]
        else:
            lam_re = jnp.broadcast_to(lam_ref[0:1, re_l], (nb, width))
            lam_im = jnp.broadcast_to(lam_ref[0:1, im_l], (nb, width))

        def step(t, carry, re_l=re_l, im_l=im_l, lam_re=lam_re, lam_im=lam_im):
            s_re, s_im = carry
            r0 = t * nb if isinstance(t, int) else pl.multiple_of(t * nb, nb)
            n_re = lam_re * s_re - lam_im * s_im + bu_scr[pl.ds(r0, nb), re_l]
            n_im = lam_re * s_im + lam_im * s_re + bu_scr[pl.ds(r0, nb), im_l]
            bu_scr[pl.ds(r0, nb), re_l] = n_re
            bu_scr[pl.ds(r0, nb), im_l] = n_im
            return n_re, n_im

        carry = (st_ref[:, re_l], st_ref[:, im_l])
        if tt <= 8:
            for t in range(tt):
                carry = step(t, carry)
        else:
            carry = lax.fori_loop(0, tt, step, carry, unroll=4)
        st_ref[:, re_l] = carry[0]
        st_ref[:, im_l] = carry[1]

    ys = []
    for kt in range(4):
        xc = jnp.concatenate([bu_scr[:, kt * 512:(kt + 1) * 512],
                              bu_scr[:, D_STATE + kt * 512:D_STATE + (kt + 1) * 512]], axis=1).astype(BF16)
        y = jnp.dot(xc, cc_ref[kt], preferred_element_type=F32)
        ys.append(y + d_ref[:, kt * LANES:(kt + 1) * LANES] * u_scr[kt])
    y = jnp.concatenate(ys, axis=1)
    g = jax.nn.gelu(y)
    gl = jnp.dot(g.astype(BF16), wglu_ref[...], preferred_element_type=F32) + bglu_ref[...]
    z = jnp.concatenate([u_scr[4 + s] for s in range(4)], axis=1)
    y = gl[:, 0:D_SSM] * _sigmoid(gl[:, D_SSM:2 * D_SSM]) * (z * _sigmoid(z))
    y = _rms_scale(y) * g_ref[...]

    for s in range(4):
        ys_scr[s] = y[:, s * LANES:(s + 1) * LANES]
    for s in range(4):
        if prompt:
            for b in range(nb):
                ys_ref[s, b] = ys_scr[s, pl.ds(b, tt, stride=nb), :]
        else:
            for t in range(tt):
                ys_ref[s, pl.ds(t, nb, stride=tt), :] = ys_scr[s, t * nb:(t + 1) * nb, :]


def _ssm(xz, h0, lam, wb, cc, d_vec, w_glu_bf, b_glu, norm_g, *, nb, t_total, tt):
    rows = nb * tt
    const = lambda shape: pl.BlockSpec(shape, lambda i: (0,) * len(shape))
    if t_total > tt:
        xz_in = xz.reshape(8, nb, t_total, LANES)
        xz_spec = pl.BlockSpec((8, nb, tt, LANES), lambda i: (0, 0, i, 0))
        ys_spec = pl.BlockSpec((4, nb, tt, LANES), lambda i: (0, 0, i, 0))
        ys_shape = jax.ShapeDtypeStruct((4, nb, t_total, LANES), F32)
    else:
        xz_in = xz
        xz_spec = const((8, rows, LANES))
        ys_spec = const((4, rows, LANES))
        ys_shape = jax.ShapeDtypeStruct((4, rows, LANES), F32)
    ys, st = pl.pallas_call(
        functools.partial(_ssm_kernel, nb=nb, tt=tt),
        grid=(t_total // tt,),
        in_specs=[xz_spec, const((nb, 2 * D_STATE)), const((SUBLANES, 2 * D_STATE)),
                  const((4, 128, 1024)), const((4, 1024, 128)), const((1, D_SSM)),
                  const((D_SSM, 2 * D_SSM)), const((1, 2 * D_SSM)), const((1, D_SSM))],
        out_specs=[ys_spec, const((nb, 2 * D_STATE))],
        out_shape=[ys_shape, jax.ShapeDtypeStruct((nb, 2 * D_STATE), F32)],
        scratch_shapes=[pltpu.VMEM((8, rows, LANES), F32),
                        pltpu.VMEM((rows, 2 * D_STATE), F32),
                        pltpu.VMEM((4, rows, LANES), F32)],
        compiler_params=pltpu.CompilerParams(dimension_semantics=("arbitrary",), vmem_limit_bytes=VMEM_LIMIT),
        name="ssm_prompt" if t_total > tt else "ssm_sample",
    )(xz_in, h0, lam, wb, cc, d_vec.reshape(1, D_SSM), w_glu_bf, b_glu.reshape(1, 2 * D_SSM),
      norm_g.reshape(1, D_SSM))
    return ys.reshape(4, nb * t_total, LANES), st


def _band_block(q, k, v, offset):
    kwin = k.shape[0]
    qq = lax.broadcasted_iota(jnp.int32, (BAND, kwin), 0)
    kk = lax.broadcasted_iota(jnp.int32, (BAND, kwin), 1)
    dist = offset + qq - kk
    valid = (dist >= 0) & (dist <= BAND)
    low = lax.broadcasted_iota(jnp.int32, (BAND, LANES), 1) < HEAD_DIM
    kb = k.astype(BF16)
    vb = v.astype(BF16)
    parts = []
    for h in range(2):
        qh = (jnp.where(low if h == 0 else ~low, q, 0.0) * (HEAD_DIM ** -0.5)).astype(BF16)
        s = lax.dot_general(qh, kb, (((1,), (1,)), ((), ())), preferred_element_type=F32)
        s = jnp.where(valid, s, NEG_INF)
        m = jnp.max(s, axis=-1, keepdims=True)
        p = jnp.exp(s - m)
        l = jnp.sum(p, axis=-1, keepdims=True)
        o = jnp.dot(p.astype(BF16), vb, preferred_element_type=F32)
        parts.append((o, m, l))
    pick = lambda a, b: jnp.where(low, jnp.broadcast_to(a, (BAND, LANES)), jnp.broadcast_to(b, (BAND, LANES)))
    return tuple(pick(parts[0][i], parts[1][i]) for i in range(3))


def _prompt_attn_kernel(q_ref, k_ref, v_ref, z_ref, o_ref, o_scr, m_scr, l_scr):
    t_total = q_ref.shape[0]
    for pidx, (window, dil) in enumerate(DILATED_PATTERNS):
        n = t_total // dil
        nblk = n // BAND
        kwin = 2 * BAND if nblk > 1 else BAND

        def body(idx, _, pidx=pidx, dil=dil, nblk=nblk, kwin=kwin):
            rho = idx // nblk
            i = idx % nblk
            wblk = jnp.maximum(i - 1, 0)
            q0 = rho + dil * BAND * i
            k0 = rho + dil * BAND * wblk
            if dil == 1:
                qrows = pl.ds(pl.multiple_of(q0, BAND), BAND)
                krows = pl.ds(pl.multiple_of(k0, BAND), kwin)
            else:
                qrows = pl.ds(q0, BAND, stride=dil)
                krows = pl.ds(k0, kwin, stride=dil)
            o, m, l = _band_block(q_ref[qrows, :], k_ref[krows, :], v_ref[krows, :], (i - wblk) * BAND)
            o_scr[pidx, qrows, :] = o
            m_scr[pidx, qrows, :] = m
            l_scr[pidx, qrows, :] = l
            return 0

        lax.fori_loop(0, dil * nblk, body, 0)

    chunk = 256
    def combine(c, _):
        rows = pl.ds(pl.multiple_of(c * chunk, chunk), chunk)
        ms = [m_scr[p, rows, :] for p in range(3)]
        big = jnp.maximum(jnp.maximum(ms[0], ms[1]), ms[2])
        ws = [jnp.exp(m - big) for m in ms]
        num = ws[0] * o_scr[0, rows, :] + ws[1] * o_scr[1, rows, :] + ws[2] * o_scr[2, rows, :]
        den = ws[0] * l_scr[0, rows, :] + ws[1] * l_scr[1, rows, :] + ws[2] * l_scr[2, rows, :]
        z = z_ref[rows, :]
        o_ref[rows, :] = (num / den) * (z * _sigmoid(z))
        return 0
    lax.fori_loop(0, t_total // chunk, combine, 0)


def _prompt_attention(q, k, v, z, *, batch, t_total):
    shp = (batch, t_total, D_ATTN)
    spec = pl.BlockSpec((None, t_total, LANES), lambda b, p: (b, 0, p))
    out = pl.pallas_call(
        _prompt_attn_kernel,
        grid=(batch, D_ATTN // LANES),
        in_specs=[spec] * 4,
        out_specs=spec,
        out_shape=jax.ShapeDtypeStruct(shp, F32),
        scratch_shapes=[pltpu.VMEM((3, t_total, LANES), F32)] * 3,
        compiler_params=pltpu.CompilerParams(dimension_semantics=("parallel", "parallel"),
                                             vmem_limit_bytes=VMEM_LIMIT),
        name="prompt_attention",
    )(q.reshape(shp), k.reshape(shp), v.reshape(shp), z.reshape(shp))
    return out.reshape(batch * t_total, D_ATTN)


NEAR = 512
SEQS_PER_STEP = 2


def _sample_attn_kernel(q_ref, kn_ref, vn_ref, z_ref, kc_ref, vc_ref, o_ref, *, tt):
    assert 2 * tt == SUBLANES
    scale = HEAD_DIM ** -0.5
    nt = (((1,), (1,)), ((), ()))
    row8 = lax.broadcasted_iota(jnp.int32, (SUBLANES, LANES), 0)
    lane8 = lax.broadcasted_iota(jnp.int32, (SUBLANES, LANES), 1)
    top = row8 < tt
    own_head = top == (lane8 < HEAD_DIM)

    def dup(x, seq):
        rolled = pltpu.roll(x, tt, axis=0)
        return jnp.where(top, x, rolled) if seq == 0 else jnp.where(top, rolled, x)

    jq = lambda shape: lax.broadcasted_iota(jnp.int32, shape, 0) & (tt - 1)
    d_full = WINDOW_MAX + jq((SUBLANES, WINDOW_MAX)) - lax.broadcasted_iota(jnp.int32, (SUBLANES, WINDOW_MAX), 1)
    d_near = NEAR + jq((SUBLANES, NEAR)) - lax.broadcasted_iota(jnp.int32, (SUBLANES, NEAR), 1)
    d_new = jq((SUBLANES, SUBLANES)) - lax.broadcasted_iota(jnp.int32, (SUBLANES, SUBLANES), 1)
    new_key = lax.broadcasted_iota(jnp.int32, (SUBLANES, SUBLANES), 1) < tt
    ok_main, ok_new = [], []
    for window, dil in DILATED_PATTERNS:
        d = d_near if window <= NEAR else d_full
        ok_main.append(((d & (dil - 1)) == 0) & (d <= window))
        ok_new.append(new_key & (d_new >= 0) & ((d_new & (dil - 1)) == 0))

    z = z_ref[...]
    gate = z * _sigmoid(z)
    for pr in range(D_ATTN // LANES):
        lanes = slice(pr * LANES, (pr + 1) * LANES)
        halves = []
        for seq in range(SEQS_PER_STEP):
            q = jnp.where(own_head, dup(q_ref[:, lanes], seq) * scale, 0.0).astype(BF16)
            kn = dup(kn_ref[:, lanes], seq).astype(BF16)
            vn = dup(vn_ref[:, lanes], seq).astype(BF16).astype(F32)
            s_main = jnp.dot(q, kc_ref[seq, lanes, :].astype(BF16), preferred_element_type=F32)
            s_new = lax.dot_general(q, kn, nt, preferred_element_type=F32)

            stats = []
            for (window, dil), okm, okn in zip(DILATED_PATTERNS, ok_main, ok_new):
                sm = jnp.where(okm, s_main[:, WINDOW_MAX - NEAR:] if window <= NEAR else s_main, NEG_INF)
                sn = jnp.where(okn, s_new, NEG_INF)
                m = jnp.maximum(jnp.max(sm, axis=-1, keepdims=True), jnp.max(sn, axis=-1, keepdims=True))
                pm = jnp.exp(sm - m)
                pn = jnp.exp(sn - m)
                l = jnp.sum(pm, axis=-1, keepdims=True) + jnp.sum(pn, axis=-1, keepdims=True)
                stats.append((m, l, pm, pn))
            big = functools.reduce(jnp.maximum, [s[0] for s in stats])
            ws = [jnp.exp(s[0] - big) for s in stats]
            den = functools.reduce(jnp.add, [w * s[1] for w, s in zip(ws, stats)])
            p_new = functools.reduce(jnp.add, [w * s[3] for w, s in zip(ws, stats)])
            p_near = functools.reduce(jnp.add, [w * s[2] for w, s in zip(ws, stats) if s[2].shape[1] == NEAR])
            p_full = functools.reduce(jnp.add, [w * s[2] for w, s in zip(ws, stats) if s[2].shape[1] != NEAR])
            p_main = jnp.concatenate([p_full[:, 0:WINDOW_MAX - NEAR], p_full[:, WINDOW_MAX - NEAR:] + p_near], axis=1)

            acc = lax.dot_general(p_main.astype(BF16), vc_ref[seq, lanes, :].astype(BF16), nt,
                                  preferred_element_type=F32)
            p_new = p_new.astype(BF16).astype(F32)
            for j in range(tt):
                acc = acc + p_new[:, j:j + 1] * vn[j:j + 1, :]
            acc = acc / den
            halves.append(jnp.where(lane8 < HEAD_DIM, acc, pltpu.roll(acc, tt, axis=0)))
        both = jnp.where(top, halves[0], pltpu.roll(halves[1], tt, axis=0))
        o_ref[:, lanes] = both * gate[:, lanes]


def _sample_attention(q, k_new, v_new, z, cache_kt, cache_vt, *, batch, tt):
    assert cache_kt.shape == (batch, D_ATTN, WINDOW_MAX) and batch % SEQS_PER_STEP == 0
    tok = pl.BlockSpec((SEQS_PER_STEP * tt, D_ATTN), lambda i: (i, 0))
    cache = pl.BlockSpec((SEQS_PER_STEP, D_ATTN, WINDOW_MAX), lambda i: (i, 0, 0))
    return pl.pallas_call(
        functools.partial(_sample_attn_kernel, tt=tt),
        grid=(batch // SEQS_PER_STEP,),
        in_specs=[tok, tok, tok, tok, cache, cache],
        out_specs=tok,
        out_shape=jax.ShapeDtypeStruct((batch * tt, D_ATTN), F32),
        compiler_params=pltpu.CompilerParams(dimension_semantics=("parallel",), vmem_limit_bytes=VMEM_LIMIT),
        name="sample_attention",
    )(q, k_new, v_new, z, cache_kt, cache_vt)


def _out_proj_kernel(x_ref, ys_ref, ya_ref, ga_ref, w_ref, gf_ref, y_ref):
    ys = jnp.concatenate([ys_ref[s] for s in range(4)], axis=1)
    ya = _rms_scale(ya_ref[...]) * ga_ref[...]
    mix = jnp.concatenate([ys, ya], axis=1).astype(BF16)
    h = x_ref[...] + jnp.dot(mix, w_ref[...], preferred_element_type=F32)
    y_ref[...] = _rms_scale(h) * gf_ref[...]


def _out_proj(x, ys, ya, norm_attn_g, w_out_bf, final_g):
    rows = x.shape[0]
    row_spec = lambda width: pl.BlockSpec((ROW_TILE, width), lambda i: (i, 0))
    const = lambda shape: pl.BlockSpec(shape, lambda i: (0,) * len(shape))
    return pl.pallas_call(
        _out_proj_kernel,
        grid=(rows // ROW_TILE,),
        in_specs=[row_spec(D_MODEL), pl.BlockSpec((4, ROW_TILE, LANES), lambda i: (0, i, 0)), row_spec(D_ATTN),
                  const((1, D_ATTN)), const((D_MODEL, D_MODEL)), const((1, D_MODEL))],
        out_specs=row_spec(D_MODEL),
        out_shape=jax.ShapeDtypeStruct((rows, D_MODEL), F32),
        compiler_params=pltpu.CompilerParams(dimension_semantics=("parallel",), vmem_limit_bytes=VMEM_LIMIT),
        name="out_proj",
    )(x, ys, ya, norm_attn_g.reshape(1, D_ATTN), w_out_bf, final_g.reshape(1, D_MODEL))


def kernel(x_prompt, x_sample, cache_k, cache_v, state_ssm_re, state_ssm_im, norm_in_g, w_in,
           ssm_A_re, ssm_A_im, ssm_log_dt, ssm_B_re, ssm_B_im, ssm_C_re, ssm_C_im, ssm_D,
           w_glu, b_glu, norm_ssm_g, norm_attn_g, w_out, final_norm_g):
    depth = w_in.shape[0]
    assert depth == 1
    batch, seq, _ = x_prompt.shape
    dec_batch, dec_seq, _ = x_sample.shape
    assert batch == SUBLANES and seq == WINDOW_MAX and seq % SSM_STEPS == 0

    lam, bb_re, bb_im = _ssm_prep(ssm_A_re[0], ssm_A_im[0], ssm_log_dt[0], ssm_B_re[0], ssm_B_im[0])
    wb, cc = _block_diag_weights(bb_re, bb_im, ssm_C_re[0], ssm_C_im[0])
    w_in_bf = w_in[0].astype(BF16)
    w_glu_bf = w_glu[0].astype(BF16)
    w_out_bf = w_out[0].astype(BF16)

    def layer(x, h0, attend, nb, t_total, tt, seq_len=None):
        xz, q, k, v, za, *kv_t = _in_proj(x, norm_in_g[0], w_in_bf, seq_len=seq_len)
        ys, st = _ssm(xz, h0, lam, wb, cc, ssm_D[0], w_glu_bf, b_glu[0], norm_ssm_g[0], nb=nb, t_total=t_total, tt=tt)
        ya = attend(q, k, v, za)
        y = _out_proj(x, ys, ya, norm_attn_g[0], w_out_bf, final_norm_g)
        return y, (kv_t or (k, v)), st

    xp = x_prompt.reshape(batch * seq, D_MODEL)
    yp, (kpt, vpt), stp = layer(xp, jnp.zeros((batch, 2 * D_STATE), F32),
                                functools.partial(_prompt_attention, batch=batch, t_total=seq),
                                batch, seq, SSM_STEPS, seq_len=seq)

    xs = x_sample.reshape(dec_batch * dec_seq, D_MODEL)
    h0 = jnp.concatenate([state_ssm_re[0].reshape(dec_batch, D_STATE),
                          state_ssm_im[0].reshape(dec_batch, D_STATE)], axis=1)
    chan_major = lambda c: c[0].transpose(0, 2, 3, 1).reshape(dec_batch, D_ATTN, WINDOW_MAX)
    ysm, (ksm, vsm), sts = layer(xs, h0,
                                 functools.partial(_sample_attention, cache_kt=chan_major(cache_k),
                                                   cache_vt=chan_major(cache_v), batch=dec_batch, tt=dec_seq),
                                 dec_batch, dec_seq, dec_seq)

    heads = lambda a, n, t: a.reshape(1, n, t, N_HEADS, HEAD_DIM)
    heads_t = lambda a, n, t: a.reshape(n, N_HEADS, HEAD_DIM, t).transpose(0, 3, 1, 2)[None]
    state = lambda s, n, part: s[:, part * D_STATE:(part + 1) * D_STATE].reshape(1, n, N_SSM_GROUPS, SSM_STATE)
    return (yp.reshape(batch, seq, D_MODEL), ysm.reshape(dec_batch, dec_seq, D_MODEL),
            heads_t(kpt, batch, seq), heads_t(vpt, batch, seq), state(stp, batch, 0), state(stp, batch, 1),
            heads(ksm, dec_batch, dec_seq), heads(vsm, dec_batch, dec_seq),
            state(sts, dec_batch, 0), state(sts, dec_batch, 1))
```

```python
import functools

import jax
import jax.numpy as jnp
from jax import lax
from jax.experimental import pallas as pl
from jax.experimental.pallas import tpu as pltpu

F32 = jnp.float32
BF16 = jnp.bfloat16

D_MODEL = 1024
D_SSM = 512
D_ATTN = 512
SSM_GROUP = 16
N_SSM_GROUPS = D_SSM // SSM_GROUP
SSM_STATE = 64
D_STATE = N_SSM_GROUPS * SSM_STATE
HEAD_DIM = 64
N_HEADS = D_ATTN // HEAD_DIM
D_IN_PROJ = 2 * D_SSM + 4 * D_ATTN
RMS_EPS = 1e-6
DILATED_PATTERNS = ((128, 1), (512, 4), (2048, 16))
BAND = 128
WINDOW_MAX = 2048
LANES = 128
SUBLANES = 8
VMEM_LIMIT = 56 * 1024 * 1024
ROW_TILE = 512
SSM_STEPS = 64
NEG_INF = float("-inf")


def _sigmoid(x):
    return 1.0 / (1.0 + jnp.exp(-x))


def _rms_scale(x):
    return x * lax.rsqrt(jnp.mean(x * x, axis=-1, keepdims=True) + RMS_EPS)


def _ssm_prep_kernel(are_ref, aim_ref, ldt_ref, bre_ref, bim_ref, lam_ref, bbre_ref, bbim_ref):
    a_re = are_ref[...]
    a_im = aim_ref[...]
    dt = jnp.exp(ldt_ref[...])
    mag = jnp.exp(a_re * dt)
    ang = a_im * dt
    lam_re = mag * jnp.cos(ang)
    lam_im = mag * jnp.sin(ang)
    den = a_re * a_re + a_im * a_im
    f_re = ((lam_re - 1.0) * a_re + lam_im * a_im) / den
    f_im = (lam_im * a_re - (lam_re - 1.0) * a_im) / den
    lam_ref[:, 0:D_STATE] = jnp.broadcast_to(lam_re, (SUBLANES, D_STATE))
    lam_ref[:, D_STATE:2 * D_STATE] = jnp.broadcast_to(lam_im, (SUBLANES, D_STATE))
    b_re = bre_ref[...]
    b_im = bim_ref[...]
    bbre_ref[...] = f_re * b_re - f_im * b_im
    bbim_ref[...] = f_re * b_im + f_im * b_re


def _ssm_prep(a_re, a_im, log_dt, b_re, b_im):
    flat = lambda a: a.reshape(1, D_STATE)
    ldt = jnp.repeat(log_dt, SSM_STATE).reshape(1, D_STATE)
    bt = lambda b: b.transpose(2, 0, 1).reshape(SSM_GROUP, D_STATE)
    return pl.pallas_call(
        _ssm_prep_kernel,
        out_shape=(jax.ShapeDtypeStruct((SUBLANES, 2 * D_STATE), F32),
                   jax.ShapeDtypeStruct((SSM_GROUP, D_STATE), F32),
                   jax.ShapeDtypeStruct((SSM_GROUP, D_STATE), F32)),
        name="ssm_prep",
    )(flat(a_re), flat(a_im), ldt, bt(b_re), bt(b_im))


def _block_diag_weights(bb_re, bb_im, c_re, c_im):
    eye = jnp.eye(8, dtype=F32)
    def place_b(bb):
        b4 = bb.reshape(SSM_GROUP, 4, 8, SSM_STATE).transpose(1, 2, 0, 3)
        return jnp.einsum("kgcp,gh->kgchp", b4, eye).reshape(4, 128, 512)
    def place_c(c):
        c4 = c.reshape(4, 8, SSM_GROUP, SSM_STATE)
        return jnp.einsum("kgcp,gh->khpgc", c4, eye).reshape(4, 512, 128)
    wb = jnp.concatenate([place_b(bb_re), place_b(bb_im)], axis=2).astype(BF16)
    cc = jnp.concatenate([place_c(c_re), -place_c(c_im)], axis=1).astype(BF16)
    return wb, cc


def _in_proj_kernel(x_ref, g_ref, w_ref, xz_ref, q_ref, k_ref, v_ref, za_ref, *t_refs):
    xn = (_rms_scale(x_ref[...]) * g_ref[...]).astype(BF16)
    outs = (None, None, q_ref, k_ref, v_ref, za_ref)
    for j in range(D_IN_PROJ // 512):
        r = jnp.dot(xn, w_ref[:, j * 512:(j + 1) * 512], preferred_element_type=F32)
        if j < 2:
            for s in range(4):
                xz_ref[j * 4 + s] = r[:, s * LANES:(s + 1) * LANES]
        else:
            outs[j][...] = r
            if t_refs and j in (3, 4):
                t_refs[j - 3][...] = r.T


def _in_proj(x, norm_g, w_in_bf, *, seq_len=None):
    rows = x.shape[0]
    row_spec = lambda width: pl.BlockSpec((ROW_TILE, width), lambda i: (i, 0))
    out_specs = [pl.BlockSpec((8, ROW_TILE, LANES), lambda i: (0, i, 0))] + [row_spec(512)] * 4
    out_shape = [jax.ShapeDtypeStruct((8, rows, LANES), F32)] + [jax.ShapeDtypeStruct((rows, 512), F32)] * 4
    if seq_len is not None:
        per_seq = seq_len // ROW_TILE
        out_specs += [pl.BlockSpec((None, 512, ROW_TILE), lambda i: (i // per_seq, 0, i % per_seq))] * 2
        out_shape += [jax.ShapeDtypeStruct((rows // seq_len, 512, seq_len), F32)] * 2
    return pl.pallas_call(
        _in_proj_kernel,
        grid=(rows // ROW_TILE,),
        in_specs=[row_spec(D_MODEL),
                  pl.BlockSpec((1, D_MODEL), lambda i: (0, 0)),
                  pl.BlockSpec((D_MODEL, D_IN_PROJ), lambda i: (0, 0))],
        out_specs=out_specs,
        out_shape=out_shape,
        compiler_params=pltpu.CompilerParams(dimension_semantics=("parallel",), vmem_limit_bytes=VMEM_LIMIT),
        name="in_proj",
    )(x, norm_g.reshape(1, D_MODEL), w_in_bf)


def _ssm_kernel(xz_ref, h0_ref, lam_ref, wb_ref, cc_ref, d_ref, wglu_ref, bglu_ref, g_ref,
                ys_ref, st_ref, u_scr, bu_scr, ys_scr, *, nb, tt):
    rows = nb * tt
    prompt = xz_ref.ndim == 4

    @pl.when(pl.program_id(0) == 0)
    def _():
        st_ref[...] = h0_ref[...]

    for s in range(8):
        if prompt:
            for b in range(nb):
                u_scr[s, pl.ds(b, tt, stride=nb), :] = xz_ref[s, b]
        else:
            for t in range(tt):
                u_scr[s, t * nb:(t + 1) * nb, :] = xz_ref[s, pl.ds(t, nb, stride=tt), :]

    for kt in range(4):
        r = jnp.dot(u_scr[kt].astype(BF16), wb_ref[kt], preferred_element_type=F32)
        bu_scr[:, kt * 512:(kt + 1) * 512] = r[:, 0:512]
        bu_scr[:, D_STATE + kt * 512:D_STATE + (kt + 1) * 512] = r[:, 512:1024]

    width = (SUBLANES * LANES * 8) // nb if nb <= 64 else LANES
    for c in range(D_STATE // width):
        re_l = slice(c * width, (c + 1) * width)
        im_l = slice(D_STATE + c * width, D_STATE + (c + 1) * width)
        if nb == SUBLANES:
            lam_re, lam_im = lam_ref[:, re_l], lam_ref[:, im_l]
        else:
            lam_re = jnp.broadcast_to(lam_ref[0:1, re_l], (nb, width))
            lam_im = jnp.broadcast_to(lam_ref[0:1, im_l], (nb, width))

        def step(t, carry, re_l=re_l, im_l=im_l, lam_re=lam_re, lam_im=lam_im):
            s_re, s_im = carry
            r0 = t * nb if isinstance(t, int) else pl.multiple_of(t * nb, nb)
            n_re = lam_re * s_re - lam_im * s_im + bu_scr[pl.ds(r0, nb), re_l]
            n_im = lam_re * s_im + lam_im * s_re + bu_scr[pl.ds(r0, nb), im_l]
            bu_scr[pl.ds(r0, nb), re_l] = n_re
            bu_scr[pl.ds(r0, nb), im_l] = n_im
            return n_re, n_im

        carry = (st_ref[:, re_l], st_ref[:, im_l])
        if tt <= 8:
            for t in range(tt):
                carry = step(t, carry)
        else:
            carry = lax.fori_loop(0, tt, step, carry, unroll=4)
        st_ref[:, re_l] = carry[0]
        st_ref[:, im_l] = carry[1]

    ys = []
    for kt in range(4):
        xc = jnp.concatenate([bu_scr[:, kt * 512:(kt + 1) * 512],
                              bu_scr[:, D_STATE + kt * 512:D_STATE + (kt + 1) * 512]], axis=1).astype(BF16)
        y = jnp.dot(xc, cc_ref[kt], preferred_element_type=F32)
        ys.append(y + d_ref[:, kt * LANES:(kt + 1) * LANES] * u_scr[kt])
    y = jnp.concatenate(ys, axis=1)
    g = jax.nn.gelu(y)
    gl = jnp.dot(g.astype(BF16), wglu_ref[...], preferred_element_type=F32) + bglu_ref[...]
    z = jnp.concatenate([u_scr[4 + s] for s in range(4)], axis=1)
    y = gl[:, 0:D_SSM] * _sigmoid(gl[:, D_SSM:2 * D_SSM]) * (z * _sigmoid(z))
    y = _rms_scale(y) * g_ref[...]

    for s in range(4):
        ys_scr[s] = y[:, s * LANES:(s + 1) * LANES]
    for s in range(4):
        if prompt:
            for b in range(nb):
                ys_ref[s, b] = ys_scr[s, pl.ds(b, tt, stride=nb), :]
        else:
            for t in range(tt):
                ys_ref[s, pl.ds(t, nb, stride=tt), :] = ys_scr[s, t * nb:(t + 1) * nb, :]


def _ssm(xz, h0, lam, wb, cc, d_vec, w_glu_bf, b_glu, norm_g, *, nb, t_total, tt):
    rows = nb * tt
    const = lambda shape: pl.BlockSpec(shape, lambda i: (0,) * len(shape))
    if t_total > tt:
        xz_in = xz.reshape(8, nb, t_total, LANES)
        xz_spec = pl.BlockSpec((8, nb, tt, LANES), lambda i: (0, 0, i, 0))
        ys_spec = pl.BlockSpec((4, nb, tt, LANES), lambda i: (0, 0, i, 0))
        ys_shape = jax.ShapeDtypeStruct((4, nb, t_total, LANES), F32)
    else:
        xz_in = xz
        xz_spec = const((8, rows, LANES))
        ys_spec = const((4, rows, LANES))
        ys_shape = jax.ShapeDtypeStruct((4, rows, LANES), F32)
    ys, st = pl.pallas_call(
        functools.partial(_ssm_kernel, nb=nb, tt=tt),
        grid=(t_total // tt,),
        in_specs=[xz_spec, const((nb, 2 * D_STATE)), const((SUBLANES, 2 * D_STATE)),
                  const((4, 128, 1024)), const((4, 1024, 128)), const((1, D_SSM)),
                  const((D_SSM, 2 * D_SSM)), const((1, 2 * D_SSM)), const((1, D_SSM))],
        out_specs=[ys_spec, const((nb, 2 * D_STATE))],
        out_shape=[ys_shape, jax.ShapeDtypeStruct((nb, 2 * D_STATE), F32)],
        scratch_shapes=[pltpu.VMEM((8, rows, LANES), F32),
                        pltpu.VMEM((rows, 2 * D_STATE), F32),
                        pltpu.VMEM((4, rows, LANES), F32)],
        compiler_params=pltpu.CompilerParams(dimension_semantics=("arbitrary",), vmem_limit_bytes=VMEM_LIMIT),
        name="ssm_prompt" if t_total > tt else "ssm_sample",
    )(xz_in, h0, lam, wb, cc, d_vec.reshape(1, D_SSM), w_glu_bf, b_glu.reshape(1, 2 * D_SSM),
      norm_g.reshape(1, D_SSM))
    return ys.reshape(4, nb * t_total, LANES), st


MAX_GROUP_BLOCKS = 6


def _largest_divisor(n, cap):
    return max(g for g in range(1, cap + 1) if n % g == 0)


def _band_blocks(qs, ks, vs, bias):
    low = lax.broadcasted_iota(jnp.int32, (BAND, LANES), 1) < HEAD_DIM
    nt = (((1,), (1,)), ((), ()))
    stacked = []
    for q in qs:
        q = q * (HEAD_DIM ** -0.5)
        stacked.append(jnp.concatenate([jnp.where(low, q, 0.0), jnp.where(low, 0.0, q)], axis=0).astype(BF16))
    ss = [lax.dot_general(q, k.astype(BF16), nt, preferred_element_type=F32) + bias for q, k in zip(stacked, ks)]
    ms = [jnp.max(s, axis=-1, keepdims=True) for s in ss]
    ps = [jnp.exp(s - m) for s, m in zip(ss, ms)]
    ls = [jnp.sum(p, axis=-1, keepdims=True) for p in ps]
    outs = [jnp.dot(p.astype(BF16), v.astype(BF16), preferred_element_type=F32) for p, v in zip(ps, vs)]
    wide = lambda a: jnp.broadcast_to(a, (BAND, a.shape[1] if a.shape[1] > 1 else LANES))
    return [tuple(jnp.where(low, wide(a[0:BAND]), wide(a[BAND:2 * BAND])) for a in oml)
            for oml in zip(outs, ms, ls)]


def _prompt_attn_kernel(q_ref, k_ref, v_ref, z_ref, o_ref, o_scr, m_scr, l_scr, band_scr, causal_scr):
    t_total = q_ref.shape[0]
    for scr, kwin in ((band_scr, 2 * BAND), (causal_scr, BAND)):
        qq = lax.broadcasted_iota(jnp.int32, (2 * BAND, kwin), 0) & (BAND - 1)
        kk = lax.broadcasted_iota(jnp.int32, (2 * BAND, kwin), 1)
        valid = ((kk >= qq) & (kk <= qq + BAND)) if kwin > BAND else (kk <= qq)
        scr[...] = jnp.where(valid, 0.0, NEG_INF)

    def run_blocks(pidx, dil, starts, kwin, bias_ref):
        rows = []
        for q0, k0 in starts:
            if dil == 1:
                rows.append((pl.ds(pl.multiple_of(q0, BAND), BAND), pl.ds(pl.multiple_of(k0, BAND), kwin)))
            else:
                rows.append((pl.ds(q0, BAND, stride=dil), pl.ds(k0, kwin, stride=dil)))
        res = _band_blocks([q_ref[qr, :] for qr, _ in rows], [k_ref[kr, :] for _, kr in rows],
                           [v_ref[kr, :] for _, kr in rows], bias_ref[...])
        for (qr, _), (o, m, l) in zip(rows, res):
            o_scr[pidx, qr, :] = o
            m_scr[pidx, qr, :] = m
            l_scr[pidx, qr, :] = l

    for pidx, (window, dil) in enumerate(DILATED_PATTERNS):
        nblk = t_total // dil // BAND

        group = _largest_divisor(dil, MAX_GROUP_BLOCKS)
        def first(g, _, pidx=pidx, dil=dil, group=group):
            run_blocks(pidx, dil, [(g * group + r,) * 2 for r in range(group)], BAND, causal_scr)
            return 0
        lax.fori_loop(0, dil // group, first, 0)

        n_rest = dil * (nblk - 1)
        if n_rest:
            group = _largest_divisor(n_rest, MAX_GROUP_BLOCKS)
            def rest(g, _, pidx=pidx, dil=dil, nblk=nblk, group=group):
                starts = []
                for r in range(group):
                    idx = g * group + r
                    rho = idx // (nblk - 1)
                    i = 1 + idx % (nblk - 1)
                    starts.append((rho + dil * BAND * i, rho + dil * BAND * (i - 1)))
                run_blocks(pidx, dil, starts, 2 * BAND, band_scr)
                return 0
            lax.fori_loop(0, n_rest // group, rest, 0)

    chunk = 256
    def combine(c, _):
        rows = pl.ds(pl.multiple_of(c * chunk, chunk), chunk)
        ms = [m_scr[p, rows, :] for p in range(3)]
        big = jnp.maximum(jnp.maximum(ms[0], ms[1]), ms[2])
        ws = [jnp.exp(m - big) for m in ms]
        num = ws[0] * o_scr[0, rows, :] + ws[1] * o_scr[1, rows, :] + ws[2] * o_scr[2, rows, :]
        den = ws[0] * l_scr[0, rows, :] + ws[1] * l_scr[1, rows, :] + ws[2] * l_scr[2, rows, :]
        z = z_ref[rows, :]
        o_ref[rows, :] = (num / den) * (z * _sigmoid(z))
        return 0
    lax.fori_loop(0, t_total // chunk, combine, 0)


def _prompt_attention(q, k, v, z, *, batch, t_total):
    shp = (batch, t_total, D_ATTN)
    spec = pl.BlockSpec((None, t_total, LANES), lambda b, p: (b, 0, p))
    out = pl.pallas_call(
        _prompt_attn_kernel,
        grid=(batch, D_ATTN // LANES),
        in_specs=[spec] * 4,
        out_specs=spec,
        out_shape=jax.ShapeDtypeStruct(shp, F32),
        scratch_shapes=[pltpu.VMEM((3, t_total, LANES), F32)] * 3
                       + [pltpu.VMEM((2 * BAND, 2 * BAND), F32), pltpu.VMEM((2 * BAND, BAND), F32)],
        compiler_params=pltpu.CompilerParams(dimension_semantics=("parallel", "parallel"),
                                             vmem_limit_bytes=VMEM_LIMIT),
        name="prompt_attention",
    )(q.reshape(shp), k.reshape(shp), v.reshape(shp), z.reshape(shp))
    return out.reshape(batch * t_total, D_ATTN)


NEAR = 512
SEQS_PER_STEP = 2


def _sample_attn_kernel(q_ref, kn_ref, vn_ref, z_ref, kc_ref, vc_ref, o_ref, *, tt):
    assert 2 * tt == SUBLANES
    scale = HEAD_DIM ** -0.5
    nt = (((1,), (1,)), ((), ()))
    row8 = lax.broadcasted_iota(jnp.int32, (SUBLANES, LANES), 0)
    lane8 = lax.broadcasted_iota(jnp.int32, (SUBLANES, LANES), 1)
    top = row8 < tt
    own_head = top == (lane8 < HEAD_DIM)

    def dup(x, seq):
        rolled = pltpu.roll(x, tt, axis=0)
        return jnp.where(top, x, rolled) if seq == 0 else jnp.where(top, rolled, x)

    jq = lambda shape: lax.broadcasted_iota(jnp.int32, shape, 0) & (tt - 1)
    d_full = WINDOW_MAX + jq((SUBLANES, WINDOW_MAX)) - lax.broadcasted_iota(jnp.int32, (SUBLANES, WINDOW_MAX), 1)
    d_near = NEAR + jq((SUBLANES, NEAR)) - lax.broadcasted_iota(jnp.int32, (SUBLANES, NEAR), 1)
    d_new = jq((SUBLANES, SUBLANES)) - lax.broadcasted_iota(jnp.int32, (SUBLANES, SUBLANES), 1)
    new_key = lax.broadcasted_iota(jnp.int32, (SUBLANES, SUBLANES), 1) < tt
    ok_main, ok_new = [], []
    for window, dil in DILATED_PATTERNS:
        d = d_near if window <= NEAR else d_full
        ok_main.append(((d & (dil - 1)) == 0) & (d <= window))
        ok_new.append(new_key & (d_new >= 0) & ((d_new & (dil - 1)) == 0))

    z = z_ref[...]
    gate = z * _sigmoid(z)
    for pr in range(D_ATTN // LANES):
        lanes = slice(pr * LANES, (pr + 1) * LANES)
        halves = []
        for seq in range(SEQS_PER_STEP):
            q = jnp.where(own_head, dup(q_ref[:, lanes], seq) * scale, 0.0).astype(BF16)
            kn = dup(kn_ref[:, lanes], seq).astype(BF16)
            vn = dup(vn_ref[:, lanes], seq).astype(BF16).astype(F32)
            s_main = jnp.dot(q, kc_ref[seq, lanes, :].astype(BF16), preferred_element_type=F32)
            s_new = lax.dot_general(q, kn, nt, preferred_element_type=F32)

            stats = []
            for (window, dil), okm, okn in zip(DILATED_PATTERNS, ok_main, ok_new):
                sm = jnp.where(okm, s_main[:, WINDOW_MAX - NEAR:] if window <= NEAR else s_main, NEG_INF)
                sn = jnp.where(okn, s_new, NEG_INF)
                m = jnp.maximum(jnp.max(sm, axis=-1, keepdims=True), jnp.max(sn, axis=-1, keepdims=True))
                pm = jnp.exp(sm - m)
                pn = jnp.exp(sn - m)
                l = jnp.sum(pm, axis=-1, keepdims=True) + jnp.sum(pn, axis=-1, keepdims=True)
                stats.append((m, l, pm, pn))
            big = functools.reduce(jnp.maximum, [s[0] for s in stats])
            ws = [jnp.exp(s[0] - big) for s in stats]
            den = functools.reduce(jnp.add, [w * s[1] for w, s in zip(ws, stats)])
            p_new = functools.reduce(jnp.add, [w * s[3] for w, s in zip(ws, stats)])
            p_near = functools.reduce(jnp.add, [w * s[2] for w, s in zip(ws, stats) if s[2].shape[1] == NEAR])
            p_full = functools.reduce(jnp.add, [w * s[2] for w, s in zip(ws, stats) if s[2].shape[1] != NEAR])
            p_main = jnp.concatenate([p_full[:, 0:WINDOW_MAX - NEAR], p_full[:, WINDOW_MAX - NEAR:] + p_near], axis=1)

            acc = lax.dot_general(p_main.astype(BF16), vc_ref[seq, lanes, :].astype(BF16), nt,
                                  preferred_element_type=F32)
            p_new = p_new.astype(BF16).astype(F32)
            for j in range(tt):
                acc = acc + p_new[:, j:j + 1] * vn[j:j + 1, :]
            acc = acc / den
            halves.append(jnp.where(lane8 < HEAD_DIM, acc, pltpu.roll(acc, tt, axis=0)))
        both = jnp.where(top, halves[0], pltpu.roll(halves[1], tt, axis=0))
        o_ref[:, lanes] = both * gate[:, lanes]


def _sample_attention(q, k_new, v_new, z, cache_kt, cache_vt, *, batch, tt):
    assert cache_kt.shape == (batch, D_ATTN, WINDOW_MAX) and batch % SEQS_PER_STEP == 0
    tok = pl.BlockSpec((SEQS_PER_STEP * tt, D_ATTN), lambda i: (i, 0))
    cache = pl.BlockSpec((SEQS_PER_STEP, D_ATTN, WINDOW_MAX), lambda i: (i, 0, 0))
    return pl.pallas_call(
        functools.partial(_sample_attn_kernel, tt=tt),
        grid=(batch // SEQS_PER_STEP,),
        in_specs=[tok, tok, tok, tok, cache, cache],
        out_specs=tok,
        out_shape=jax.ShapeDtypeStruct((batch * tt, D_ATTN), F32),
        compiler_params=pltpu.CompilerParams(dimension_semantics=("parallel",), vmem_limit_bytes=VMEM_LIMIT),
        name="sample_attention",
    )(q, k_new, v_new, z, cache_kt, cache_vt)


def _out_proj_kernel(x_ref, ys_ref, ya_ref, ga_ref, w_ref, gf_ref, y_ref):
    ys = jnp.concatenate([ys_ref[s] for s in range(4)], axis=1)
    ya = _rms_scale(ya_ref[...]) * ga_ref[...]
    mix = jnp.concatenate([ys, ya], axis=1).astype(BF16)
    h = x_ref[...] + jnp.dot(mix, w_ref[...], preferred_element_type=F32)
    y_ref[...] = _rms_scale(h) * gf_ref[...]


def _out_proj(x, ys, ya, norm_attn_g, w_out_bf, final_g):
    rows = x.shape[0]
    row_spec = lambda width: pl.BlockSpec((ROW_TILE, width), lambda i: (i, 0))
    const = lambda shape: pl.BlockSpec(shape, lambda i: (0,) * len(shape))
    return pl.pallas_call(
        _out_proj_kernel,
        grid=(rows // ROW_TILE,),
        in_specs=[row_spec(D_MODEL), pl.BlockSpec((4, ROW_TILE, LANES), lambda i: (0, i, 0)), row_spec(D_ATTN),
                  const((1, D_ATTN)), const((D_MODEL, D_MODEL)), const((1, D_MODEL))],
        out_specs=row_spec(D_MODEL),
        out_shape=jax.ShapeDtypeStruct((rows, D_MODEL), F32),
        compiler_params=pltpu.CompilerParams(dimension_semantics=("parallel",), vmem_limit_bytes=VMEM_LIMIT),
        name="out_proj",
    )(x, ys, ya, norm_attn_g.reshape(1, D_ATTN), w_out_bf, final_g.reshape(1, D_MODEL))


def kernel(x_prompt, x_sample, cache_k, cache_v, state_ssm_re, state_ssm_im, norm_in_g, w_in,
           ssm_A_re, ssm_A_im, ssm_log_dt, ssm_B_re, ssm_B_im, ssm_C_re, ssm_C_im, ssm_D,
           w_glu, b_glu, norm_ssm_g, norm_attn_g, w_out, final_norm_g):
    depth = w_in.shape[0]
    assert depth == 1
    batch, seq, _ = x_prompt.shape
    dec_batch, dec_seq, _ = x_sample.shape
    assert batch == SUBLANES and seq == WINDOW_MAX and seq % SSM_STEPS == 0

    lam, bb_re, bb_im = _ssm_prep(ssm_A_re[0], ssm_A_im[0], ssm_log_dt[0], ssm_B_re[0], ssm_B_im[0])
    wb, cc = _block_diag_weights(bb_re, bb_im, ssm_C_re[0], ssm_C_im[0])
    w_in_bf = w_in[0].astype(BF16)
    w_glu_bf = w_glu[0].astype(BF16)
    w_out_bf = w_out[0].astype(BF16)

    def layer(x, h0, attend, nb, t_total, tt, seq_len=None):
        xz, q, k, v, za, *kv_t = _in_proj(x, norm_in_g[0], w_in_bf, seq_len=seq_len)
        ys, st = _ssm(xz, h0, lam, wb, cc, ssm_D[0], w_glu_bf, b_glu[0], norm_ssm_g[0], nb=nb, t_total=t_total, tt=tt)
        ya = attend(q, k, v, za)
        y = _out_proj(x, ys, ya, norm_attn_g[0], w_out_bf, final_norm_g)
        return y, (kv_t or (k, v)), st

    xp = x_prompt.reshape(batch * seq, D_MODEL)
    yp, (kpt, vpt), stp = layer(xp, jnp.zeros((batch, 2 * D_STATE), F32),
                                functools.partial(_prompt_attention, batch=batch, t_total=seq),
                                batch, seq, SSM_STEPS, seq_len=seq)

    xs = x_sample.reshape(dec_batch * dec_seq, D_MODEL)
    h0 = jnp.concatenate([state_ssm_re[0].reshape(dec_batch, D_STATE),
                          state_ssm_im[0].reshape(dec_batch, D_STATE)], axis=1)
    chan_major = lambda c: c[0].transpose(0, 2, 3, 1).reshape(dec_batch, D_ATTN, WINDOW_MAX)
    ysm, (ksm, vsm), sts = layer(xs, h0,
                                 functools.partial(_sample_attention, cache_kt=chan_major(cache_k),
                                                   cache_vt=chan_major(cache_v), batch=dec_batch, tt=dec_seq),
                                 dec_batch, dec_seq, dec_seq)

    heads = lambda a, n, t: a.reshape(1, n, t, N_HEADS, HEAD_DIM)
    heads_t = lambda a, n, t: a.reshape(n, N_HEADS, HEAD_DIM, t).transpose(0, 3, 1, 2)[None]
    state = lambda s, n, part: s[:, part * D_STATE:(part + 1) * D_STATE].reshape(1, n, N_SSM_GROUPS, SSM_STATE)
    return (yp.reshape(batch, seq, D_MODEL), ysm.reshape(dec_batch, dec_seq, D_MODEL),
            heads_t(kpt, batch, seq), heads_t(vpt, batch, seq), state(stp, batch, 0), state(stp, batch, 1),
            heads(ksm, dec_batch, dec_seq), heads(vsm, dec_batch, dec_seq),
            state(sts, dec_batch, 0), state(sts, dec_batch, 1))
```

```python
import functools

import jax
import jax.numpy as jnp
from jax import lax
from jax.experimental import pallas as pl
from jax.experimental.pallas import tpu as pltpu

F32 = jnp.float32
BF16 = jnp.bfloat16

D_MODEL = 1024
D_SSM = 512
D_ATTN = 512
SSM_GROUP = 16
N_SSM_GROUPS = D_SSM // SSM_GROUP
SSM_STATE = 64
D_STATE = N_SSM_GROUPS * SSM_STATE
HEAD_DIM = 64
N_HEADS = D_ATTN // HEAD_DIM
D_IN_PROJ = 2 * D_SSM + 4 * D_ATTN
RMS_EPS = 1e-6
DILATED_PATTERNS = ((128, 1), (512, 4), (2048, 16))
BAND = 128
WINDOW_MAX = 2048
LANES = 128
SUBLANES = 8
VMEM_LIMIT = 56 * 1024 * 1024
ROW_TILE = 512
SSM_STEPS = 128
SSM_SUB_STEPS = 32
NEG_INF = float("-inf")


def _sigmoid(x):
    return 1.0 / (1.0 + jnp.exp(-x))


def _rms_scale(x):
    return x * lax.rsqrt(jnp.mean(x * x, axis=-1, keepdims=True) + RMS_EPS)


def _ssm_prep_kernel(are_ref, aim_ref, ldt_ref, bre_ref, bim_ref, lam_ref, bbre_ref, bbim_ref):
    a_re = are_ref[...]
    a_im = aim_ref[...]
    dt = jnp.exp(ldt_ref[...])
    mag = jnp.exp(a_re * dt)
    ang = a_im * dt
    lam_re = mag * jnp.cos(ang)
    lam_im = mag * jnp.sin(ang)
    den = a_re * a_re + a_im * a_im
    f_re = ((lam_re - 1.0) * a_re + lam_im * a_im) / den
    f_im = (lam_im * a_re - (lam_re - 1.0) * a_im) / den
    lam_ref[:, 0:D_STATE] = jnp.broadcast_to(lam_re, (SUBLANES, D_STATE))
    lam_ref[:, D_STATE:2 * D_STATE] = jnp.broadcast_to(lam_im, (SUBLANES, D_STATE))
    b_re = bre_ref[...]
    b_im = bim_ref[...]
    bbre_ref[...] = f_re * b_re - f_im * b_im
    bbim_ref[...] = f_re * b_im + f_im * b_re


def _ssm_prep(a_re, a_im, log_dt, b_re, b_im):
    flat = lambda a: a.reshape(1, D_STATE)
    ldt = jnp.repeat(log_dt, SSM_STATE).reshape(1, D_STATE)
    bt = lambda b: b.transpose(2, 0, 1).reshape(SSM_GROUP, D_STATE)
    return pl.pallas_call(
        _ssm_prep_kernel,
        out_shape=(jax.ShapeDtypeStruct((SUBLANES, 2 * D_STATE), F32),
                   jax.ShapeDtypeStruct((SSM_GROUP, D_STATE), F32),
                   jax.ShapeDtypeStruct((SSM_GROUP, D_STATE), F32)),
        name="ssm_prep",
    )(flat(a_re), flat(a_im), ldt, bt(b_re), bt(b_im))


def _block_diag_weights(bb_re, bb_im, c_re, c_im):
    eye = jnp.eye(8, dtype=F32)
    def place_b(bb):
        b4 = bb.reshape(SSM_GROUP, 4, 8, SSM_STATE).transpose(1, 2, 0, 3)
        return jnp.einsum("kgcp,gh->kgchp", b4, eye).reshape(4, 128, 512)
    def place_c(c):
        c4 = c.reshape(4, 8, SSM_GROUP, SSM_STATE)
        return jnp.einsum("kgcp,gh->khpgc", c4, eye).reshape(4, 512, 128)
    wb = jnp.concatenate([place_b(bb_re), place_b(bb_im)], axis=2).astype(BF16)
    cc = jnp.concatenate([place_c(c_re), -place_c(c_im)], axis=1).astype(BF16)
    return wb, cc


def _in_proj_kernel(x_ref, g_ref, w_ref, xz_ref, q_ref, k_ref, v_ref, za_ref, *t_refs):
    xn = (_rms_scale(x_ref[...]) * g_ref[...]).astype(BF16)
    outs = (None, None, q_ref, k_ref, v_ref, za_ref)
    for j in range(D_IN_PROJ // 512):
        r = jnp.dot(xn, w_ref[:, j * 512:(j + 1) * 512], preferred_element_type=F32)
        if j < 2:
            for s in range(4):
                xz_ref[j * 4 + s] = r[:, s * LANES:(s + 1) * LANES]
        else:
            outs[j][...] = r
            if t_refs and j in (3, 4):
                t_refs[j - 3][...] = r.T


def _in_proj(x, norm_g, w_in_bf, *, seq_len=None):
    rows = x.shape[0]
    row_spec = lambda width: pl.BlockSpec((ROW_TILE, width), lambda i: (i, 0))
    out_specs = [pl.BlockSpec((8, ROW_TILE, LANES), lambda i: (0, i, 0))] + [row_spec(512)] * 4
    out_shape = [jax.ShapeDtypeStruct((8, rows, LANES), F32)] + [jax.ShapeDtypeStruct((rows, 512), F32)] * 4
    if seq_len is not None:
        per_seq = seq_len // ROW_TILE
        out_specs += [pl.BlockSpec((None, 512, ROW_TILE), lambda i: (i // per_seq, 0, i % per_seq))] * 2
        out_shape += [jax.ShapeDtypeStruct((rows // seq_len, 512, seq_len), F32)] * 2
    return pl.pallas_call(
        _in_proj_kernel,
        grid=(rows // ROW_TILE,),
        in_specs=[row_spec(D_MODEL),
                  pl.BlockSpec((1, D_MODEL), lambda i: (0, 0)),
                  pl.BlockSpec((D_MODEL, D_IN_PROJ), lambda i: (0, 0))],
        out_specs=out_specs,
        out_shape=out_shape,
        compiler_params=pltpu.CompilerParams(dimension_semantics=("parallel",), vmem_limit_bytes=VMEM_LIMIT),
        name="in_proj",
    )(x, norm_g.reshape(1, D_MODEL), w_in_bf)


def _ssm_kernel(xz_ref, h0_ref, lam_ref, wb_ref, cc_ref, d_ref, wglu_ref, bglu_ref, g_ref,
                ys_ref, st_ref, u_scr, bu_scr, ys_scr, *, nb, tt):
    rows = nb * tt
    prompt = xz_ref.ndim == 4

    @pl.when(pl.program_id(0) == 0)
    def _():
        st_ref[...] = h0_ref[...]

    for s in range(8):
        if prompt:
            for b in range(nb):
                u_scr[s, pl.ds(b, tt, stride=nb), :] = xz_ref[s, b]
        else:
            for t in range(tt):
                u_scr[s, t * nb:(t + 1) * nb, :] = xz_ref[s, pl.ds(t, nb, stride=tt), :]

    sub = min(tt, SSM_SUB_STEPS)
    sub_rows = [slice(j * sub * nb, (j + 1) * sub * nb) for j in range(tt // sub)]

    for rs in sub_rows:
        for kt in range(4):
            r = jnp.dot(u_scr[kt, rs, :].astype(BF16), wb_ref[kt], preferred_element_type=F32)
            bu_scr[rs, kt * 512:(kt + 1) * 512] = r[:, 0:512]
            bu_scr[rs, D_STATE + kt * 512:D_STATE + (kt + 1) * 512] = r[:, 512:1024]

    width = (SUBLANES * LANES * 8) // nb if nb <= 64 else LANES
    chunks = [(slice(c * width, (c + 1) * width), slice(D_STATE + c * width, D_STATE + (c + 1) * width))
              for c in range(D_STATE // width)]

    for j, rs in enumerate(sub_rows):
        for re_l, im_l in chunks:
            if nb == SUBLANES:
                lam_re, lam_im = lam_ref[:, re_l], lam_ref[:, im_l]
            else:
                lam_re = jnp.broadcast_to(lam_ref[0:1, re_l], (nb, width))
                lam_im = jnp.broadcast_to(lam_ref[0:1, im_l], (nb, width))
            s_re, s_im = st_ref[:, re_l], st_ref[:, im_l]
            for t in range(j * sub, (j + 1) * sub):
                tr = slice(t * nb, (t + 1) * nb)
                s_re, s_im = (lam_re * s_re - lam_im * s_im + bu_scr[tr, re_l],
                              lam_re * s_im + lam_im * s_re + bu_scr[tr, im_l])
                bu_scr[tr, re_l] = s_re
                bu_scr[tr, im_l] = s_im
            st_ref[:, re_l] = s_re
            st_ref[:, im_l] = s_im

        ys = []
        for kt in range(4):
            xc = jnp.concatenate([bu_scr[rs, kt * 512:(kt + 1) * 512],
                                  bu_scr[rs, D_STATE + kt * 512:D_STATE + (kt + 1) * 512]], axis=1).astype(BF16)
            y = jnp.dot(xc, cc_ref[kt], preferred_element_type=F32)
            ys.append(y + d_ref[:, kt * LANES:(kt + 1) * LANES] * u_scr[kt, rs, :])
        y = jnp.concatenate(ys, axis=1)
        g = jax.nn.gelu(y)
        gl = jnp.dot(g.astype(BF16), wglu_ref[...], preferred_element_type=F32) + bglu_ref[...]
        z = jnp.concatenate([u_scr[4 + s, rs, :] for s in range(4)], axis=1)
        y = gl[:, 0:D_SSM] * _sigmoid(gl[:, D_SSM:2 * D_SSM]) * (z * _sigmoid(z))
        y = _rms_scale(y) * g_ref[...]
        for s in range(4):
            ys_scr[s, rs, :] = y[:, s * LANES:(s + 1) * LANES]

    for s in range(4):
        if prompt:
            for b in range(nb):
                ys_ref[s, b] = ys_scr[s, pl.ds(b, tt, stride=nb), :]
        else:
            for t in range(tt):
                ys_ref[s, pl.ds(t, nb, stride=tt), :] = ys_scr[s, t * nb:(t + 1) * nb, :]


def _ssm(xz, h0, lam, wb, cc, d_vec, w_glu_bf, b_glu, norm_g, *, nb, t_total, tt):
    rows = nb * tt
    const = lambda shape: pl.BlockSpec(shape, lambda i: (0,) * len(shape))
    if t_total > tt:
        xz_in = xz.reshape(8, nb, t_total, LANES)
        xz_spec = pl.BlockSpec((8, nb, tt, LANES), lambda i: (0, 0, i, 0))
        ys_spec = pl.BlockSpec((4, nb, tt, LANES), lambda i: (0, 0, i, 0))
        ys_shape = jax.ShapeDtypeStruct((4, nb, t_total, LANES), F32)
    else:
        xz_in = xz
        xz_spec = const((8, rows, LANES))
        ys_spec = const((4, rows, LANES))
        ys_shape = jax.ShapeDtypeStruct((4, rows, LANES), F32)
    ys, st = pl.pallas_call(
        functools.partial(_ssm_kernel, nb=nb, tt=tt),
        grid=(t_total // tt,),
        in_specs=[xz_spec, const((nb, 2 * D_STATE)), const((SUBLANES, 2 * D_STATE)),
                  const((4, 128, 1024)), const((4, 1024, 128)), const((1, D_SSM)),
                  const((D_SSM, 2 * D_SSM)), const((1, 2 * D_SSM)), const((1, D_SSM))],
        out_specs=[ys_spec, const((nb, 2 * D_STATE))],
        out_shape=[ys_shape, jax.ShapeDtypeStruct((nb, 2 * D_STATE), F32)],
        scratch_shapes=[pltpu.VMEM((8, rows, LANES), F32),
                        pltpu.VMEM((rows, 2 * D_STATE), F32),
                        pltpu.VMEM((4, rows, LANES), F32)],
        compiler_params=pltpu.CompilerParams(dimension_semantics=("arbitrary",), vmem_limit_bytes=VMEM_LIMIT),
        name="ssm_prompt" if t_total > tt else "ssm_sample",
    )(xz_in, h0, lam, wb, cc, d_vec.reshape(1, D_SSM), w_glu_bf, b_glu.reshape(1, 2 * D_SSM),
      norm_g.reshape(1, D_SSM))
    return ys.reshape(4, nb * t_total, LANES), st


MAX_GROUP_BLOCKS = 6


def _largest_divisor(n, cap):
    return max(g for g in range(1, cap + 1) if n % g == 0)


def _band_blocks(qs, ks, vs, bias):
    low = lax.broadcasted_iota(jnp.int32, (BAND, LANES), 1) < HEAD_DIM
    nt = (((1,), (1,)), ((), ()))
    stacked = []
    for q in qs:
        q = q * (HEAD_DIM ** -0.5)
        stacked.append(jnp.concatenate([jnp.where(low, q, 0.0), jnp.where(low, 0.0, q)], axis=0).astype(BF16))
    ss = [lax.dot_general(q, k.astype(BF16), nt, preferred_element_type=F32) + bias for q, k in zip(stacked, ks)]
    ms = [jnp.max(s, axis=-1, keepdims=True) for s in ss]
    ps = [jnp.exp(s - m) for s, m in zip(ss, ms)]
    ls = [jnp.sum(p, axis=-1, keepdims=True) for p in ps]
    outs = [jnp.dot(p.astype(BF16), v.astype(BF16), preferred_element_type=F32) for p, v in zip(ps, vs)]
    wide = lambda a: jnp.broadcast_to(a, (BAND, a.shape[1] if a.shape[1] > 1 else LANES))
    return [tuple(jnp.where(low, wide(a[0:BAND]), wide(a[BAND:2 * BAND])) for a in oml)
            for oml in zip(outs, ms, ls)]


def _prompt_attn_kernel(q_ref, k_ref, v_ref, z_ref, o_ref, o_scr, m_scr, l_scr, band_scr, causal_scr):
    t_total = q_ref.shape[0]
    for scr, kwin in ((band_scr, 2 * BAND), (causal_scr, BAND)):
        qq = lax.broadcasted_iota(jnp.int32, (2 * BAND, kwin), 0) & (BAND - 1)
        kk = lax.broadcasted_iota(jnp.int32, (2 * BAND, kwin), 1)
        valid = ((kk >= qq) & (kk <= qq + BAND)) if kwin > BAND else (kk <= qq)
        scr[...] = jnp.where(valid, 0.0, NEG_INF)

    def run_blocks(pidx, dil, starts, kwin, bias_ref):
        rows = []
        for q0, k0 in starts:
            if dil == 1:
                rows.append((pl.ds(pl.multiple_of(q0, BAND), BAND), pl.ds(pl.multiple_of(k0, BAND), kwin)))
            else:
                rows.append((pl.ds(q0, BAND, stride=dil), pl.ds(k0, kwin, stride=dil)))
        res = _band_blocks([q_ref[qr, :] for qr, _ in rows], [k_ref[kr, :] for _, kr in rows],
                           [v_ref[kr, :] for _, kr in rows], bias_ref[...])
        for (qr, _), (o, m, l) in zip(rows, res):
            o_scr[pidx, qr, :] = o
            m_scr[pidx, qr, :] = m
            l_scr[pidx, qr, :] = l

    for pidx, (window, dil) in enumerate(DILATED_PATTERNS):
        nblk = t_total // dil // BAND

        group = _largest_divisor(dil, MAX_GROUP_BLOCKS)
        def first(g, _, pidx=pidx, dil=dil, group=group):
            run_blocks(pidx, dil, [(g * group + r,) * 2 for r in range(group)], BAND, causal_scr)
            return 0
        lax.fori_loop(0, dil // group, first, 0)

        n_rest = dil * (nblk - 1)
        if n_rest:
            group = _largest_divisor(n_rest, MAX_GROUP_BLOCKS)
            def rest(g, _, pidx=pidx, dil=dil, nblk=nblk, group=group):
                starts = []
                for r in range(group):
                    idx = g * group + r
                    rho = idx // (nblk - 1)
                    i = 1 + idx % (nblk - 1)
                    starts.append((rho + dil * BAND * i, rho + dil * BAND * (i - 1)))
                run_blocks(pidx, dil, starts, 2 * BAND, band_scr)
                return 0
            lax.fori_loop(0, n_rest // group, rest, 0)

    chunk = 256
    def combine(c, _):
        rows = pl.ds(pl.multiple_of(c * chunk, chunk), chunk)
        ms = [m_scr[p, rows, :] for p in range(3)]
        big = jnp.maximum(jnp.maximum(ms[0], ms[1]), ms[2])
        ws = [jnp.exp(m - big) for m in ms]
        num = ws[0] * o_scr[0, rows, :] + ws[1] * o_scr[1, rows, :] + ws[2] * o_scr[2, rows, :]
        den = ws[0] * l_scr[0, rows, :] + ws[1] * l_scr[1, rows, :] + ws[2] * l_scr[2, rows, :]
        z = z_ref[rows, :]
        o_ref[rows, :] = (num / den) * (z * _sigmoid(z))
        return 0
    lax.fori_loop(0, t_total // chunk, combine, 0)


def _prompt_attention(q, k, v, z, *, batch, t_total):
    shp = (batch, t_total, D_ATTN)
    spec = pl.BlockSpec((None, t_total, LANES), lambda b, p: (b, 0, p))
    out = pl.pallas_call(
        _prompt_attn_kernel,
        grid=(batch, D_ATTN // LANES),
        in_specs=[spec] * 4,
        out_specs=spec,
        out_shape=jax.ShapeDtypeStruct(shp, F32),
        scratch_shapes=[pltpu.VMEM((3, t_total, LANES), F32)] * 3
                       + [pltpu.VMEM((2 * BAND, 2 * BAND), F32), pltpu.VMEM((2 * BAND, BAND), F32)],
        compiler_params=pltpu.CompilerParams(dimension_semantics=("parallel", "parallel"),
                                             vmem_limit_bytes=VMEM_LIMIT),
        name="prompt_attention",
    )(q.reshape(shp), k.reshape(shp), v.reshape(shp), z.reshape(shp))
    return out.reshape(batch * t_total, D_ATTN)


NEAR = 512
SEQS_PER_STEP = 2


def _sample_attn_kernel(q_ref, kn_ref, vn_ref, z_ref, kc_ref, vc_ref, o_ref, *, tt):
    assert 2 * tt == SUBLANES
    scale = HEAD_DIM ** -0.5
    nt = (((1,), (1,)), ((), ()))
    row8 = lax.broadcasted_iota(jnp.int32, (SUBLANES, LANES), 0)
    lane8 = lax.broadcasted_iota(jnp.int32, (SUBLANES, LANES), 1)
    top = row8 < tt
    own_head = top == (lane8 < HEAD_DIM)

    def dup(x, seq):
        rolled = pltpu.roll(x, tt, axis=0)
        return jnp.where(top, x, rolled) if seq == 0 else jnp.where(top, rolled, x)

    jq = lambda shape: lax.broadcasted_iota(jnp.int32, shape, 0) & (tt - 1)
    d_full = WINDOW_MAX + jq((SUBLANES, WINDOW_MAX)) - lax.broadcasted_iota(jnp.int32, (SUBLANES, WINDOW_MAX), 1)
    d_near = NEAR + jq((SUBLANES, NEAR)) - lax.broadcasted_iota(jnp.int32, (SUBLANES, NEAR), 1)
    d_new = jq((SUBLANES, SUBLANES)) - lax.broadcasted_iota(jnp.int32, (SUBLANES, SUBLANES), 1)
    new_key = lax.broadcasted_iota(jnp.int32, (SUBLANES, SUBLANES), 1) < tt
    ok_main, ok_new = [], []
    for window, dil in DILATED_PATTERNS:
        d = d_near if window <= NEAR else d_full
        ok_main.append(((d & (dil - 1)) == 0) & (d <= window))
        ok_new.append(new_key & (d_new >= 0) & ((d_new & (dil - 1)) == 0))

    z = z_ref[...]
    gate = z * _sigmoid(z)
    for pr in range(D_ATTN // LANES):
        lanes = slice(pr * LANES, (pr + 1) * LANES)
        halves = []
        for seq in range(SEQS_PER_STEP):
            q = jnp.where(own_head, dup(q_ref[:, lanes], seq) * scale, 0.0).astype(BF16)
            kn = dup(kn_ref[:, lanes], seq).astype(BF16)
            vn = dup(vn_ref[:, lanes], seq).astype(BF16).astype(F32)
            s_main = jnp.dot(q, kc_ref[seq, lanes, :].astype(BF16), preferred_element_type=F32)
            s_new = lax.dot_general(q, kn, nt, preferred_element_type=F32)

            stats = []
            for (window, dil), okm, okn in zip(DILATED_PATTERNS, ok_main, ok_new):
                sm = jnp.where(okm, s_main[:, WINDOW_MAX - NEAR:] if window <= NEAR else s_main, NEG_INF)
                sn = jnp.where(okn, s_new, NEG_INF)
                m = jnp.maximum(jnp.max(sm, axis=-1, keepdims=True), jnp.max(sn, axis=-1, keepdims=True))
                pm = jnp.exp(sm - m)
                pn = jnp.exp(sn - m)
                l = jnp.sum(pm, axis=-1, keepdims=True) + jnp.sum(pn, axis=-1, keepdims=True)
                stats.append((m, l, pm, pn))
            big = functools.reduce(jnp.maximum, [s[0] for s in stats])
            ws = [jnp.exp(s[0] - big) for s in stats]
            den = functools.reduce(jnp.add, [w * s[1] for w, s in zip(ws, stats)])
            p_new = functools.reduce(jnp.add, [w * s[3] for w, s in zip(ws, stats)])
            p_near = functools.reduce(jnp.add, [w * s[2] for w, s in zip(ws, stats) if s[2].shape[1] == NEAR])
            p_full = functools.reduce(jnp.add, [w * s[2] for w, s in zip(ws, stats) if s[2].shape[1] != NEAR])
            p_main = jnp.concatenate([p_full[:, 0:WINDOW_MAX - NEAR], p_full[:, WINDOW_MAX - NEAR:] + p_near], axis=1)

            acc = lax.dot_general(p_main.astype(BF16), vc_ref[seq, lanes, :].astype(BF16), nt,
                                  preferred_element_type=F32)
            p_new = p_new.astype(BF16).astype(F32)
            for j in range(tt):
                acc = acc + p_new[:, j:j + 1] * vn[j:j + 1, :]
            acc = acc / den
            halves.append(jnp.where(lane8 < HEAD_DIM, acc, pltpu.roll(acc, tt, axis=0)))
        both = jnp.where(top, halves[0], pltpu.roll(halves[1], tt, axis=0))
        o_ref[:, lanes] = both * gate[:, lanes]


def _sample_attention(q, k_new, v_new, z, cache_kt, cache_vt, *, batch, tt):
    assert cache_kt.shape == (batch, D_ATTN, WINDOW_MAX) and batch % SEQS_PER_STEP == 0
    tok = pl.BlockSpec((SEQS_PER_STEP * tt, D_ATTN), lambda i: (i, 0))
    cache = pl.BlockSpec((SEQS_PER_STEP, D_ATTN, WINDOW_MAX), lambda i: (i, 0, 0))
    return pl.pallas_call(
        functools.partial(_sample_attn_kernel, tt=tt),
        grid=(batch // SEQS_PER_STEP,),
        in_specs=[tok, tok, tok, tok, cache, cache],
        out_specs=tok,
        out_shape=jax.ShapeDtypeStruct((batch * tt, D_ATTN), F32),
        compiler_params=pltpu.CompilerParams(dimension_semantics=("parallel",), vmem_limit_bytes=VMEM_LIMIT),
        name="sample_attention",
    )(q, k_new, v_new, z, cache_kt, cache_vt)


def _out_proj_kernel(x_ref, ys_ref, ya_ref, ga_ref, w_ref, gf_ref, y_ref):
    ys = jnp.concatenate([ys_ref[s] for s in range(4)], axis=1)
    ya = _rms_scale(ya_ref[...]) * ga_ref[...]
    mix = jnp.concatenate([ys, ya], axis=1).astype(BF16)
    h = x_ref[...] + jnp.dot(mix, w_ref[...], preferred_element_type=F32)
    y_ref[...] = _rms_scale(h) * gf_ref[...]


def _out_proj(x, ys, ya, norm_attn_g, w_out_bf, final_g):
    rows = x.shape[0]
    row_spec = lambda width: pl.BlockSpec((ROW_TILE, width), lambda i: (i, 0))
    const = lambda shape: pl.BlockSpec(shape, lambda i: (0,) * len(shape))
    return pl.pallas_call(
        _out_proj_kernel,
        grid=(rows // ROW_TILE,),
        in_specs=[row_spec(D_MODEL), pl.BlockSpec((4, ROW_TILE, LANES), lambda i: (0, i, 0)), row_spec(D_ATTN),
                  const((1, D_ATTN)), const((D_MODEL, D_MODEL)), const((1, D_MODEL))],
        out_specs=row_spec(D_MODEL),
        out_shape=jax.ShapeDtypeStruct((rows, D_MODEL), F32),
        compiler_params=pltpu.CompilerParams(dimension_semantics=("parallel",), vmem_limit_bytes=VMEM_LIMIT),
        name="out_proj",
    )(x, ys, ya, norm_attn_g.reshape(1, D_ATTN), w_out_bf, final_g.reshape(1, D_MODEL))


def kernel(x_prompt, x_sample, cache_k, cache_v, state_ssm_re, state_ssm_im, norm_in_g, w_in,
           ssm_A_re, ssm_A_im, ssm_log_dt, ssm_B_re, ssm_B_im, ssm_C_re, ssm_C_im, ssm_D,
           w_glu, b_glu, norm_ssm_g, norm_attn_g, w_out, final_norm_g):
    depth = w_in.shape[0]
    assert depth == 1
    batch, seq, _ = x_prompt.shape
    dec_batch, dec_seq, _ = x_sample.shape
    assert batch == SUBLANES and seq == WINDOW_MAX and seq % SSM_STEPS == 0

    lam, bb_re, bb_im = _ssm_prep(ssm_A_re[0], ssm_A_im[0], ssm_log_dt[0], ssm_B_re[0], ssm_B_im[0])
    wb, cc = _block_diag_weights(bb_re, bb_im, ssm_C_re[0], ssm_C_im[0])
    w_in_bf = w_in[0].astype(BF16)
    w_glu_bf = w_glu[0].astype(BF16)
    w_out_bf = w_out[0].astype(BF16)

    def layer(x, h0, attend, nb, t_total, tt, seq_len=None):
        xz, q, k, v, za, *kv_t = _in_proj(x, norm_in_g[0], w_in_bf, seq_len=seq_len)
        ys, st = _ssm(xz, h0, lam, wb, cc, ssm_D[0], w_glu_bf, b_glu[0], norm_ssm_g[0], nb=nb, t_total=t_total, tt=tt)
        ya = attend(q, k, v, za)
        y = _out_proj(x, ys, ya, norm_attn_g[0], w_out_bf, final_norm_g)
        return y, (kv_t or (k, v)), st

    xp = x_prompt.reshape(batch * seq, D_MODEL)
    yp, (kpt, vpt), stp = layer(xp, jnp.zeros((batch, 2 * D_STATE), F32),
                                functools.partial(_prompt_attention, batch=batch, t_total=seq),
                                batch, seq, SSM_STEPS, seq_len=seq)

    xs = x_sample.reshape(dec_batch * dec_seq, D_MODEL)
    h0 = jnp.concatenate([state_ssm_re[0].reshape(dec_batch, D_STATE),
                          state_ssm_im[0].reshape(dec_batch, D_STATE)], axis=1)
    chan_major = lambda c: c[0].transpose(0, 2, 3, 1).reshape(dec_batch, D_ATTN, WINDOW_MAX)
    ysm, (ksm, vsm), sts = layer(xs, h0,
                                 functools.partial(_sample_attention, cache_kt=chan_major(cache_k),
                                                   cache_vt=chan_major(cache_v), batch=dec_batch, tt=dec_seq),
                                 dec_batch, dec_seq, dec_seq)

    heads = lambda a, n, t: a.reshape(1, n, t, N_HEADS, HEAD_DIM)
    heads_t = lambda a, n, t: a.reshape(n, N_HEADS, HEAD_DIM, t).transpose(0, 3, 1, 2)[None]
    state = lambda s, n, part: s[:, part * D_STATE:(part + 1) * D_STATE].reshape(1, n, N_SSM_GROUPS, SSM_STATE)
    return (yp.reshape(batch, seq, D_MODEL), ysm.reshape(dec_batch, dec_seq, D_MODEL),
            heads_t(kpt, batch, seq), heads_t(vpt, batch, seq), state(stp, batch, 0), state(stp, batch, 1),
            heads(ksm, dec_batch, dec_seq), heads(vsm, dec_batch, dec_seq),
            state(sts, dec_batch, 0), state(sts, dec_batch, 1))
```

```python
import functools

import jax
import jax.numpy as jnp
from jax import lax
from jax.experimental import pallas as pl
from jax.experimental.pallas import tpu as pltpu

F32 = jnp.float32
BF16 = jnp.bfloat16

D_MODEL = 1024
D_SSM = 512
D_ATTN = 512
SSM_GROUP = 16
N_SSM_GROUPS = D_SSM // SSM_GROUP
SSM_STATE = 64
D_STATE = N_SSM_GROUPS * SSM_STATE
HEAD_DIM = 64
N_HEADS = D_ATTN // HEAD_DIM
D_IN_PROJ = 2 * D_SSM + 4 * D_ATTN
RMS_EPS = 1e-6
DILATED_PATTERNS = ((128, 1), (512, 4), (2048, 16))
BAND = 128
WINDOW_MAX = 2048
LANES = 128
SUBLANES = 8
VMEM_LIMIT = 56 * 1024 * 1024
ROW_TILE = 512
SSM_STEPS = 128
SSM_SUB_STEPS = 32
NEG_INF = float("-inf")


def _sigmoid(x):
    return 1.0 / (1.0 + jnp.exp(-x))


def _rms_scale(x):
    return x * lax.rsqrt(jnp.mean(x * x, axis=-1, keepdims=True) + RMS_EPS)


def _ssm_prep_kernel(are_ref, aim_ref, ldt_ref, bre_ref, bim_ref, lam_ref, bbre_ref, bbim_ref):
    a_re = are_ref[...]
    a_im = aim_ref[...]
    dt = jnp.exp(ldt_ref[...])
    mag = jnp.exp(a_re * dt)
    ang = a_im * dt
    lam_re = mag * jnp.cos(ang)
    lam_im = mag * jnp.sin(ang)
    den = a_re * a_re + a_im * a_im
    f_re = ((lam_re - 1.0) * a_re + lam_im * a_im) / den
    f_im = (lam_im * a_re - (lam_re - 1.0) * a_im) / den
    lam_ref[:, 0:D_STATE] = jnp.broadcast_to(lam_re, (SUBLANES, D_STATE))
    lam_ref[:, D_STATE:2 * D_STATE] = jnp.broadcast_to(lam_im, (SUBLANES, D_STATE))
    b_re = bre_ref[...]
    b_im = bim_ref[...]
    bbre_ref[...] = f_re * b_re - f_im * b_im
    bbim_ref[...] = f_re * b_im + f_im * b_re


def _ssm_prep(a_re, a_im, log_dt, b_re, b_im):
    flat = lambda a: a.reshape(1, D_STATE)
    ldt = jnp.repeat(log_dt, SSM_STATE).reshape(1, D_STATE)
    bt = lambda b: b.transpose(2, 0, 1).reshape(SSM_GROUP, D_STATE)
    return pl.pallas_call(
        _ssm_prep_kernel,
        out_shape=(jax.ShapeDtypeStruct((SUBLANES, 2 * D_STATE), F32),
                   jax.ShapeDtypeStruct((SSM_GROUP, D_STATE), F32),
                   jax.ShapeDtypeStruct((SSM_GROUP, D_STATE), F32)),
        name="ssm_prep",
    )(flat(a_re), flat(a_im), ldt, bt(b_re), bt(b_im))


def _block_diag_weights(bb_re, bb_im, c_re, c_im):
    eye = jnp.eye(8, dtype=F32)
    def place_b(bb):
        b4 = bb.reshape(SSM_GROUP, 4, 8, SSM_STATE).transpose(1, 2, 0, 3)
        return jnp.einsum("kgcp,gh->kgchp", b4, eye).reshape(4, 128, 512)
    def place_c(c):
        c4 = c.reshape(4, 8, SSM_GROUP, SSM_STATE)
        return jnp.einsum("kgcp,gh->khpgc", c4, eye).reshape(4, 512, 128)
    wb = jnp.concatenate([place_b(bb_re), place_b(bb_im)], axis=2).astype(BF16)
    cc = jnp.concatenate([place_c(c_re), -place_c(c_im)], axis=1).astype(BF16)
    return wb, cc


def _in_proj_kernel(x_ref, g_ref, w_ref, xz_ref, q_ref, k_ref, v_ref, za_ref, *t_refs):
    xn = (_rms_scale(x_ref[...]) * g_ref[...]).astype(BF16)
    outs = (None, None, q_ref, k_ref, v_ref, za_ref)
    for j in range(D_IN_PROJ // 512):
        r = jnp.dot(xn, w_ref[:, j * 512:(j + 1) * 512], preferred_element_type=F32)
        if j in (1, 5):
            r = r * _sigmoid(r)
        if j < 2:
            for s in range(4):
                xz_ref[j * 4 + s] = r[:, s * LANES:(s + 1) * LANES]
        else:
            outs[j][...] = r
            if t_refs and j in (3, 4):
                t_refs[j - 3][...] = r.T


def _in_proj(x, norm_g, w_in_bf, *, seq_len=None):
    rows = x.shape[0]
    row_spec = lambda width: pl.BlockSpec((ROW_TILE, width), lambda i: (i, 0))
    out_specs = [pl.BlockSpec((8, ROW_TILE, LANES), lambda i: (0, i, 0))] + [row_spec(512)] * 4
    out_shape = [jax.ShapeDtypeStruct((8, rows, LANES), F32)] + [jax.ShapeDtypeStruct((rows, 512), F32)] * 4
    if seq_len is not None:
        per_seq = seq_len // ROW_TILE
        out_specs += [pl.BlockSpec((None, 512, ROW_TILE), lambda i: (i // per_seq, 0, i % per_seq))] * 2
        out_shape += [jax.ShapeDtypeStruct((rows // seq_len, 512, seq_len), F32)] * 2
    return pl.pallas_call(
        _in_proj_kernel,
        grid=(rows // ROW_TILE,),
        in_specs=[row_spec(D_MODEL),
                  pl.BlockSpec((1, D_MODEL), lambda i: (0, 0)),
                  pl.BlockSpec((D_MODEL, D_IN_PROJ), lambda i: (0, 0))],
        out_specs=out_specs,
        out_shape=out_shape,
        compiler_params=pltpu.CompilerParams(dimension_semantics=("parallel",), vmem_limit_bytes=VMEM_LIMIT),
        name="in_proj",
    )(x, norm_g.reshape(1, D_MODEL), w_in_bf)


def _ssm_kernel(xz_ref, h0_ref, lam_ref, wb_ref, cc_ref, d_ref, wglu_ref, bglu_ref, g_ref,
                ys_ref, st_ref, u_scr, bu_scr, ys_scr, *, nb, tt):
    rows = nb * tt
    prompt = xz_ref.ndim == 4

    @pl.when(pl.program_id(0) == 0)
    def _():
        st_ref[...] = h0_ref[...]

    for s in range(8):
        if prompt:
            for b in range(nb):
                u_scr[s, pl.ds(b, tt, stride=nb), :] = xz_ref[s, b]
        else:
            for t in range(tt):
                u_scr[s, t * nb:(t + 1) * nb, :] = xz_ref[s, pl.ds(t, nb, stride=tt), :]

    sub = min(tt, SSM_SUB_STEPS)
    sub_rows = [slice(j * sub * nb, (j + 1) * sub * nb) for j in range(tt // sub)]

    for rs in sub_rows:
        for kt in range(4):
            r = jnp.dot(u_scr[kt, rs, :].astype(BF16), wb_ref[kt], preferred_element_type=F32)
            bu_scr[rs, kt * 512:(kt + 1) * 512] = r[:, 0:512]
            bu_scr[rs, D_STATE + kt * 512:D_STATE + (kt + 1) * 512] = r[:, 512:1024]

    width = (SUBLANES * LANES * 8) // nb if nb <= 64 else LANES
    chunks = [(slice(c * width, (c + 1) * width), slice(D_STATE + c * width, D_STATE + (c + 1) * width))
              for c in range(D_STATE // width)]

    for j, rs in enumerate(sub_rows):
        for re_l, im_l in chunks:
            if nb == SUBLANES:
                lam_re, lam_im = lam_ref[:, re_l], lam_ref[:, im_l]
            else:
                lam_re = jnp.broadcast_to(lam_ref[0:1, re_l], (nb, width))
                lam_im = jnp.broadcast_to(lam_ref[0:1, im_l], (nb, width))
            s_re, s_im = st_ref[:, re_l], st_ref[:, im_l]
            for t in range(j * sub, (j + 1) * sub):
                tr = slice(t * nb, (t + 1) * nb)
                s_re, s_im = (lam_re * s_re - lam_im * s_im + bu_scr[tr, re_l],
                              lam_re * s_im + lam_im * s_re + bu_scr[tr, im_l])
                bu_scr[tr, re_l] = s_re
                bu_scr[tr, im_l] = s_im
            st_ref[:, re_l] = s_re
            st_ref[:, im_l] = s_im

        ys = []
        for kt in range(4):
            xc = jnp.concatenate([bu_scr[rs, kt * 512:(kt + 1) * 512],
                                  bu_scr[rs, D_STATE + kt * 512:D_STATE + (kt + 1) * 512]], axis=1).astype(BF16)
            y = jnp.dot(xc, cc_ref[kt], preferred_element_type=F32)
            ys.append(y + d_ref[:, kt * LANES:(kt + 1) * LANES] * u_scr[kt, rs, :])
        y = jnp.concatenate(ys, axis=1)
        g = jax.nn.gelu(y)
        gl = jnp.dot(g.astype(BF16), wglu_ref[...], preferred_element_type=F32) + bglu_ref[...]
        gate = jnp.concatenate([u_scr[4 + s, rs, :] for s in range(4)], axis=1)
        y = gl[:, 0:D_SSM] * _sigmoid(gl[:, D_SSM:2 * D_SSM]) * gate
        y = _rms_scale(y) * g_ref[...]
        for s in range(4):
            ys_scr[s, rs, :] = y[:, s * LANES:(s + 1) * LANES]

    for s in range(4):
        if prompt:
            for b in range(nb):
                ys_ref[s, b] = ys_scr[s, pl.ds(b, tt, stride=nb), :].astype(ys_ref.dtype)
        else:
            for t in range(tt):
                ys_ref[s, pl.ds(t, nb, stride=tt), :] = ys_scr[s, t * nb:(t + 1) * nb, :]


def _ssm(xz, h0, lam, wb, cc, d_vec, w_glu_bf, b_glu, norm_g, *, nb, t_total, tt):
    rows = nb * tt
    const = lambda shape: pl.BlockSpec(shape, lambda i: (0,) * len(shape))
    if t_total > tt:
        xz_in = xz.reshape(8, nb, t_total, LANES)
        xz_spec = pl.BlockSpec((8, nb, tt, LANES), lambda i: (0, 0, i, 0))
        ys_spec = pl.BlockSpec((4, nb, tt, LANES), lambda i: (0, 0, i, 0))
        ys_shape = jax.ShapeDtypeStruct((4, nb, t_total, LANES), BF16)
    else:
        xz_in = xz
        xz_spec = const((8, rows, LANES))
        ys_spec = const((4, rows, LANES))
        ys_shape = jax.ShapeDtypeStruct((4, rows, LANES), F32)
    ys, st = pl.pallas_call(
        functools.partial(_ssm_kernel, nb=nb, tt=tt),
        grid=(t_total // tt,),
        in_specs=[xz_spec, const((nb, 2 * D_STATE)), const((SUBLANES, 2 * D_STATE)),
                  const((4, 128, 1024)), const((4, 1024, 128)), const((1, D_SSM)),
                  const((D_SSM, 2 * D_SSM)), const((1, 2 * D_SSM)), const((1, D_SSM))],
        out_specs=[ys_spec, const((nb, 2 * D_STATE))],
        out_shape=[ys_shape, jax.ShapeDtypeStruct((nb, 2 * D_STATE), F32)],
        scratch_shapes=[pltpu.VMEM((8, rows, LANES), F32),
                        pltpu.VMEM((rows, 2 * D_STATE), F32),
                        pltpu.VMEM((4, rows, LANES), F32)],
        compiler_params=pltpu.CompilerParams(dimension_semantics=("arbitrary",), vmem_limit_bytes=VMEM_LIMIT),
        name="ssm_prompt" if t_total > tt else "ssm_sample",
    )(xz_in, h0, lam, wb, cc, d_vec.reshape(1, D_SSM), w_glu_bf, b_glu.reshape(1, 2 * D_SSM),
      norm_g.reshape(1, D_SSM))
    return ys.reshape(4, nb * t_total, LANES), st


GROUP_BLOCKS = 3


def _band_scores(qs, ks, biases):
    low = lax.broadcasted_iota(jnp.int32, (BAND, LANES), 1) < HEAD_DIM
    nt = (((1,), (1,)), ((), ()))
    stacked = []
    for q in qs:
        q = q * (HEAD_DIM ** -0.5)
        stacked.append(jnp.concatenate([jnp.where(low, q, 0.0), jnp.where(low, 0.0, q)], axis=0).astype(BF16))
    return [lax.dot_general(q, k.astype(BF16), nt, preferred_element_type=F32) + bias
            for q, k, bias in zip(stacked, ks, biases)]


def _band_softmax_pv(ss, vs):
    low = lax.broadcasted_iota(jnp.int32, (BAND, LANES), 1) < HEAD_DIM
    ms = [jnp.max(s, axis=-1, keepdims=True) for s in ss]
    ps = [jnp.exp(s - m).astype(BF16) for s, m in zip(ss, ms)]
    outs = [jnp.dot(p, jnp.concatenate([v.astype(BF16), jnp.ones(v.shape, BF16)], axis=1),
                    preferred_element_type=F32) for p, v in zip(ps, vs)]
    wide = lambda a: jnp.broadcast_to(a, (BAND, a.shape[1] if a.shape[1] > 1 else LANES))
    pick = lambda a: jnp.where(low, wide(a[0:BAND]), wide(a[BAND:2 * BAND]))
    return [(pick(o[:, 0:LANES]), pick(m), pick(o[:, LANES:2 * LANES])) for o, m in zip(outs, ms)]


def _prompt_attn_kernel(q_ref, k_ref, v_ref, gate_ref, o_ref, o_scr, m_scr, l_scr, band_scr, causal_scr):
    t_total = q_ref.shape[0]
    for scr, kwin in ((band_scr, 2 * BAND), (causal_scr, BAND)):
        qq = lax.broadcasted_iota(jnp.int32, (2 * BAND, kwin), 0) & (BAND - 1)
        kk = lax.broadcasted_iota(jnp.int32, (2 * BAND, kwin), 1)
        valid = ((kk >= qq) & (kk <= qq + BAND)) if kwin > BAND else (kk <= qq)
        scr[...] = jnp.where(valid, 0.0, NEG_INF)

    blocks = []
    for pidx, (window, dil) in enumerate(DILATED_PATTERNS):
        nblk = t_total // dil // BAND
        rows = (lambda r0, n, dil=dil: pl.ds(r0, n, stride=dil) if dil > 1 else pl.ds(r0, n))
        for rho in range(dil):
            blocks.append((pidx, rows(rho, BAND), rows(rho, BAND), causal_scr))
            blocks += [(pidx, rows(rho + dil * BAND * i, BAND), rows(rho + dil * BAND * (i - 1), 2 * BAND), band_scr)
                       for i in range(1, nblk)]

    def scores(group):
        return _band_scores([q_ref[qr, :] for _, qr, _, _ in group], [k_ref[kr, :] for _, _, kr, _ in group],
                            [bias[...] for _, _, _, bias in group])

    def finish(group, ss):
        res = _band_softmax_pv(ss, [v_ref[kr, :] for _, _, kr, _ in group])
        for (pidx, qr, _, _), (o, m, l) in zip(group, res):
            o_scr[pidx, qr, :] = o
            m_scr[pidx, qr, :] = m
            l_scr[pidx, qr, :] = l

    groups = [blocks[g:g + GROUP_BLOCKS] for g in range(0, len(blocks), GROUP_BLOCKS)]
    pending = None
    for group in groups:
        ss = scores(group)
        if pending is not None:
            finish(*pending)
        pending = (group, ss)
    finish(*pending)

    chunk = 256
    def combine(c, _):
        rows = pl.ds(pl.multiple_of(c * chunk, chunk), chunk)
        ms = [m_scr[p, rows, :] for p in range(3)]
        big = jnp.maximum(jnp.maximum(ms[0], ms[1]), ms[2])
        ws = [jnp.exp(m - big) for m in ms]
        num = ws[0] * o_scr[0, rows, :] + ws[1] * o_scr[1, rows, :] + ws[2] * o_scr[2, rows, :]
        den = ws[0] * l_scr[0, rows, :] + ws[1] * l_scr[1, rows, :] + ws[2] * l_scr[2, rows, :]
        o_ref[rows, :] = (num / den) * gate_ref[rows, :]
        return 0
    lax.fori_loop(0, t_total // chunk, combine, 0)


def _prompt_attention(q, k, v, z, *, batch, t_total):
    shp = (batch, t_total, D_ATTN)
    spec = pl.BlockSpec((None, t_total, LANES), lambda b, p: (b, 0, p))
    out = pl.pallas_call(
        _prompt_attn_kernel,
        grid=(batch, D_ATTN // LANES),
        in_specs=[spec] * 4,
        out_specs=spec,
        out_shape=jax.ShapeDtypeStruct(shp, F32),
        scratch_shapes=[pltpu.VMEM((3, t_total, LANES), F32)] * 3
                       + [pltpu.VMEM((2 * BAND, 2 * BAND), F32), pltpu.VMEM((2 * BAND, BAND), F32)],
        compiler_params=pltpu.CompilerParams(dimension_semantics=("parallel", "parallel"),
                                             vmem_limit_bytes=VMEM_LIMIT),
        name="prompt_attention",
    )(q.reshape(shp), k.reshape(shp), v.reshape(shp), z.reshape(shp))
    return out.reshape(batch * t_total, D_ATTN)


NEAR = 512
SEQS_PER_STEP = 2


def _sample_attn_kernel(q_ref, kn_ref, vn_ref, gate_ref, kc_ref, vc_ref, o_ref, *, tt):
    assert 2 * tt == SUBLANES
    scale = HEAD_DIM ** -0.5
    nt = (((1,), (1,)), ((), ()))
    row8 = lax.broadcasted_iota(jnp.int32, (SUBLANES, LANES), 0)
    lane8 = lax.broadcasted_iota(jnp.int32, (SUBLANES, LANES), 1)
    top = row8 < tt
    own_head = top == (lane8 < HEAD_DIM)

    def dup(x, seq):
        rolled = pltpu.roll(x, tt, axis=0)
        return jnp.where(top, x, rolled) if seq == 0 else jnp.where(top, rolled, x)

    jq = lambda shape: lax.broadcasted_iota(jnp.int32, shape, 0) & (tt - 1)
    d_full = WINDOW_MAX + jq((SUBLANES, WINDOW_MAX)) - lax.broadcasted_iota(jnp.int32, (SUBLANES, WINDOW_MAX), 1)
    d_near = NEAR + jq((SUBLANES, NEAR)) - lax.broadcasted_iota(jnp.int32, (SUBLANES, NEAR), 1)
    d_new = jq((SUBLANES, SUBLANES)) - lax.broadcasted_iota(jnp.int32, (SUBLANES, SUBLANES), 1)
    new_key = lax.broadcasted_iota(jnp.int32, (SUBLANES, SUBLANES), 1) < tt
    ok_main, ok_new = [], []
    for window, dil in DILATED_PATTERNS:
        d = d_near if window <= NEAR else d_full
        ok_main.append(((d & (dil - 1)) == 0) & (d <= window))
        ok_new.append(new_key & (d_new >= 0) & ((d_new & (dil - 1)) == 0))

    for pr in range(D_ATTN // LANES):
        lanes = slice(pr * LANES, (pr + 1) * LANES)
        halves = []
        for seq in range(SEQS_PER_STEP):
            q = jnp.where(own_head, dup(q_ref[:, lanes], seq) * scale, 0.0).astype(BF16)
            kn = dup(kn_ref[:, lanes], seq).astype(BF16)
            vn = dup(vn_ref[:, lanes], seq).astype(BF16).astype(F32)
            s_main = jnp.dot(q, kc_ref[seq, lanes, :].astype(BF16), preferred_element_type=F32)
            s_new = lax.dot_general(q, kn, nt, preferred_element_type=F32)

            stats = []
            for (window, dil), okm, okn in zip(DILATED_PATTERNS, ok_main, ok_new):
                sm = jnp.where(okm, s_main[:, WINDOW_MAX - NEAR:] if window <= NEAR else s_main, NEG_INF)
                sn = jnp.where(okn, s_new, NEG_INF)
                m = jnp.maximum(jnp.max(sm, axis=-1, keepdims=True), jnp.max(sn, axis=-1, keepdims=True))
                pm = jnp.exp(sm - m)
                pn = jnp.exp(sn - m)
                l = jnp.sum(pm, axis=-1, keepdims=True) + jnp.sum(pn, axis=-1, keepdims=True)
                stats.append((m, l, pm, pn))
            big = functools.reduce(jnp.maximum, [s[0] for s in stats])
            ws = [jnp.exp(s[0] - big) for s in stats]
            den = functools.reduce(jnp.add, [w * s[1] for w, s in zip(ws, stats)])
            p_new = functools.reduce(jnp.add, [w * s[3] for w, s in zip(ws, stats)])
            p_near = functools.reduce(jnp.add, [w * s[2] for w, s in zip(ws, stats) if s[2].shape[1] == NEAR])
            p_full = functools.reduce(jnp.add, [w * s[2] for w, s in zip(ws, stats) if s[2].shape[1] != NEAR])
            p_main = jnp.concatenate([p_full[:, 0:WINDOW_MAX - NEAR], p_full[:, WINDOW_MAX - NEAR:] + p_near], axis=1)

            acc = lax.dot_general(p_main.astype(BF16), vc_ref[seq, lanes, :].astype(BF16), nt,
                                  preferred_element_type=F32)
            p_new = p_new.astype(BF16).astype(F32)
            for j in range(tt):
                acc = acc + p_new[:, j:j + 1] * vn[j:j + 1, :]
            acc = acc / den
            halves.append(jnp.where(lane8 < HEAD_DIM, acc, pltpu.roll(acc, tt, axis=0)))
        both = jnp.where(top, halves[0], pltpu.roll(halves[1], tt, axis=0))
        o_ref[:, lanes] = both * gate_ref[:, lanes]


def _sample_attention(q, k_new, v_new, z, cache_kt, cache_vt, *, batch, tt):
    assert cache_kt.shape == (batch, D_ATTN, WINDOW_MAX) and batch % SEQS_PER_STEP == 0
    tok = pl.BlockSpec((SEQS_PER_STEP * tt, D_ATTN), lambda i: (i, 0))
    cache = pl.BlockSpec((SEQS_PER_STEP, D_ATTN, WINDOW_MAX), lambda i: (i, 0, 0))
    return pl.pallas_call(
        functools.partial(_sample_attn_kernel, tt=tt),
        grid=(batch // SEQS_PER_STEP,),
        in_specs=[tok, tok, tok, tok, cache, cache],
        out_specs=tok,
        out_shape=jax.ShapeDtypeStruct((batch * tt, D_ATTN), F32),
        compiler_params=pltpu.CompilerParams(dimension_semantics=("parallel",), vmem_limit_bytes=VMEM_LIMIT),
        name="sample_attention",
    )(q, k_new, v_new, z, cache_kt, cache_vt)


def _out_proj_kernel(x_ref, ys_ref, ya_ref, ga_ref, w_ref, gf_ref, y_ref):
    ys = jnp.concatenate([ys_ref[s] for s in range(4)], axis=1).astype(BF16)
    ya = (_rms_scale(ya_ref[...]) * ga_ref[...]).astype(BF16)
    mix = jnp.concatenate([ys, ya], axis=1)
    h = x_ref[...] + jnp.dot(mix, w_ref[...], preferred_element_type=F32)
    y_ref[...] = _rms_scale(h) * gf_ref[...]


def _out_proj(x, ys, ya, norm_attn_g, w_out_bf, final_g):
    rows = x.shape[0]
    row_spec = lambda width: pl.BlockSpec((ROW_TILE, width), lambda i: (i, 0))
    const = lambda shape: pl.BlockSpec(shape, lambda i: (0,) * len(shape))
    return pl.pallas_call(
        _out_proj_kernel,
        grid=(rows // ROW_TILE,),
        in_specs=[row_spec(D_MODEL), pl.BlockSpec((4, ROW_TILE, LANES), lambda i: (0, i, 0)), row_spec(D_ATTN),
                  const((1, D_ATTN)), const((D_MODEL, D_MODEL)), const((1, D_MODEL))],
        out_specs=row_spec(D_MODEL),
        out_shape=jax.ShapeDtypeStruct((rows, D_MODEL), F32),
        compiler_params=pltpu.CompilerParams(dimension_semantics=("parallel",), vmem_limit_bytes=VMEM_LIMIT),
        name="out_proj",
    )(x, ys, ya, norm_attn_g.reshape(1, D_ATTN), w_out_bf, final_g.reshape(1, D_MODEL))


def kernel(x_prompt, x_sample, cache_k, cache_v, state_ssm_re, state_ssm_im, norm_in_g, w_in,
           ssm_A_re, ssm_A_im, ssm_log_dt, ssm_B_re, ssm_B_im, ssm_C_re, ssm_C_im, ssm_D,
           w_glu, b_glu, norm_ssm_g, norm_attn_g, w_out, final_norm_g):
    depth = w_in.shape[0]
    assert depth == 1
    batch, seq, _ = x_prompt.shape
    dec_batch, dec_seq, _ = x_sample.shape
    assert batch == SUBLANES and seq == WINDOW_MAX and seq % SSM_STEPS == 0

    lam, bb_re, bb_im = _ssm_prep(ssm_A_re[0], ssm_A_im[0], ssm_log_dt[0], ssm_B_re[0], ssm_B_im[0])
    wb, cc = _block_diag_weights(bb_re, bb_im, ssm_C_re[0], ssm_C_im[0])
    w_in_bf = w_in[0].astype(BF16)
    w_glu_bf = w_glu[0].astype(BF16)
    w_out_bf = w_out[0].astype(BF16)

    def layer(x, h0, attend, nb, t_total, tt, seq_len=None):
        xz, q, k, v, za, *kv_t = _in_proj(x, norm_in_g[0], w_in_bf, seq_len=seq_len)
        ys, st = _ssm(xz, h0, lam, wb, cc, ssm_D[0], w_glu_bf, b_glu[0], norm_ssm_g[0], nb=nb, t_total=t_total, tt=tt)
        ya = attend(q, k, v, za)
        y = _out_proj(x, ys, ya, norm_attn_g[0], w_out_bf, final_norm_g)
        return y, (kv_t or (k, v)), st

    xp = x_prompt.reshape(batch * seq, D_MODEL)
    yp, (kpt, vpt), stp = layer(xp, jnp.zeros((batch, 2 * D_STATE), F32),
                                functools.partial(_prompt_attention, batch=batch, t_total=seq),
                                batch, seq, SSM_STEPS, seq_len=seq)

    xs = x_sample.reshape(dec_batch * dec_seq, D_MODEL)
    h0 = jnp.concatenate([state_ssm_re[0].reshape(dec_batch, D_STATE),
                          state_ssm_im[0].reshape(dec_batch, D_STATE)], axis=1)
    chan_major = lambda c: c[0].transpose(0, 2, 3, 1).reshape(dec_batch, D_ATTN, WINDOW_MAX)
    ysm, (ksm, vsm), sts = layer(xs, h0,
                                 functools.partial(_sample_attention, cache_kt=chan_major(cache_k),
                                                   cache_vt=chan_major(cache_v), batch=dec_batch, tt=dec_seq),
                                 dec_batch, dec_seq, dec_seq)

    heads = lambda a, n, t: a.reshape(1, n, t, N_HEADS, HEAD_DIM)
    heads_t = lambda a, n, t: a.reshape(n, N_HEADS, HEAD_DIM, t).transpose(0, 3, 1, 2)[None]
    state = lambda s, n, part: s[:, part * D_STATE:(part + 1) * D_STATE].reshape(1, n, N_SSM_GROUPS, SSM_STATE)
    return (yp.reshape(batch, seq, D_MODEL), ysm.reshape(dec_batch, dec_seq, D_MODEL),
            heads_t(kpt, batch, seq), heads_t(vpt, batch, seq), state(stp, batch, 0), state(stp, batch, 1),
            heads(ksm, dec_batch, dec_seq), heads(vsm, dec_batch, dec_seq),
            state(sts, dec_batch, 0), state(sts, dec_batch, 1))
```

```python
import functools

import jax
import jax.numpy as jnp
from jax import lax
from jax.experimental import pallas as pl
from jax.experimental.pallas import tpu as pltpu

F32 = jnp.float32
BF16 = jnp.bfloat16

D_MODEL = 1024
D_SSM = 512
D_ATTN = 512
SSM_GROUP = 16
N_SSM_GROUPS = D_SSM // SSM_GROUP
SSM_STATE = 64
D_STATE = N_SSM_GROUPS * SSM_STATE
HEAD_DIM = 64
N_HEADS = D_ATTN // HEAD_DIM
D_IN_PROJ = 2 * D_SSM + 4 * D_ATTN
RMS_EPS = 1e-6
DILATED_PATTERNS = ((128, 1), (512, 4), (2048, 16))
BAND = 128
WINDOW_MAX = 2048
LANES = 128
SUBLANES = 8
VMEM_LIMIT = 56 * 1024 * 1024
ROW_TILE = 512
SSM_STEPS = 128
SSM_SUB_STEPS = 32
NEG_INF = float("-inf")


def _sigmoid(x):
    return 1.0 / (1.0 + jnp.exp(-x))


def _rms_scale(x):
    return x * lax.rsqrt(jnp.mean(x * x, axis=-1, keepdims=True) + RMS_EPS)


def _ssm_prep_kernel(are_ref, aim_ref, ldt_ref, bre_ref, bim_ref, lam_ref, bbre_ref, bbim_ref):
    a_re = are_ref[...]
    a_im = aim_ref[...]
    dt = jnp.exp(ldt_ref[...])
    mag = jnp.exp(a_re * dt)
    ang = a_im * dt
    lam_re = mag * jnp.cos(ang)
    lam_im = mag * jnp.sin(ang)
    den = a_re * a_re + a_im * a_im
    f_re = ((lam_re - 1.0) * a_re + lam_im * a_im) / den
    f_im = (lam_im * a_re - (lam_re - 1.0) * a_im) / den
    lam_ref[:, 0:D_STATE] = jnp.broadcast_to(lam_re, (SUBLANES, D_STATE))
    lam_ref[:, D_STATE:2 * D_STATE] = jnp.broadcast_to(lam_im, (SUBLANES, D_STATE))
    b_re = bre_ref[...]
    b_im = bim_ref[...]
    bbre_ref[...] = f_re * b_re - f_im * b_im
    bbim_ref[...] = f_re * b_im + f_im * b_re


def _ssm_prep(a_re, a_im, log_dt, b_re, b_im):
    flat = lambda a: a.reshape(1, D_STATE)
    ldt = jnp.repeat(log_dt, SSM_STATE).reshape(1, D_STATE)
    bt = lambda b: b.transpose(2, 0, 1).reshape(SSM_GROUP, D_STATE)
    return pl.pallas_call(
        _ssm_prep_kernel,
        out_shape=(jax.ShapeDtypeStruct((SUBLANES, 2 * D_STATE), F32),
                   jax.ShapeDtypeStruct((SSM_GROUP, D_STATE), F32),
                   jax.ShapeDtypeStruct((SSM_GROUP, D_STATE), F32)),
        name="ssm_prep",
    )(flat(a_re), flat(a_im), ldt, bt(b_re), bt(b_im))


def _block_diag_weights(bb_re, bb_im, c_re, c_im):
    eye = jnp.eye(8, dtype=F32)
    def place_b(bb):
        b4 = bb.reshape(SSM_GROUP, 4, 8, SSM_STATE).transpose(1, 2, 0, 3)
        return jnp.einsum("kgcp,gh->kgchp", b4, eye).reshape(4, 128, 512)
    def place_c(c):
        c4 = c.reshape(4, 8, SSM_GROUP, SSM_STATE)
        return jnp.einsum("kgcp,gh->khpgc", c4, eye).reshape(4, 512, 128)
    wb = jnp.concatenate([place_b(bb_re), place_b(bb_im)], axis=2).astype(BF16)
    cc = jnp.concatenate([place_c(c_re), -place_c(c_im)], axis=1).astype(BF16)
    return wb, cc


def _in_proj_kernel(x_ref, g_ref, w_ref, xz_ref, q_ref, k_ref, v_ref, za_ref, *t_refs):
    xn = (_rms_scale(x_ref[...]) * g_ref[...]).astype(BF16)
    outs = (None, None, q_ref, k_ref, v_ref, za_ref)
    for j in range(D_IN_PROJ // 512):
        r = jnp.dot(xn, w_ref[:, j * 512:(j + 1) * 512], preferred_element_type=F32)
        if j in (1, 5):
            r = r * _sigmoid(r)
        if j < 2:
            for s in range(4):
                xz_ref[j * 4 + s] = r[:, s * LANES:(s + 1) * LANES]
        else:
            outs[j][...] = r
            if t_refs and j in (3, 4):
                t_refs[j - 3][...] = r.T


def _in_proj(x, norm_g, w_in_bf, *, seq_len=None):
    rows = x.shape[0]
    row_spec = lambda width: pl.BlockSpec((ROW_TILE, width), lambda i: (i, 0))
    out_specs = [pl.BlockSpec((8, ROW_TILE, LANES), lambda i: (0, i, 0))] + [row_spec(512)] * 4
    out_shape = [jax.ShapeDtypeStruct((8, rows, LANES), F32)] + [jax.ShapeDtypeStruct((rows, 512), F32)] * 4
    if seq_len is not None:
        per_seq = seq_len // ROW_TILE
        out_specs += [pl.BlockSpec((None, 512, ROW_TILE), lambda i: (i // per_seq, 0, i % per_seq))] * 2
        out_shape += [jax.ShapeDtypeStruct((rows // seq_len, 512, seq_len), F32)] * 2
    return pl.pallas_call(
        _in_proj_kernel,
        grid=(rows // ROW_TILE,),
        in_specs=[row_spec(D_MODEL),
                  pl.BlockSpec((1, D_MODEL), lambda i: (0, 0)),
                  pl.BlockSpec((D_MODEL, D_IN_PROJ), lambda i: (0, 0))],
        out_specs=out_specs,
        out_shape=out_shape,
        compiler_params=pltpu.CompilerParams(dimension_semantics=("parallel",), vmem_limit_bytes=VMEM_LIMIT),
        name="in_proj",
    )(x, norm_g.reshape(1, D_MODEL), w_in_bf)


def _ssm_kernel(xz_ref, h0_ref, lam_ref, wb_ref, cc_ref, d_ref, wglu_ref, bglu_ref, g_ref,
                ys_ref, st_ref, u_scr, bu_scr, ys_scr, *, nb, tt):
    rows = nb * tt
    prompt = xz_ref.ndim == 4

    @pl.when(pl.program_id(0) == 0)
    def _():
        st_ref[...] = h0_ref[...]

    for s in range(8):
        if prompt:
            for b in range(nb):
                u_scr[s, pl.ds(b, tt, stride=nb), :] = xz_ref[s, b]
        else:
            for t in range(tt):
                u_scr[s, t * nb:(t + 1) * nb, :] = xz_ref[s, pl.ds(t, nb, stride=tt), :]

    sub = min(tt, SSM_SUB_STEPS)
    sub_rows = [slice(j * sub * nb, (j + 1) * sub * nb) for j in range(tt // sub)]

    for rs in sub_rows:
        for kt in range(4):
            r = jnp.dot(u_scr[kt, rs, :].astype(BF16), wb_ref[kt], preferred_element_type=F32)
            bu_scr[rs, kt * 512:(kt + 1) * 512] = r[:, 0:512]
            bu_scr[rs, D_STATE + kt * 512:D_STATE + (kt + 1) * 512] = r[:, 512:1024]

    width = (SUBLANES * LANES * 8) // nb if nb <= 64 else LANES
    chunks = [(slice(c * width, (c + 1) * width), slice(D_STATE + c * width, D_STATE + (c + 1) * width))
              for c in range(D_STATE // width)]

    for j, rs in enumerate(sub_rows):
        for re_l, im_l in chunks:
            if nb == SUBLANES:
                lam_re, lam_im = lam_ref[:, re_l], lam_ref[:, im_l]
            else:
                lam_re = jnp.broadcast_to(lam_ref[0:1, re_l], (nb, width))
                lam_im = jnp.broadcast_to(lam_ref[0:1, im_l], (nb, width))
            s_re, s_im = st_ref[:, re_l], st_ref[:, im_l]
            for t in range(j * sub, (j + 1) * sub):
                tr = slice(t * nb, (t + 1) * nb)
                s_re, s_im = (lam_re * s_re - lam_im * s_im + bu_scr[tr, re_l],
                              lam_re * s_im + lam_im * s_re + bu_scr[tr, im_l])
                bu_scr[tr, re_l] = s_re
                bu_scr[tr, im_l] = s_im
            st_ref[:, re_l] = s_re
            st_ref[:, im_l] = s_im

        ys = []
        for kt in range(4):
            xc = jnp.concatenate([bu_scr[rs, kt * 512:(kt + 1) * 512],
                                  bu_scr[rs, D_STATE + kt * 512:D_STATE + (kt + 1) * 512]], axis=1).astype(BF16)
            y = jnp.dot(xc, cc_ref[kt], preferred_element_type=F32)
            ys.append(y + d_ref[:, kt * LANES:(kt + 1) * LANES] * u_scr[kt, rs, :])
        y = jnp.concatenate(ys, axis=1)
        g = jax.nn.gelu(y)
        gl = jnp.dot(g.astype(BF16), wglu_ref[...], preferred_element_type=F32) + bglu_ref[...]
        gate = jnp.concatenate([u_scr[4 + s, rs, :] for s in range(4)], axis=1)
        y = gl[:, 0:D_SSM] * _sigmoid(gl[:, D_SSM:2 * D_SSM]) * gate
        y = _rms_scale(y) * g_ref[...]
        for s in range(4):
            ys_scr[s, rs, :] = y[:, s * LANES:(s + 1) * LANES]

    for s in range(4):
        if prompt:
            for b in range(nb):
                ys_ref[s, b] = ys_scr[s, pl.ds(b, tt, stride=nb), :].astype(ys_ref.dtype)
        else:
            for t in range(tt):
                ys_ref[s, pl.ds(t, nb, stride=tt), :] = ys_scr[s, t * nb:(t + 1) * nb, :]


def _ssm(xz, h0, lam, wb, cc, d_vec, w_glu_bf, b_glu, norm_g, *, nb, t_total, tt):
    rows = nb * tt
    const = lambda shape: pl.BlockSpec(shape, lambda i: (0,) * len(shape))
    if t_total > tt:
        xz_in = xz.reshape(8, nb, t_total, LANES)
        xz_spec = pl.BlockSpec((8, nb, tt, LANES), lambda i: (0, 0, i, 0))
        ys_spec = pl.BlockSpec((4, nb, tt, LANES), lambda i: (0, 0, i, 0))
        ys_shape = jax.ShapeDtypeStruct((4, nb, t_total, LANES), BF16)
    else:
        xz_in = xz
        xz_spec = const((8, rows, LANES))
        ys_spec = const((4, rows, LANES))
        ys_shape = jax.ShapeDtypeStruct((4, rows, LANES), F32)
    ys, st = pl.pallas_call(
        functools.partial(_ssm_kernel, nb=nb, tt=tt),
        grid=(t_total // tt,),
        in_specs=[xz_spec, const((nb, 2 * D_STATE)), const((SUBLANES, 2 * D_STATE)),
                  const((4, 128, 1024)), const((4, 1024, 128)), const((1, D_SSM)),
                  const((D_SSM, 2 * D_SSM)), const((1, 2 * D_SSM)), const((1, D_SSM))],
        out_specs=[ys_spec, const((nb, 2 * D_STATE))],
        out_shape=[ys_shape, jax.ShapeDtypeStruct((nb, 2 * D_STATE), F32)],
        scratch_shapes=[pltpu.VMEM((8, rows, LANES), F32),
                        pltpu.VMEM((rows, 2 * D_STATE), F32),
                        pltpu.VMEM((4, rows, LANES), F32)],
        compiler_params=pltpu.CompilerParams(dimension_semantics=("arbitrary",), vmem_limit_bytes=VMEM_LIMIT),
        name="ssm_prompt" if t_total > tt else "ssm_sample",
    )(xz_in, h0, lam, wb, cc, d_vec.reshape(1, D_SSM), w_glu_bf, b_glu.reshape(1, 2 * D_SSM),
      norm_g.reshape(1, D_SSM))
    return ys.reshape(4, nb * t_total, LANES), st


PART0_SIXTEENTHS = 11
GROUP_BLOCKS = 3


def _band_scores(qs, ks, biases):
    low = lax.broadcasted_iota(jnp.int32, (BAND, LANES), 1) < HEAD_DIM
    nt = (((1,), (1,)), ((), ()))
    stacked = []
    for q in qs:
        q = q * (HEAD_DIM ** -0.5)
        stacked.append(jnp.concatenate([jnp.where(low, q, 0.0), jnp.where(low, 0.0, q)], axis=0).astype(BF16))
    return [lax.dot_general(q, k.astype(BF16), nt, preferred_element_type=F32) + bias
            for q, k, bias in zip(stacked, ks, biases)]


def _band_softmax_pv(ss, vs):
    low = lax.broadcasted_iota(jnp.int32, (BAND, LANES), 1) < HEAD_DIM
    ms = [jnp.max(s, axis=-1, keepdims=True) for s in ss]
    ps = [jnp.exp(s - m).astype(BF16) for s, m in zip(ss, ms)]
    outs = [jnp.dot(p, jnp.concatenate([v.astype(BF16), jnp.ones(v.shape, BF16)], axis=1),
                    preferred_element_type=F32) for p, v in zip(ps, vs)]
    wide = lambda a: jnp.broadcast_to(a, (BAND, a.shape[1] if a.shape[1] > 1 else LANES))
    pick = lambda a: jnp.where(low, wide(a[0:BAND]), wide(a[BAND:2 * BAND]))
    return [(pick(o[:, 0:LANES]), pick(m), pick(o[:, LANES:2 * LANES])) for o, m in zip(outs, ms)]


def _prompt_attn_part(q_ref, k_ref, v_ref, gate_ref, o_ref, o_scr, m_scr, l_scr, band_scr, causal_scr, part):
    t_total = q_ref.shape[0]
    if part == 0:
        for scr, kwin in ((band_scr, 2 * BAND), (causal_scr, BAND)):
            qq = lax.broadcasted_iota(jnp.int32, (2 * BAND, kwin), 0) & (BAND - 1)
            kk = lax.broadcasted_iota(jnp.int32, (2 * BAND, kwin), 1)
            valid = ((kk >= qq) & (kk <= qq + BAND)) if kwin > BAND else (kk <= qq)
            scr[...] = jnp.where(valid, 0.0, NEG_INF)

    blocks = []
    for pidx, (window, dil) in enumerate(DILATED_PATTERNS):
        nblk = t_total // dil // BAND
        rows = (lambda r0, n, dil=dil: pl.ds(r0, n, stride=dil) if dil > 1 else pl.ds(r0, n))
        for rho in range(dil):
            blocks.append((pidx, rows(rho, BAND), rows(rho, BAND), causal_scr))
            blocks += [(pidx, rows(rho + dil * BAND * i, BAND), rows(rho + dil * BAND * (i - 1), 2 * BAND), band_scr)
                       for i in range(1, nblk)]

    def scores(group):
        return _band_scores([q_ref[qr, :] for _, qr, _, _ in group], [k_ref[kr, :] for _, _, kr, _ in group],
                            [bias[...] for _, _, _, bias in group])

    def finish(group, ss):
        res = _band_softmax_pv(ss, [v_ref[kr, :] for _, _, kr, _ in group])
        for (pidx, qr, _, _), (o, m, l) in zip(group, res):
            o_scr[pidx, qr, :] = o
            m_scr[pidx, qr, :] = m
            l_scr[pidx, qr, :] = l

    groups = [blocks[g:g + GROUP_BLOCKS] for g in range(0, len(blocks), GROUP_BLOCKS)]
    half = (len(groups) * PART0_SIXTEENTHS) // 16
    pending = None
    for group in (groups[:half] if part == 0 else groups[half:]):
        ss = scores(group)
        if pending is not None:
            finish(*pending)
        pending = (group, ss)
    finish(*pending)

    if part == 1:
        chunk = 256
        def combine(c, _):
            rows = pl.ds(pl.multiple_of(c * chunk, chunk), chunk)
            ms = [m_scr[p, rows, :] for p in range(3)]
            big = jnp.maximum(jnp.maximum(ms[0], ms[1]), ms[2])
            ws = [jnp.exp(m - big) for m in ms]
            num = ws[0] * o_scr[0, rows, :] + ws[1] * o_scr[1, rows, :] + ws[2] * o_scr[2, rows, :]
            den = ws[0] * l_scr[0, rows, :] + ws[1] * l_scr[1, rows, :] + ws[2] * l_scr[2, rows, :]
            o_ref[rows, :] = (num / den) * gate_ref[rows, :]
            return 0
        lax.fori_loop(0, t_total // chunk, combine, 0)


NEAR = 512
SEQS_PER_STEP = 2


def _sample_attn_step(q_ref, kn_ref, vn_ref, gate_ref, kc_ref, vc_ref, o_ref, tt):
    assert 2 * tt == SUBLANES
    scale = HEAD_DIM ** -0.5
    nt = (((1,), (1,)), ((), ()))
    row8 = lax.broadcasted_iota(jnp.int32, (SUBLANES, LANES), 0)
    lane8 = lax.broadcasted_iota(jnp.int32, (SUBLANES, LANES), 1)
    top = row8 < tt
    own_head = top == (lane8 < HEAD_DIM)

    def dup(x, seq):
        rolled = pltpu.roll(x, tt, axis=0)
        return jnp.where(top, x, rolled) if seq == 0 else jnp.where(top, rolled, x)

    jq = lambda shape: lax.broadcasted_iota(jnp.int32, shape, 0) & (tt - 1)
    d_full = WINDOW_MAX + jq((SUBLANES, WINDOW_MAX)) - lax.broadcasted_iota(jnp.int32, (SUBLANES, WINDOW_MAX), 1)
    d_near = NEAR + jq((SUBLANES, NEAR)) - lax.broadcasted_iota(jnp.int32, (SUBLANES, NEAR), 1)
    d_new = jq((SUBLANES, SUBLANES)) - lax.broadcasted_iota(jnp.int32, (SUBLANES, SUBLANES), 1)
    new_key = lax.broadcasted_iota(jnp.int32, (SUBLANES, SUBLANES), 1) < tt
    ok_main, ok_new = [], []
    for window, dil in DILATED_PATTERNS:
        d = d_near if window <= NEAR else d_full
        ok_main.append(((d & (dil - 1)) == 0) & (d <= window))
        ok_new.append(new_key & (d_new >= 0) & ((d_new & (dil - 1)) == 0))

    insts = [(slice(pr * LANES, (pr + 1) * LANES), seq)
             for pr in range(D_ATTN // LANES) for seq in range(SEQS_PER_STEP)]

    scored = []
    for lanes, seq in insts:
        q = jnp.where(own_head, dup(q_ref[:, lanes], seq) * scale, 0.0).astype(BF16)
        kn = dup(kn_ref[:, lanes], seq).astype(BF16)
        s_main = jnp.dot(q, kc_ref[seq, lanes, :].astype(BF16), preferred_element_type=F32)
        s_new = lax.dot_general(q, kn, nt, preferred_element_type=F32)
        scored.append((s_main, s_new))

    weighted = []
    for s_main, s_new in scored:
        stats = []
        for (window, dil), okm, okn in zip(DILATED_PATTERNS, ok_main, ok_new):
            sm = jnp.where(okm, s_main[:, WINDOW_MAX - NEAR:] if window <= NEAR else s_main, NEG_INF)
            sn = jnp.where(okn, s_new, NEG_INF)
            m = jnp.maximum(jnp.max(sm, axis=-1, keepdims=True), jnp.max(sn, axis=-1, keepdims=True))
            pm = jnp.exp(sm - m)
            pn = jnp.exp(sn - m)
            l = jnp.sum(pm, axis=-1, keepdims=True) + jnp.sum(pn, axis=-1, keepdims=True)
            stats.append((m, l, pm, pn))
        big = functools.reduce(jnp.maximum, [s[0] for s in stats])
        ws = [jnp.exp(s[0] - big) for s in stats]
        den = functools.reduce(jnp.add, [w * s[1] for w, s in zip(ws, stats)])
        p_new = functools.reduce(jnp.add, [w * s[3] for w, s in zip(ws, stats)])
        p_near = functools.reduce(jnp.add, [w * s[2] for w, s in zip(ws, stats) if s[2].shape[1] == NEAR])
        p_full = functools.reduce(jnp.add, [w * s[2] for w, s in zip(ws, stats) if s[2].shape[1] != NEAR])
        p_main = jnp.concatenate([p_full[:, 0:WINDOW_MAX - NEAR], p_full[:, WINDOW_MAX - NEAR:] + p_near], axis=1)
        weighted.append((p_main.astype(BF16), p_new.astype(BF16).astype(F32), den))

    halves = []
    for (lanes, seq), (p_main, p_new, den) in zip(insts, weighted):
        acc = lax.dot_general(p_main, vc_ref[seq, lanes, :].astype(BF16), nt, preferred_element_type=F32)
        vn = dup(vn_ref[:, lanes], seq).astype(BF16).astype(F32)
        for j in range(tt):
            acc = acc + p_new[:, j:j + 1] * vn[j:j + 1, :]
        acc = acc / den
        halves.append(jnp.where(lane8 < HEAD_DIM, acc, pltpu.roll(acc, tt, axis=0)))
    for pr in range(D_ATTN // LANES):
        lanes = slice(pr * LANES, (pr + 1) * LANES)
        both = jnp.where(top, halves[SEQS_PER_STEP * pr], pltpu.roll(halves[SEQS_PER_STEP * pr + 1], tt, axis=0))
        o_ref[:, lanes] = both * gate_ref[:, lanes]


def _attention_kernel(sq_ref, skn_ref, svn_ref, sgate_ref, kc_ref, vc_ref, pq_ref, pk_ref, pv_ref, pgate_ref,
                      so_ref, po_ref, o_scr, m_scr, l_scr, band_scr, causal_scr, *, tt):
    for part in range(2):
        @pl.when(pl.program_id(0) % 2 == part)
        def _(part=part):
            _sample_attn_step(sq_ref, skn_ref, svn_ref, sgate_ref, kc_ref, vc_ref, so_ref, tt)
            _prompt_attn_part(pq_ref, pk_ref, pv_ref, pgate_ref, po_ref, o_scr, m_scr, l_scr,
                              band_scr, causal_scr, part)


def _attention(sq, sk_new, sv_new, sgate, cache_kt, cache_vt, pq, pk, pv, pgate, *, dec_batch, tt, batch, t_total):
    pairs = D_ATTN // LANES
    steps = dec_batch // SEQS_PER_STEP
    assert cache_kt.shape == (dec_batch, D_ATTN, WINDOW_MAX) and steps == 2 * batch * pairs
    tok = pl.BlockSpec((SEQS_PER_STEP * tt, D_ATTN), lambda i: (i, 0))
    cache = pl.BlockSpec((SEQS_PER_STEP, D_ATTN, WINDOW_MAX), lambda i: (i, 0, 0))
    shp = (batch, t_total, D_ATTN)
    unit = pl.BlockSpec((None, t_total, LANES), lambda i: (i // (2 * pairs), 0, (i // 2) % pairs))
    so, po = pl.pallas_call(
        functools.partial(_attention_kernel, tt=tt),
        grid=(steps,),
        in_specs=[tok, tok, tok, tok, cache, cache, unit, unit, unit, unit],
        out_specs=[tok, unit],
        out_shape=[jax.ShapeDtypeStruct((dec_batch * tt, D_ATTN), F32), jax.ShapeDtypeStruct(shp, F32)],
        scratch_shapes=[pltpu.VMEM((3, t_total, LANES), F32)] * 3
                       + [pltpu.VMEM((2 * BAND, 2 * BAND), F32), pltpu.VMEM((2 * BAND, BAND), F32)],
        compiler_params=pltpu.CompilerParams(dimension_semantics=("arbitrary",), vmem_limit_bytes=VMEM_LIMIT),
        name="attention",
    )(sq, sk_new, sv_new, sgate, cache_kt, cache_vt,
      pq.reshape(shp), pk.reshape(shp), pv.reshape(shp), pgate.reshape(shp))
    return so, po.reshape(batch * t_total, D_ATTN)


def _out_proj_kernel(x_ref, ys_ref, ya_ref, ga_ref, w_ref, gf_ref, y_ref):
    ys = jnp.concatenate([ys_ref[s] for s in range(4)], axis=1).astype(BF16)
    ya = (_rms_scale(ya_ref[...]) * ga_ref[...]).astype(BF16)
    mix = jnp.concatenate([ys, ya], axis=1)
    h = x_ref[...] + jnp.dot(mix, w_ref[...], preferred_element_type=F32)
    y_ref[...] = _rms_scale(h) * gf_ref[...]


def _out_proj(x, ys, ya, norm_attn_g, w_out_bf, final_g):
    rows = x.shape[0]
    row_spec = lambda width: pl.BlockSpec((ROW_TILE, width), lambda i: (i, 0))
    const = lambda shape: pl.BlockSpec(shape, lambda i: (0,) * len(shape))
    return pl.pallas_call(
        _out_proj_kernel,
        grid=(rows // ROW_TILE,),
        in_specs=[row_spec(D_MODEL), pl.BlockSpec((4, ROW_TILE, LANES), lambda i: (0, i, 0)), row_spec(D_ATTN),
                  const((1, D_ATTN)), const((D_MODEL, D_MODEL)), const((1, D_MODEL))],
        out_specs=row_spec(D_MODEL),
        out_shape=jax.ShapeDtypeStruct((rows, D_MODEL), F32),
        compiler_params=pltpu.CompilerParams(dimension_semantics=("parallel",), vmem_limit_bytes=VMEM_LIMIT),
        name="out_proj",
    )(x, ys, ya, norm_attn_g.reshape(1, D_ATTN), w_out_bf, final_g.reshape(1, D_MODEL))


def kernel(x_prompt, x_sample, cache_k, cache_v, state_ssm_re, state_ssm_im, norm_in_g, w_in,
           ssm_A_re, ssm_A_im, ssm_log_dt, ssm_B_re, ssm_B_im, ssm_C_re, ssm_C_im, ssm_D,
           w_glu, b_glu, norm_ssm_g, norm_attn_g, w_out, final_norm_g):
    depth = w_in.shape[0]
    assert depth == 1
    batch, seq, _ = x_prompt.shape
    dec_batch, dec_seq, _ = x_sample.shape
    assert batch == SUBLANES and seq == WINDOW_MAX and seq % SSM_STEPS == 0

    lam, bb_re, bb_im = _ssm_prep(ssm_A_re[0], ssm_A_im[0], ssm_log_dt[0], ssm_B_re[0], ssm_B_im[0])
    wb, cc = _block_diag_weights(bb_re, bb_im, ssm_C_re[0], ssm_C_im[0])
    w_in_bf = w_in[0].astype(BF16)
    w_glu_bf = w_glu[0].astype(BF16)
    w_out_bf = w_out[0].astype(BF16)

    ssm = functools.partial(_ssm, lam=lam, wb=wb, cc=cc, d_vec=ssm_D[0], w_glu_bf=w_glu_bf, b_glu=b_glu[0],
                            norm_g=norm_ssm_g[0])
    out_proj = functools.partial(_out_proj, norm_attn_g=norm_attn_g[0], w_out_bf=w_out_bf, final_g=final_norm_g)

    xp = x_prompt.reshape(batch * seq, D_MODEL)
    xzp, qp, kp, vp, gp, kpt, vpt = _in_proj(xp, norm_in_g[0], w_in_bf, seq_len=seq)
    ysp, stp = ssm(xzp, jnp.zeros((batch, 2 * D_STATE), F32), nb=batch, t_total=seq, tt=SSM_STEPS)

    xs = x_sample.reshape(dec_batch * dec_seq, D_MODEL)
    xzs, qs, ksm, vsm, gs = _in_proj(xs, norm_in_g[0], w_in_bf)
    h0 = jnp.concatenate([state_ssm_re[0].reshape(dec_batch, D_STATE),
                          state_ssm_im[0].reshape(dec_batch, D_STATE)], axis=1)
    yss, sts = ssm(xzs, h0, nb=dec_batch, t_total=dec_seq, tt=dec_seq)

    chan_major = lambda c: c[0].transpose(0, 2, 3, 1).reshape(dec_batch, D_ATTN, WINDOW_MAX)
    yas, yap = _attention(qs, ksm, vsm, gs, chan_major(cache_k), chan_major(cache_v), qp, kp, vp, gp,
                          dec_batch=dec_batch, tt=dec_seq, batch=batch, t_total=seq)
    yp = out_proj(xp, ysp, yap)
    ysm = out_proj(xs, yss, yas)

    heads = lambda a, n, t: a.reshape(1, n, t, N_HEADS, HEAD_DIM)
    heads_t = lambda a, n, t: a.reshape(n, N_HEADS, HEAD_DIM, t).transpose(0, 3, 1, 2)[None]
    state = lambda s, n, part: s[:, part * D_STATE:(part + 1) * D_STATE].reshape(1, n, N_SSM_GROUPS, SSM_STATE)
    return (yp.reshape(batch, seq, D_MODEL), ysm.reshape(dec_batch, dec_seq, D_MODEL),
            heads_t(kpt, batch, seq), heads_t(vpt, batch, seq), state(stp, batch, 0), state(stp, batch, 1),
            heads(ksm, dec_batch, dec_seq), heads(vsm, dec_batch, dec_seq),
            state(sts, dec_batch, 0), state(sts, dec_batch, 1))
```

```python
import functools

import jax
import jax.numpy as jnp
from jax import lax
from jax.experimental import pallas as pl
from jax.experimental.pallas import tpu as pltpu

F32 = jnp.float32
BF16 = jnp.bfloat16

D_MODEL = 1024
D_SSM = 512
D_ATTN = 512
SSM_GROUP = 16
N_SSM_GROUPS = D_SSM // SSM_GROUP
SSM_STATE = 64
D_STATE = N_SSM_GROUPS * SSM_STATE
HEAD_DIM = 64
N_HEADS = D_ATTN // HEAD_DIM
D_IN_PROJ = 2 * D_SSM + 4 * D_ATTN
RMS_EPS = 1e-6
DILATED_PATTERNS = ((128, 1), (512, 4), (2048, 16))
BAND = 128
WINDOW_MAX = 2048
LANES = 128
SUBLANES = 8
VMEM_LIMIT = 56 * 1024 * 1024
ROW_TILE = 512
SSM_STEPS = 64
SSM_SUB_STEPS = 32
NEG_INF = float("-inf")


def _sigmoid(x):
    return 1.0 / (1.0 + jnp.exp(-x))


def _rms_scale(x):
    return x * lax.rsqrt(jnp.mean(x * x, axis=-1, keepdims=True) + RMS_EPS)


def _ssm_prep_kernel(are_ref, aim_ref, ldt_ref, bre_ref, bim_ref, lam_ref, bbre_ref, bbim_ref):
    a_re = are_ref[...]
    a_im = aim_ref[...]
    dt = jnp.exp(ldt_ref[...])
    mag = jnp.exp(a_re * dt)
    ang = a_im * dt
    lam_re = mag * jnp.cos(ang)
    lam_im = mag * jnp.sin(ang)
    den = a_re * a_re + a_im * a_im
    f_re = ((lam_re - 1.0) * a_re + lam_im * a_im) / den
    f_im = (lam_im * a_re - (lam_re - 1.0) * a_im) / den
    lam_ref[:, 0:D_STATE] = jnp.broadcast_to(lam_re, (SUBLANES, D_STATE))
    lam_ref[:, D_STATE:2 * D_STATE] = jnp.broadcast_to(lam_im, (SUBLANES, D_STATE))
    b_re = bre_ref[...]
    b_im = bim_ref[...]
    bbre_ref[...] = f_re * b_re - f_im * b_im
    bbim_ref[...] = f_re * b_im + f_im * b_re


def _ssm_prep(a_re, a_im, log_dt, b_re, b_im):
    flat = lambda a: a.reshape(1, D_STATE)
    ldt = jnp.repeat(log_dt, SSM_STATE).reshape(1, D_STATE)
    bt = lambda b: b.transpose(2, 0, 1).reshape(SSM_GROUP, D_STATE)
    return pl.pallas_call(
        _ssm_prep_kernel,
        out_shape=(jax.ShapeDtypeStruct((SUBLANES, 2 * D_STATE), F32),
                   jax.ShapeDtypeStruct((SSM_GROUP, D_STATE), F32),
                   jax.ShapeDtypeStruct((SSM_GROUP, D_STATE), F32)),
        name="ssm_prep",
    )(flat(a_re), flat(a_im), ldt, bt(b_re), bt(b_im))


def _block_diag_weights(bb_re, bb_im, c_re, c_im):
    eye = jnp.eye(8, dtype=F32)
    def place_b(bb):
        b4 = bb.reshape(SSM_GROUP, 4, 8, SSM_STATE).transpose(1, 2, 0, 3)
        return jnp.einsum("kgcp,gh->kgchp", b4, eye).reshape(4, 128, 512)
    def place_c(c):
        c4 = c.reshape(4, 8, SSM_GROUP, SSM_STATE)
        return jnp.einsum("kgcp,gh->khpgc", c4, eye).reshape(4, 512, 128)
    wb = jnp.concatenate([place_b(bb_re), place_b(bb_im)], axis=2).astype(BF16)
    cc = jnp.concatenate([place_c(c_re), -place_c(c_im)], axis=1).astype(BF16)
    return wb, cc


def _in_proj_kernel(x_ref, g_ref, w_ref, xz_ref, q_ref, k_ref, v_ref, za_ref, *t_refs):
    xn = (_rms_scale(x_ref[...]) * g_ref[...]).astype(BF16)
    outs = (None, None, q_ref, k_ref, v_ref, za_ref)
    for j in range(D_IN_PROJ // 512):
        r = jnp.dot(xn, w_ref[:, j * 512:(j + 1) * 512], preferred_element_type=F32)
        if j in (1, 5):
            r = r * _sigmoid(r)
        if j < 2:
            for s in range(4):
                xz_ref[j * 4 + s] = r[:, s * LANES:(s + 1) * LANES]
        else:
            outs[j][...] = r
            if t_refs and j in (3, 4):
                t_refs[j - 3][...] = r.T


def _in_proj(x, norm_g, w_in_bf, *, seq_len=None):
    rows = x.shape[0]
    row_spec = lambda width: pl.BlockSpec((ROW_TILE, width), lambda i: (i, 0))
    out_specs = [pl.BlockSpec((8, ROW_TILE, LANES), lambda i: (0, i, 0))] + [row_spec(512)] * 4
    out_shape = [jax.ShapeDtypeStruct((8, rows, LANES), F32)] + [jax.ShapeDtypeStruct((rows, 512), F32)] * 4
    if seq_len is not None:
        per_seq = seq_len // ROW_TILE
        out_specs += [pl.BlockSpec((None, 512, ROW_TILE), lambda i: (i // per_seq, 0, i % per_seq))] * 2
        out_shape += [jax.ShapeDtypeStruct((rows // seq_len, 512, seq_len), F32)] * 2
    return pl.pallas_call(
        _in_proj_kernel,
        grid=(rows // ROW_TILE,),
        in_specs=[row_spec(D_MODEL),
                  pl.BlockSpec((1, D_MODEL), lambda i: (0, 0)),
                  pl.BlockSpec((D_MODEL, D_IN_PROJ), lambda i: (0, 0))],
        out_specs=out_specs,
        out_shape=out_shape,
        compiler_params=pltpu.CompilerParams(dimension_semantics=("parallel",), vmem_limit_bytes=VMEM_LIMIT),
        name="in_proj",
    )(x, norm_g.reshape(1, D_MODEL), w_in_bf)


def _ssm_kernel(xz_ref, h0_ref, lam_ref, wb_ref, cc_ref, d_ref, wglu_ref, bglu_ref, g_ref, *rest, nb, tt):
    rows = nb * tt
    prompt = xz_ref.ndim == 4
    with_out_proj = len(rest) > 5
    if with_out_proj:
        x_ref, ya_ref, ga_ref, wout_ref, gf_ref, y_ref, st_ref, u_scr, bu_scr, ys_scr, mix_scr = rest
    else:
        ys_ref, st_ref, u_scr, bu_scr, ys_scr = rest

    @pl.when(pl.program_id(0) == 0)
    def _():
        st_ref[...] = h0_ref[...]

    for s in range(8):
        if prompt:
            for b in range(nb):
                u_scr[s, pl.ds(b, tt, stride=nb), :] = xz_ref[s, b]
        else:
            for t in range(tt):
                u_scr[s, t * nb:(t + 1) * nb, :] = xz_ref[s, pl.ds(t, nb, stride=tt), :]

    sub = min(tt, SSM_SUB_STEPS)
    sub_rows = [slice(j * sub * nb, (j + 1) * sub * nb) for j in range(tt // sub)]

    def expand(rs):
        for kt in range(4):
            r = jnp.dot(u_scr[kt, rs, :].astype(BF16), wb_ref[kt], preferred_element_type=F32)
            bu_scr[rs, kt * 512:(kt + 1) * 512] = r[:, 0:512]
            bu_scr[rs, D_STATE + kt * 512:D_STATE + (kt + 1) * 512] = r[:, 512:1024]

    width = (SUBLANES * LANES * 8) // nb if nb <= 64 else LANES
    chunks = [(slice(c * width, (c + 1) * width), slice(D_STATE + c * width, D_STATE + (c + 1) * width))
              for c in range(D_STATE // width)]

    def recur(j):
        for re_l, im_l in chunks:
            if nb == SUBLANES:
                lam_re, lam_im = lam_ref[:, re_l], lam_ref[:, im_l]
            else:
                lam_re = jnp.broadcast_to(lam_ref[0:1, re_l], (nb, width))
                lam_im = jnp.broadcast_to(lam_ref[0:1, im_l], (nb, width))
            s_re, s_im = st_ref[:, re_l], st_ref[:, im_l]
            for t in range(j * sub, (j + 1) * sub):
                tr = slice(t * nb, (t + 1) * nb)
                s_re, s_im = (lam_re * s_re - lam_im * s_im + bu_scr[tr, re_l],
                              lam_re * s_im + lam_im * s_re + bu_scr[tr, im_l])
                bu_scr[tr, re_l] = s_re
                bu_scr[tr, im_l] = s_im
            st_ref[:, re_l] = s_re
            st_ref[:, im_l] = s_im

    def project(rs):
        ys = []
        for kt in range(4):
            xc = jnp.concatenate([bu_scr[rs, kt * 512:(kt + 1) * 512],
                                  bu_scr[rs, D_STATE + kt * 512:D_STATE + (kt + 1) * 512]], axis=1).astype(BF16)
            y = jnp.dot(xc, cc_ref[kt], preferred_element_type=F32)
            ys.append(y + d_ref[:, kt * LANES:(kt + 1) * LANES] * u_scr[kt, rs, :])
        y = jnp.concatenate(ys, axis=1)
        g = jax.nn.gelu(y)
        gl = jnp.dot(g.astype(BF16), wglu_ref[...], preferred_element_type=F32) + bglu_ref[...]
        gate = jnp.concatenate([u_scr[4 + s, rs, :] for s in range(4)], axis=1)
        y = gl[:, 0:D_SSM] * _sigmoid(gl[:, D_SSM:2 * D_SSM]) * gate
        y = _rms_scale(y) * g_ref[...]
        for s in range(4):
            ys_scr[s, rs, :] = y[:, s * LANES:(s + 1) * LANES]

    n_sub = len(sub_rows)
    for j in range(n_sub + 2):
        if j < n_sub:
            expand(sub_rows[j])
        if 1 <= j <= n_sub:
            recur(j - 1)
        if j >= 2:
            project(sub_rows[j - 2])

    if with_out_proj:
        for b in range(nb):
            rb = slice(b * tt, (b + 1) * tt)
            for s in range(4):
                mix_scr[rb, s * LANES:(s + 1) * LANES] = ys_scr[s, pl.ds(b, tt, stride=nb), :].astype(BF16)
            mix_scr[rb, D_SSM:D_SSM + D_ATTN] = (_rms_scale(ya_ref[b]) * ga_ref[...]).astype(BF16)
        h = x_ref[...].reshape(rows, D_MODEL) + jnp.dot(mix_scr[...], wout_ref[...], preferred_element_type=F32)
        y_ref[...] = (_rms_scale(h) * gf_ref[...]).reshape(nb, tt, D_MODEL)
        return
    for s in range(4):
        if prompt:
            for b in range(nb):
                ys_ref[s, b] = ys_scr[s, pl.ds(b, tt, stride=nb), :].astype(ys_ref.dtype)
        else:
            for t in range(tt):
                ys_ref[s, pl.ds(t, nb, stride=tt), :] = ys_scr[s, t * nb:(t + 1) * nb, :]


def _ssm(xz, h0, lam, wb, cc, d_vec, w_glu_bf, b_glu, norm_g, *, nb, t_total, tt, out_proj=None):
    rows = nb * tt
    const = lambda shape: pl.BlockSpec(shape, lambda i: (0,) * len(shape))
    seq_block = lambda width: pl.BlockSpec((nb, tt, width), lambda i: (0, i, 0))
    args = [xz, h0, lam, wb, cc, d_vec.reshape(1, D_SSM), w_glu_bf, b_glu.reshape(1, 2 * D_SSM),
            norm_g.reshape(1, D_SSM)]
    in_specs = [None, const((nb, 2 * D_STATE)), const((SUBLANES, 2 * D_STATE)),
                const((4, 128, 1024)), const((4, 1024, 128)), const((1, D_SSM)),
                const((D_SSM, 2 * D_SSM)), const((1, 2 * D_SSM)), const((1, D_SSM))]
    scratch = [pltpu.VMEM((8, rows, LANES), F32), pltpu.VMEM((rows, 2 * D_STATE), F32),
               pltpu.VMEM((4, rows, LANES), F32)]
    if t_total > tt:
        args[0] = xz.reshape(8, nb, t_total, LANES)
        in_specs[0] = pl.BlockSpec((8, nb, tt, LANES), lambda i: (0, 0, i, 0))
        out_spec = pl.BlockSpec((4, nb, tt, LANES), lambda i: (0, 0, i, 0))
        out_shape = jax.ShapeDtypeStruct((4, nb, t_total, LANES), BF16)
    else:
        in_specs[0] = const((8, rows, LANES))
        out_spec = const((4, rows, LANES))
        out_shape = jax.ShapeDtypeStruct((4, rows, LANES), F32)
    if out_proj is not None:
        x, ya, norm_attn_g, w_out_bf, final_g = out_proj
        args += [x, ya, norm_attn_g.reshape(1, D_ATTN), w_out_bf, final_g.reshape(1, D_MODEL)]
        in_specs += [seq_block(D_MODEL), seq_block(D_ATTN), const((1, D_ATTN)), const((D_MODEL, D_MODEL)),
                     const((1, D_MODEL))]
        out_spec = seq_block(D_MODEL)
        out_shape = jax.ShapeDtypeStruct((nb, t_total, D_MODEL), F32)
        scratch.append(pltpu.VMEM((rows, D_MODEL), BF16))
    out, st = pl.pallas_call(
        functools.partial(_ssm_kernel, nb=nb, tt=tt),
        grid=(t_total // tt,),
        in_specs=in_specs,
        out_specs=[out_spec, const((nb, 2 * D_STATE))],
        out_shape=[out_shape, jax.ShapeDtypeStruct((nb, 2 * D_STATE), F32)],
        scratch_shapes=scratch,
        compiler_params=pltpu.CompilerParams(dimension_semantics=("arbitrary",), vmem_limit_bytes=VMEM_LIMIT),
        name="ssm_prompt" if t_total > tt else "ssm_sample",
    )(*args)
    return (out if out_proj is not None else out.reshape(4, nb * t_total, LANES)), st


PART0_SIXTEENTHS = 11
GROUP_BLOCKS = 3


def _band_scores(qs, ks, biases):
    low = lax.broadcasted_iota(jnp.int32, (BAND, LANES), 1) < HEAD_DIM
    nt = (((1,), (1,)), ((), ()))
    stacked = []
    for q in qs:
        q = q * (HEAD_DIM ** -0.5)
        stacked.append(jnp.concatenate([jnp.where(low, q, 0.0), jnp.where(low, 0.0, q)], axis=0).astype(BF16))
    return [lax.dot_general(q, k.astype(BF16), nt, preferred_element_type=F32) + bias
            for q, k, bias in zip(stacked, ks, biases)]


def _band_softmax_pv(ss, vs):
    low = lax.broadcasted_iota(jnp.int32, (BAND, LANES), 1) < HEAD_DIM
    ms = [jnp.max(s, axis=-1, keepdims=True) for s in ss]
    ps = [jnp.exp(s - m).astype(BF16) for s, m in zip(ss, ms)]
    outs = [jnp.dot(p, jnp.concatenate([v.astype(BF16), jnp.ones(v.shape, BF16)], axis=1),
                    preferred_element_type=F32) for p, v in zip(ps, vs)]
    wide = lambda a: jnp.broadcast_to(a, (BAND, a.shape[1] if a.shape[1] > 1 else LANES))
    pick = lambda a: jnp.where(low, wide(a[0:BAND]), wide(a[BAND:2 * BAND]))
    return [(pick(o[:, 0:LANES]), pick(m), pick(o[:, LANES:2 * LANES])) for o, m in zip(outs, ms)]


def _prompt_attn_part(q_ref, k_ref, v_ref, gate_ref, o_ref, o_scr, m_scr, l_scr, band_scr, causal_scr, part):
    t_total = q_ref.shape[0]
    if part == 0:
        for scr, kwin in ((band_scr, 2 * BAND), (causal_scr, BAND)):
            qq = lax.broadcasted_iota(jnp.int32, (2 * BAND, kwin), 0) & (BAND - 1)
            kk = lax.broadcasted_iota(jnp.int32, (2 * BAND, kwin), 1)
            valid = ((kk >= qq) & (kk <= qq + BAND)) if kwin > BAND else (kk <= qq)
            scr[...] = jnp.where(valid, 0.0, NEG_INF)

    blocks = []
    for pidx, (window, dil) in enumerate(DILATED_PATTERNS):
        nblk = t_total // dil // BAND
        rows = (lambda r0, n, dil=dil: pl.ds(r0, n, stride=dil) if dil > 1 else pl.ds(r0, n))
        for rho in range(dil):
            blocks.append((pidx, rows(rho, BAND), rows(rho, BAND), causal_scr))
            blocks += [(pidx, rows(rho + dil * BAND * i, BAND), rows(rho + dil * BAND * (i - 1), 2 * BAND), band_scr)
                       for i in range(1, nblk)]

    def scores(group):
        return _band_scores([q_ref[qr, :] for _, qr, _, _ in group], [k_ref[kr, :] for _, _, kr, _ in group],
                            [bias[...] for _, _, _, bias in group])

    def finish(group, ss):
        res = _band_softmax_pv(ss, [v_ref[kr, :] for _, _, kr, _ in group])
        for (pidx, qr, _, _), (o, m, l) in zip(group, res):
            o_scr[pidx, qr, :] = o
            m_scr[pidx, qr, :] = m
            l_scr[pidx, qr, :] = l

    groups = [blocks[g:g + GROUP_BLOCKS] for g in range(0, len(blocks), GROUP_BLOCKS)]
    half = (len(groups) * PART0_SIXTEENTHS) // 16
    pending = None
    for group in (groups[:half] if part == 0 else groups[half:]):
        ss = scores(group)
        if pending is not None:
            finish(*pending)
        pending = (group, ss)
    finish(*pending)

    if part == 1:
        chunk = 256
        def combine(c, _):
            rows = pl.ds(pl.multiple_of(c * chunk, chunk), chunk)
            ms = [m_scr[p, rows, :] for p in range(3)]
            big = jnp.maximum(jnp.maximum(ms[0], ms[1]), ms[2])
            ws = [jnp.exp(m - big) for m in ms]
            num = ws[0] * o_scr[0, rows, :] + ws[1] * o_scr[1, rows, :] + ws[2] * o_scr[2, rows, :]
            den = ws[0] * l_scr[0, rows, :] + ws[1] * l_scr[1, rows, :] + ws[2] * l_scr[2, rows, :]
            o_ref[rows, :] = (num / den) * gate_ref[rows, :]
            return 0
        lax.fori_loop(0, t_total // chunk, combine, 0)


NEAR = 512
SEQS_PER_STEP = 2


def _sample_attn_step(q_ref, kn_ref, vn_ref, gate_ref, kc_ref, vc_ref, o_ref, tt):
    assert 2 * tt == SUBLANES
    scale = HEAD_DIM ** -0.5
    nt = (((1,), (1,)), ((), ()))
    row8 = lax.broadcasted_iota(jnp.int32, (SUBLANES, LANES), 0)
    lane8 = lax.broadcasted_iota(jnp.int32, (SUBLANES, LANES), 1)
    top = row8 < tt
    own_head = top == (lane8 < HEAD_DIM)

    def dup(x, seq):
        rolled = pltpu.roll(x, tt, axis=0)
        return jnp.where(top, x, rolled) if seq == 0 else jnp.where(top, rolled, x)

    jq = lambda shape: lax.broadcasted_iota(jnp.int32, shape, 0) & (tt - 1)
    d_full = WINDOW_MAX + jq((SUBLANES, WINDOW_MAX)) - lax.broadcasted_iota(jnp.int32, (SUBLANES, WINDOW_MAX), 1)
    d_near = NEAR + jq((SUBLANES, NEAR)) - lax.broadcasted_iota(jnp.int32, (SUBLANES, NEAR), 1)
    d_new = jq((SUBLANES, SUBLANES)) - lax.broadcasted_iota(jnp.int32, (SUBLANES, SUBLANES), 1)
    new_key = lax.broadcasted_iota(jnp.int32, (SUBLANES, SUBLANES), 1) < tt
    ok_main, ok_new = [], []
    for window, dil in DILATED_PATTERNS:
        d = d_near if window <= NEAR else d_full
        ok_main.append(((d & (dil - 1)) == 0) & (d <= window))
        ok_new.append(new_key & (d_new >= 0) & ((d_new & (dil - 1)) == 0))

    insts = [(slice(pr * LANES, (pr + 1) * LANES), seq)
             for pr in range(D_ATTN // LANES) for seq in range(SEQS_PER_STEP)]

    scored = []
    for lanes, seq in insts:
        q = jnp.where(own_head, dup(q_ref[:, lanes], seq) * scale, 0.0).astype(BF16)
        kn = dup(kn_ref[:, lanes], seq).astype(BF16)
        s_main = jnp.dot(q, kc_ref[seq, lanes, :].astype(BF16), preferred_element_type=F32)
        s_new = lax.dot_general(q, kn, nt, preferred_element_type=F32)
        scored.append((s_main, s_new))

    weighted = []
    for s_main, s_new in scored:
        stats = []
        for (window, dil), okm, okn in zip(DILATED_PATTERNS, ok_main, ok_new):
            sm = jnp.where(okm, s_main[:, WINDOW_MAX - NEAR:] if window <= NEAR else s_main, NEG_INF)
            sn = jnp.where(okn, s_new, NEG_INF)
            m = jnp.maximum(jnp.max(sm, axis=-1, keepdims=True), jnp.max(sn, axis=-1, keepdims=True))
            pm = jnp.exp(sm - m)
            pn = jnp.exp(sn - m)
            l = jnp.sum(pm, axis=-1, keepdims=True) + jnp.sum(pn, axis=-1, keepdims=True)
            stats.append((m, l, pm, pn))
        big = functools.reduce(jnp.maximum, [s[0] for s in stats])
        ws = [jnp.exp(s[0] - big) for s in stats]
        den = functools.reduce(jnp.add, [w * s[1] for w, s in zip(ws, stats)])
        p_new = functools.reduce(jnp.add, [w * s[3] for w, s in zip(ws, stats)])
        p_near = functools.reduce(jnp.add, [w * s[2] for w, s in zip(ws, stats) if s[2].shape[1] == NEAR])
        p_full = functools.reduce(jnp.add, [w * s[2] for w, s in zip(ws, stats) if s[2].shape[1] != NEAR])
        p_main = jnp.concatenate([p_full[:, 0:WINDOW_MAX - NEAR], p_full[:, WINDOW_MAX - NEAR:] + p_near], axis=1)
        weighted.append((p_main.astype(BF16), p_new.astype(BF16).astype(F32), den))

    halves = []
    for (lanes, seq), (p_main, p_new, den) in zip(insts, weighted):
        acc = lax.dot_general(p_main, vc_ref[seq, lanes, :].astype(BF16), nt, preferred_element_type=F32)
        vn = dup(vn_ref[:, lanes], seq).astype(BF16).astype(F32)
        for j in range(tt):
            acc = acc + p_new[:, j:j + 1] * vn[j:j + 1, :]
        acc = acc / den
        halves.append(jnp.where(lane8 < HEAD_DIM, acc, pltpu.roll(acc, tt, axis=0)))
    for pr in range(D_ATTN // LANES):
        lanes = slice(pr * LANES, (pr + 1) * LANES)
        both = jnp.where(top, halves[SEQS_PER_STEP * pr], pltpu.roll(halves[SEQS_PER_STEP * pr + 1], tt, axis=0))
        o_ref[:, lanes] = both * gate_ref[:, lanes]


def _attention_kernel(sq_ref, skn_ref, svn_ref, sgate_ref, kc_ref, vc_ref, pq_ref, pk_ref, pv_ref, pgate_ref,
                      so_ref, po_ref, o_scr, m_scr, l_scr, band_scr, causal_scr, *, tt):
    for part in range(2):
        @pl.when(pl.program_id(0) % 2 == part)
        def _(part=part):
            _sample_attn_step(sq_ref, skn_ref, svn_ref, sgate_ref, kc_ref, vc_ref, so_ref, tt)
            _prompt_attn_part(pq_ref, pk_ref, pv_ref, pgate_ref, po_ref, o_scr, m_scr, l_scr,
                              band_scr, causal_scr, part)


def _attention(sq, sk_new, sv_new, sgate, cache_kt, cache_vt, pq, pk, pv, pgate, *, dec_batch, tt, batch, t_total):
    pairs = D_ATTN // LANES
    steps = dec_batch // SEQS_PER_STEP
    assert cache_kt.shape == (dec_batch, D_ATTN, WINDOW_MAX) and steps == 2 * batch * pairs
    tok = pl.BlockSpec((SEQS_PER_STEP * tt, D_ATTN), lambda i: (i, 0))
    cache = pl.BlockSpec((SEQS_PER_STEP, D_ATTN, WINDOW_MAX), lambda i: (i, 0, 0))
    shp = (batch, t_total, D_ATTN)
    unit = pl.BlockSpec((None, t_total, LANES), lambda i: (i // (2 * pairs), 0, (i // 2) % pairs))
    so, po = pl.pallas_call(
        functools.partial(_attention_kernel, tt=tt),
        grid=(steps,),
        in_specs=[tok, tok, tok, tok, cache, cache, unit, unit, unit, unit],
        out_specs=[tok, unit],
        out_shape=[jax.ShapeDtypeStruct((dec_batch * tt, D_ATTN), F32), jax.ShapeDtypeStruct(shp, F32)],
        scratch_shapes=[pltpu.VMEM((3, t_total, LANES), F32)] * 3
                       + [pltpu.VMEM((2 * BAND, 2 * BAND), F32), pltpu.VMEM((2 * BAND, BAND), F32)],
        compiler_params=pltpu.CompilerParams(dimension_semantics=("arbitrary",), vmem_limit_bytes=VMEM_LIMIT),
        name="attention",
    )(sq, sk_new, sv_new, sgate, cache_kt, cache_vt,
      pq.reshape(shp), pk.reshape(shp), pv.reshape(shp), pgate.reshape(shp))
    return so, po.reshape(batch * t_total, D_ATTN)


def _out_proj_kernel(x_ref, ys_ref, ya_ref, ga_ref, w_ref, gf_ref, y_ref):
    ys = jnp.concatenate([ys_ref[s] for s in range(4)], axis=1).astype(BF16)
    ya = (_rms_scale(ya_ref[...]) * ga_ref[...]).astype(BF16)
    mix = jnp.concatenate([ys, ya], axis=1)
    h = x_ref[...] + jnp.dot(mix, w_ref[...], preferred_element_type=F32)
    y_ref[...] = _rms_scale(h) * gf_ref[...]


def _out_proj(x, ys, ya, norm_attn_g, w_out_bf, final_g):
    rows = x.shape[0]
    row_spec = lambda width: pl.BlockSpec((ROW_TILE, width), lambda i: (i, 0))
    const = lambda shape: pl.BlockSpec(shape, lambda i: (0,) * len(shape))
    return pl.pallas_call(
        _out_proj_kernel,
        grid=(rows // ROW_TILE,),
        in_specs=[row_spec(D_MODEL), pl.BlockSpec((4, ROW_TILE, LANES), lambda i: (0, i, 0)), row_spec(D_ATTN),
                  const((1, D_ATTN)), const((D_MODEL, D_MODEL)), const((1, D_MODEL))],
        out_specs=row_spec(D_MODEL),
        out_shape=jax.ShapeDtypeStruct((rows, D_MODEL), F32),
        compiler_params=pltpu.CompilerParams(dimension_semantics=("parallel",), vmem_limit_bytes=VMEM_LIMIT),
        name="out_proj",
    )(x, ys, ya, norm_attn_g.reshape(1, D_ATTN), w_out_bf, final_g.reshape(1, D_MODEL))


def kernel(x_prompt, x_sample, cache_k, cache_v, state_ssm_re, state_ssm_im, norm_in_g, w_in,
           ssm_A_re, ssm_A_im, ssm_log_dt, ssm_B_re, ssm_B_im, ssm_C_re, ssm_C_im, ssm_D,
           w_glu, b_glu, norm_ssm_g, norm_attn_g, w_out, final_norm_g):
    depth = w_in.shape[0]
    assert depth == 1
    batch, seq, _ = x_prompt.shape
    dec_batch, dec_seq, _ = x_sample.shape
    assert batch == SUBLANES and seq == WINDOW_MAX and seq % SSM_STEPS == 0

    lam, bb_re, bb_im = _ssm_prep(ssm_A_re[0], ssm_A_im[0], ssm_log_dt[0], ssm_B_re[0], ssm_B_im[0])
    wb, cc = _block_diag_weights(bb_re, bb_im, ssm_C_re[0], ssm_C_im[0])
    w_in_bf = w_in[0].astype(BF16)
    w_glu_bf = w_glu[0].astype(BF16)
    w_out_bf = w_out[0].astype(BF16)

    ssm = functools.partial(_ssm, lam=lam, wb=wb, cc=cc, d_vec=ssm_D[0], w_glu_bf=w_glu_bf, b_glu=b_glu[0],
                            norm_g=norm_ssm_g[0])

    xp = x_prompt.reshape(batch * seq, D_MODEL)
    xzp, qp, kp, vp, gp, kpt, vpt = _in_proj(xp, norm_in_g[0], w_in_bf, seq_len=seq)
    xs = x_sample.reshape(dec_batch * dec_seq, D_MODEL)
    xzs, qs, ksm, vsm, gs = _in_proj(xs, norm_in_g[0], w_in_bf)

    chan_major = lambda c: c[0].transpose(0, 2, 3, 1).reshape(dec_batch, D_ATTN, WINDOW_MAX)
    yas, yap = _attention(qs, ksm, vsm, gs, chan_major(cache_k), chan_major(cache_v), qp, kp, vp, gp,
                          dec_batch=dec_batch, tt=dec_seq, batch=batch, t_total=seq)

    yp, stp = ssm(xzp, jnp.zeros((batch, 2 * D_STATE), F32), nb=batch, t_total=seq, tt=SSM_STEPS,
                  out_proj=(x_prompt, yap.reshape(batch, seq, D_ATTN), norm_attn_g[0], w_out_bf, final_norm_g))
    h0 = jnp.concatenate([state_ssm_re[0].reshape(dec_batch, D_STATE),
                          state_ssm_im[0].reshape(dec_batch, D_STATE)], axis=1)
    yss, sts = ssm(xzs, h0, nb=dec_batch, t_total=dec_seq, tt=dec_seq)
    ysm = _out_proj(xs, yss, yas, norm_attn_g[0], w_out_bf, final_norm_g)

    heads = lambda a, n, t: a.reshape(1, n, t, N_HEADS, HEAD_DIM)
    heads_t = lambda a, n, t: a.reshape(n, N_HEADS, HEAD_DIM, t).transpose(0, 3, 1, 2)[None]
    state = lambda s, n, part: s[:, part * D_STATE:(part + 1) * D_STATE].reshape(1, n, N_SSM_GROUPS, SSM_STATE)
    return (yp.reshape(batch, seq, D_MODEL), ysm.reshape(dec_batch, dec_seq, D_MODEL),
            heads_t(kpt, batch, seq), heads_t(vpt, batch, seq), state(stp, batch, 0), state(stp, batch, 1),
            heads(ksm, dec_batch, dec_seq), heads(vsm, dec_batch, dec_seq),
            state(sts, dec_batch, 0), state(sts, dec_batch, 1))
```

```python
import functools

import jax
import jax.numpy as jnp
from jax import lax
from jax.experimental import pallas as pl
from jax.experimental.pallas import tpu as pltpu

F32 = jnp.float32
BF16 = jnp.bfloat16

D_MODEL = 1024
D_SSM = 512
D_ATTN = 512
SSM_GROUP = 16
N_SSM_GROUPS = D_SSM // SSM_GROUP
SSM_STATE = 64
D_STATE = N_SSM_GROUPS * SSM_STATE
HEAD_DIM = 64
N_HEADS = D_ATTN // HEAD_DIM
D_IN_PROJ = 2 * D_SSM + 4 * D_ATTN
RMS_EPS = 1e-6
DILATED_PATTERNS = ((128, 1), (512, 4), (2048, 16))
BAND = 128
WINDOW_MAX = 2048
LANES = 128
SUBLANES = 8
VMEM_LIMIT = 56 * 1024 * 1024
ROW_TILE = 512
SSM_STEPS = 64
SSM_SUB_STEPS = 32
NEG_INF = float("-inf")


def _sigmoid(x):
    return 1.0 / (1.0 + jnp.exp(-x))


def _rms_scale(x):
    return x * lax.rsqrt(jnp.mean(x * x, axis=-1, keepdims=True) + RMS_EPS)


def _ssm_prep_kernel(are_ref, aim_ref, ldt_ref, bre_ref, bim_ref, lam_ref, bbre_ref, bbim_ref):
    a_re = are_ref[...]
    a_im = aim_ref[...]
    dt = jnp.exp(ldt_ref[...])
    mag = jnp.exp(a_re * dt)
    ang = a_im * dt
    lam_re = mag * jnp.cos(ang)
    lam_im = mag * jnp.sin(ang)
    den = a_re * a_re + a_im * a_im
    f_re = ((lam_re - 1.0) * a_re + lam_im * a_im) / den
    f_im = (lam_im * a_re - (lam_re - 1.0) * a_im) / den
    lam_ref[:, 0:D_STATE] = jnp.broadcast_to(lam_re, (SUBLANES, D_STATE))
    lam_ref[:, D_STATE:2 * D_STATE] = jnp.broadcast_to(lam_im, (SUBLANES, D_STATE))
    b_re = bre_ref[...]
    b_im = bim_ref[...]
    bbre_ref[...] = f_re * b_re - f_im * b_im
    bbim_ref[...] = f_re * b_im + f_im * b_re


def _ssm_prep(a_re, a_im, log_dt, b_re, b_im):
    flat = lambda a: a.reshape(1, D_STATE)
    ldt = jnp.repeat(log_dt, SSM_STATE).reshape(1, D_STATE)
    bt = lambda b: b.transpose(2, 0, 1).reshape(SSM_GROUP, D_STATE)
    return pl.pallas_call(
        _ssm_prep_kernel,
        out_shape=(jax.ShapeDtypeStruct((SUBLANES, 2 * D_STATE), F32),
                   jax.ShapeDtypeStruct((SSM_GROUP, D_STATE), F32),
                   jax.ShapeDtypeStruct((SSM_GROUP, D_STATE), F32)),
        name="ssm_prep",
    )(flat(a_re), flat(a_im), ldt, bt(b_re), bt(b_im))


def _block_diag_weights(bb_re, bb_im, c_re, c_im):
    eye = jnp.eye(8, dtype=F32)
    def place_b(bb):
        b4 = bb.reshape(SSM_GROUP, 4, 8, SSM_STATE).transpose(1, 2, 0, 3)
        return jnp.einsum("kgcp,gh->kgchp", b4, eye).reshape(4, 128, 512)
    def place_c(c):
        c4 = c.reshape(4, 8, SSM_GROUP, SSM_STATE)
        return jnp.einsum("kgcp,gh->khpgc", c4, eye).reshape(4, 512, 128)
    wb = jnp.concatenate([place_b(bb_re), place_b(bb_im)], axis=2).astype(BF16)
    cc = jnp.concatenate([place_c(c_re), -place_c(c_im)], axis=1).astype(BF16)
    return wb, cc


def _in_proj_kernel(x_ref, g_ref, w_ref, xz_ref, q_ref, k_ref, v_ref, za_ref, *t_refs):
    xn = (_rms_scale(x_ref[...]) * g_ref[...]).astype(BF16)
    outs = (None, None, q_ref, k_ref, v_ref, za_ref)
    for j in range(D_IN_PROJ // 512):
        r = jnp.dot(xn, w_ref[:, j * 512:(j + 1) * 512], preferred_element_type=F32)
        if j in (1, 5):
            r = r * _sigmoid(r)
        if j < 2:
            for s in range(4):
                xz_ref[j * 4 + s] = r[:, s * LANES:(s + 1) * LANES]
        else:
            outs[j][...] = r.astype(outs[j].dtype)
            if t_refs and j in (3, 4):
                t_refs[j - 3][...] = r.T


def _in_proj(x, norm_g, w_in_bf, *, seq_len=None):
    rows = x.shape[0]
    row_spec = lambda width: pl.BlockSpec((ROW_TILE, width), lambda i: (i, 0))
    out_specs = [pl.BlockSpec((8, ROW_TILE, LANES), lambda i: (0, i, 0))] + [row_spec(512)] * 4
    out_shape = ([jax.ShapeDtypeStruct((8, rows, LANES), F32)] + [jax.ShapeDtypeStruct((rows, 512), F32)] * 3
                 + [jax.ShapeDtypeStruct((rows, 512), F32 if seq_len is None else BF16)])
    if seq_len is not None:
        per_seq = seq_len // ROW_TILE
        out_specs += [pl.BlockSpec((None, 512, ROW_TILE), lambda i: (i // per_seq, 0, i % per_seq))] * 2
        out_shape += [jax.ShapeDtypeStruct((rows // seq_len, 512, seq_len), F32)] * 2
    return pl.pallas_call(
        _in_proj_kernel,
        grid=(rows // ROW_TILE,),
        in_specs=[row_spec(D_MODEL),
                  pl.BlockSpec((1, D_MODEL), lambda i: (0, 0)),
                  pl.BlockSpec((D_MODEL, D_IN_PROJ), lambda i: (0, 0))],
        out_specs=out_specs,
        out_shape=out_shape,
        compiler_params=pltpu.CompilerParams(dimension_semantics=("parallel",), vmem_limit_bytes=VMEM_LIMIT),
        name="in_proj",
    )(x, norm_g.reshape(1, D_MODEL), w_in_bf)


def _ssm_kernel(xz_ref, h0_ref, lam_ref, wb_ref, cc_ref, d_ref, wglu_ref, bglu_ref, g_ref, *rest, nb, tt):
    rows = nb * tt
    prompt = xz_ref.ndim == 4
    with_out_proj = len(rest) > 5
    if with_out_proj:
        x_ref, ya_ref, ga_ref, wout_ref, gf_ref, y_ref, st_ref, u_scr, bu_scr, ys_scr, mix_scr = rest
    else:
        ys_ref, st_ref, u_scr, bu_scr, ys_scr = rest

    @pl.when(pl.program_id(0) == 0)
    def _():
        st_ref[...] = h0_ref[...]

    for s in range(8):
        if prompt:
            for b in range(nb):
                u_scr[s, pl.ds(b, tt, stride=nb), :] = xz_ref[s, b]
        else:
            for t in range(tt):
                u_scr[s, t * nb:(t + 1) * nb, :] = xz_ref[s, pl.ds(t, nb, stride=tt), :]

    sub = min(tt, SSM_SUB_STEPS)
    sub_rows = [slice(j * sub * nb, (j + 1) * sub * nb) for j in range(tt // sub)]

    def expand(rs):
        for kt in range(4):
            r = jnp.dot(u_scr[kt, rs, :].astype(BF16), wb_ref[kt], preferred_element_type=F32)
            bu_scr[rs, kt * 512:(kt + 1) * 512] = r[:, 0:512]
            bu_scr[rs, D_STATE + kt * 512:D_STATE + (kt + 1) * 512] = r[:, 512:1024]

    width = (SUBLANES * LANES * 8) // nb if nb <= 64 else LANES
    chunks = [(slice(c * width, (c + 1) * width), slice(D_STATE + c * width, D_STATE + (c + 1) * width))
              for c in range(D_STATE // width)]

    def recur(j):
        for re_l, im_l in chunks:
            if nb == SUBLANES:
                lam_re, lam_im = lam_ref[:, re_l], lam_ref[:, im_l]
            else:
                lam_re = jnp.broadcast_to(lam_ref[0:1, re_l], (nb, width))
                lam_im = jnp.broadcast_to(lam_ref[0:1, im_l], (nb, width))
            s_re, s_im = st_ref[:, re_l], st_ref[:, im_l]
            for t in range(j * sub, (j + 1) * sub):
                tr = slice(t * nb, (t + 1) * nb)
                s_re, s_im = (lam_re * s_re - lam_im * s_im + bu_scr[tr, re_l],
                              lam_re * s_im + lam_im * s_re + bu_scr[tr, im_l])
                bu_scr[tr, re_l] = s_re
                bu_scr[tr, im_l] = s_im
            st_ref[:, re_l] = s_re
            st_ref[:, im_l] = s_im

    def project(rs):
        ys = []
        for kt in range(4):
            xc = jnp.concatenate([bu_scr[rs, kt * 512:(kt + 1) * 512],
                                  bu_scr[rs, D_STATE + kt * 512:D_STATE + (kt + 1) * 512]], axis=1).astype(BF16)
            y = jnp.dot(xc, cc_ref[kt], preferred_element_type=F32)
            ys.append(y + d_ref[:, kt * LANES:(kt + 1) * LANES] * u_scr[kt, rs, :])
        y = jnp.concatenate(ys, axis=1)
        g = jax.nn.gelu(y)
        gl = jnp.dot(g.astype(BF16), wglu_ref[...], preferred_element_type=F32) + bglu_ref[...]
        gate = jnp.concatenate([u_scr[4 + s, rs, :] for s in range(4)], axis=1)
        y = gl[:, 0:D_SSM] * _sigmoid(gl[:, D_SSM:2 * D_SSM]) * gate
        y = _rms_scale(y) * g_ref[...]
        for s in range(4):
            ys_scr[s, rs, :] = y[:, s * LANES:(s + 1) * LANES]

    n_sub = len(sub_rows)
    for j in range(n_sub + 2):
        if j < n_sub:
            expand(sub_rows[j])
        if 1 <= j <= n_sub:
            recur(j - 1)
        if j >= 2:
            project(sub_rows[j - 2])

    if with_out_proj:
        for b in range(nb):
            rb = slice(b * tt, (b + 1) * tt)
            for s in range(4):
                mix_scr[rb, s * LANES:(s + 1) * LANES] = ys_scr[s, pl.ds(b, tt, stride=nb), :].astype(BF16)
            mix_scr[rb, D_SSM:D_SSM + D_ATTN] = (_rms_scale(ya_ref[b].astype(F32)) * ga_ref[...]).astype(BF16)
        h = x_ref[...].reshape(rows, D_MODEL) + jnp.dot(mix_scr[...], wout_ref[...], preferred_element_type=F32)
        y_ref[...] = (_rms_scale(h) * gf_ref[...]).reshape(nb, tt, D_MODEL)
        return
    for s in range(4):
        if prompt:
            for b in range(nb):
                ys_ref[s, b] = ys_scr[s, pl.ds(b, tt, stride=nb), :].astype(ys_ref.dtype)
        else:
            for t in range(tt):
                ys_ref[s, pl.ds(t, nb, stride=tt), :] = ys_scr[s, t * nb:(t + 1) * nb, :]


def _ssm(xz, h0, lam, wb, cc, d_vec, w_glu_bf, b_glu, norm_g, *, nb, t_total, tt, out_proj=None):
    rows = nb * tt
    const = lambda shape: pl.BlockSpec(shape, lambda i: (0,) * len(shape))
    seq_block = lambda width: pl.BlockSpec((nb, tt, width), lambda i: (0, i, 0))
    args = [xz, h0, lam, wb, cc, d_vec.reshape(1, D_SSM), w_glu_bf, b_glu.reshape(1, 2 * D_SSM),
            norm_g.reshape(1, D_SSM)]
    in_specs = [None, const((nb, 2 * D_STATE)), const((SUBLANES, 2 * D_STATE)),
                const((4, 128, 1024)), const((4, 1024, 128)), const((1, D_SSM)),
                const((D_SSM, 2 * D_SSM)), const((1, 2 * D_SSM)), const((1, D_SSM))]
    scratch = [pltpu.VMEM((8, rows, LANES), F32), pltpu.VMEM((rows, 2 * D_STATE), F32),
               pltpu.VMEM((4, rows, LANES), F32)]
    if t_total > tt:
        args[0] = xz.reshape(8, nb, t_total, LANES)
        in_specs[0] = pl.BlockSpec((8, nb, tt, LANES), lambda i: (0, 0, i, 0))
        out_spec = pl.BlockSpec((4, nb, tt, LANES), lambda i: (0, 0, i, 0))
        out_shape = jax.ShapeDtypeStruct((4, nb, t_total, LANES), BF16)
    else:
        in_specs[0] = const((8, rows, LANES))
        out_spec = const((4, rows, LANES))
        out_shape = jax.ShapeDtypeStruct((4, rows, LANES), F32)
    if out_proj is not None:
        x, ya, norm_attn_g, w_out_bf, final_g = out_proj
        args += [x, ya, norm_attn_g.reshape(1, D_ATTN), w_out_bf, final_g.reshape(1, D_MODEL)]
        in_specs += [seq_block(D_MODEL), seq_block(D_ATTN), const((1, D_ATTN)), const((D_MODEL, D_MODEL)),
                     const((1, D_MODEL))]
        out_spec = seq_block(D_MODEL)
        out_shape = jax.ShapeDtypeStruct((nb, t_total, D_MODEL), F32)
        scratch.append(pltpu.VMEM((rows, D_MODEL), BF16))
    out, st = pl.pallas_call(
        functools.partial(_ssm_kernel, nb=nb, tt=tt),
        grid=(t_total // tt,),
        in_specs=in_specs,
        out_specs=[out_spec, const((nb, 2 * D_STATE))],
        out_shape=[out_shape, jax.ShapeDtypeStruct((nb, 2 * D_STATE), F32)],
        scratch_shapes=scratch,
        compiler_params=pltpu.CompilerParams(dimension_semantics=("arbitrary",), vmem_limit_bytes=VMEM_LIMIT),
        name="ssm_prompt" if t_total > tt else "ssm_sample",
    )(*args)
    return (out if out_proj is not None else out.reshape(4, nb * t_total, LANES)), st


PART0_SIXTEENTHS = 11
GROUP_BLOCKS = 2


def _band_scores(qs, ks, biases):
    low = lax.broadcasted_iota(jnp.int32, (BAND, LANES), 1) < HEAD_DIM
    nt = (((1,), (1,)), ((), ()))
    stacked = []
    for q in qs:
        q = q * (HEAD_DIM ** -0.5)
        stacked.append(jnp.concatenate([jnp.where(low, q, 0.0), jnp.where(low, 0.0, q)], axis=0).astype(BF16))
    return [lax.dot_general(q, k.astype(BF16), nt, preferred_element_type=F32) + bias
            for q, k, bias in zip(stacked, ks, biases)]


def _band_softmax_pv(ss, vs):
    low = lax.broadcasted_iota(jnp.int32, (BAND, LANES), 1) < HEAD_DIM
    ms = [jnp.max(s, axis=-1, keepdims=True) for s in ss]
    ps = [jnp.exp(s - m).astype(BF16) for s, m in zip(ss, ms)]
    outs = [jnp.dot(p, jnp.concatenate([v.astype(BF16), jnp.ones(v.shape, BF16)], axis=1),
                    preferred_element_type=F32) for p, v in zip(ps, vs)]
    wide = lambda a: jnp.broadcast_to(a, (BAND, a.shape[1] if a.shape[1] > 1 else LANES))
    pick = lambda a: jnp.where(low, wide(a[0:BAND]), wide(a[BAND:2 * BAND]))
    return [(pick(o[:, 0:LANES]), pick(m), pick(o[:, LANES:2 * LANES])) for o, m in zip(outs, ms)]


def _prompt_attn_part(q_ref, k_ref, v_ref, gate_ref, o_ref, o_scr, m_scr, l_scr, band_scr, causal_scr, part):
    t_total = q_ref.shape[0]
    if part == 0:
        for scr, kwin in ((band_scr, 2 * BAND), (causal_scr, BAND)):
            qq = lax.broadcasted_iota(jnp.int32, (2 * BAND, kwin), 0) & (BAND - 1)
            kk = lax.broadcasted_iota(jnp.int32, (2 * BAND, kwin), 1)
            valid = ((kk >= qq) & (kk <= qq + BAND)) if kwin > BAND else (kk <= qq)
            scr[...] = jnp.where(valid, 0.0, NEG_INF)

    blocks = []
    for pidx, (window, dil) in enumerate(DILATED_PATTERNS):
        nblk = t_total // dil // BAND
        rows = (lambda r0, n, dil=dil: pl.ds(r0, n, stride=dil) if dil > 1 else pl.ds(r0, n))
        for rho in range(dil):
            blocks.append((pidx, rows(rho, BAND), rows(rho, BAND), causal_scr))
            blocks += [(pidx, rows(rho + dil * BAND * i, BAND), rows(rho + dil * BAND * (i - 1), 2 * BAND), band_scr)
                       for i in range(1, nblk)]

    def scores(group):
        return _band_scores([q_ref[qr, :] for _, qr, _, _ in group], [k_ref[kr, :] for _, _, kr, _ in group],
                            [bias[...] for _, _, _, bias in group])

    def finish(group, ss):
        res = _band_softmax_pv(ss, [v_ref[kr, :] for _, _, kr, _ in group])
        for (pidx, qr, _, _), (o, m, l) in zip(group, res):
            o_scr[pidx, qr, :] = o
            m_scr[pidx, qr, :] = m
            l_scr[pidx, qr, :] = l

    groups = [blocks[g:g + GROUP_BLOCKS] for g in range(0, len(blocks), GROUP_BLOCKS)]
    half = (len(groups) * PART0_SIXTEENTHS) // 16
    pending = None
    for group in (groups[:half] if part == 0 else groups[half:]):
        ss = scores(group)
        if pending is not None:
            finish(*pending)
        pending = (group, ss)
    finish(*pending)

    if part == 1:
        chunk = 256
        def combine(c, _):
            rows = pl.ds(pl.multiple_of(c * chunk, chunk), chunk)
            ms = [m_scr[p, rows, :] for p in range(3)]
            big = jnp.maximum(jnp.maximum(ms[0], ms[1]), ms[2])
            ws = [jnp.exp(m - big) for m in ms]
            num = ws[0] * o_scr[0, rows, :] + ws[1] * o_scr[1, rows, :] + ws[2] * o_scr[2, rows, :]
            den = ws[0] * l_scr[0, rows, :] + ws[1] * l_scr[1, rows, :] + ws[2] * l_scr[2, rows, :]
            o_ref[rows, :] = ((num / den) * gate_ref[rows, :].astype(F32)).astype(o_ref.dtype)
            return 0
        lax.fori_loop(0, t_total // chunk, combine, 0)


NEAR = 512
SEQS_PER_STEP = 2


def _sample_attn_step(q_ref, kn_ref, vn_ref, gate_ref, kc_ref, vc_ref, o_ref, tt):
    assert 2 * tt == SUBLANES
    scale = HEAD_DIM ** -0.5
    nt = (((1,), (1,)), ((), ()))
    row8 = lax.broadcasted_iota(jnp.int32, (SUBLANES, LANES), 0)
    lane8 = lax.broadcasted_iota(jnp.int32, (SUBLANES, LANES), 1)
    top = row8 < tt
    own_head = top == (lane8 < HEAD_DIM)

    def dup(x, seq):
        rolled = pltpu.roll(x, tt, axis=0)
        return jnp.where(top, x, rolled) if seq == 0 else jnp.where(top, rolled, x)

    jq = lambda shape: lax.broadcasted_iota(jnp.int32, shape, 0) & (tt - 1)
    d_full = WINDOW_MAX + jq((SUBLANES, WINDOW_MAX)) - lax.broadcasted_iota(jnp.int32, (SUBLANES, WINDOW_MAX), 1)
    d_near = NEAR + jq((SUBLANES, NEAR)) - lax.broadcasted_iota(jnp.int32, (SUBLANES, NEAR), 1)
    d_new = jq((SUBLANES, SUBLANES)) - lax.broadcasted_iota(jnp.int32, (SUBLANES, SUBLANES), 1)
    new_key = lax.broadcasted_iota(jnp.int32, (SUBLANES, SUBLANES), 1) < tt
    ok_main, ok_new = [], []
    for window, dil in DILATED_PATTERNS:
        d = d_near if window <= NEAR else d_full
        ok_main.append(((d & (dil - 1)) == 0) & (d <= window))
        ok_new.append(new_key & (d_new >= 0) & ((d_new & (dil - 1)) == 0))

    insts = [(slice(pr * LANES, (pr + 1) * LANES), seq)
             for pr in range(D_ATTN // LANES) for seq in range(SEQS_PER_STEP)]

    scored = []
    for lanes, seq in insts:
        q = jnp.where(own_head, dup(q_ref[:, lanes], seq) * scale, 0.0).astype(BF16)
        kn = dup(kn_ref[:, lanes], seq).astype(BF16)
        s_main = jnp.dot(q, kc_ref[seq, lanes, :].astype(BF16), preferred_element_type=F32)
        s_new = lax.dot_general(q, kn, nt, preferred_element_type=F32)
        scored.append((s_main, s_new))

    weighted = []
    for s_main, s_new in scored:
        stats = []
        for (window, dil), okm, okn in zip(DILATED_PATTERNS, ok_main, ok_new):
            sm = jnp.where(okm, s_main[:, WINDOW_MAX - NEAR:] if window <= NEAR else s_main, NEG_INF)
            sn = jnp.where(okn, s_new, NEG_INF)
            m = jnp.maximum(jnp.max(sm, axis=-1, keepdims=True), jnp.max(sn, axis=-1, keepdims=True))
            pm = jnp.exp(sm - m)
            pn = jnp.exp(sn - m)
            l = jnp.sum(pm, axis=-1, keepdims=True) + jnp.sum(pn, axis=-1, keepdims=True)
            stats.append((m, l, pm, pn))
        big = functools.reduce(jnp.maximum, [s[0] for s in stats])
        ws = [jnp.exp(s[0] - big) for s in stats]
        den = functools.reduce(jnp.add, [w * s[1] for w, s in zip(ws, stats)])
        p_new = functools.reduce(jnp.add, [w * s[3] for w, s in zip(ws, stats)])
        p_near = functools.reduce(jnp.add, [w * s[2] for w, s in zip(ws, stats) if s[2].shape[1] == NEAR])
        p_full = functools.reduce(jnp.add, [w * s[2] for w, s in zip(ws, stats) if s[2].shape[1] != NEAR])
        p_main = jnp.concatenate([p_full[:, 0:WINDOW_MAX - NEAR], p_full[:, WINDOW_MAX - NEAR:] + p_near], axis=1)
        weighted.append((p_main.astype(BF16), p_new.astype(BF16).astype(F32), den))

    halves = []
    for (lanes, seq), (p_main, p_new, den) in zip(insts, weighted):
        acc = lax.dot_general(p_main, vc_ref[seq, lanes, :].astype(BF16), nt, preferred_element_type=F32)
        vn = dup(vn_ref[:, lanes], seq).astype(BF16).astype(F32)
        for j in range(tt):
            acc = acc + p_new[:, j:j + 1] * vn[j:j + 1, :]
        acc = acc / den
        halves.append(jnp.where(lane8 < HEAD_DIM, acc, pltpu.roll(acc, tt, axis=0)))
    for pr in range(D_ATTN // LANES):
        lanes = slice(pr * LANES, (pr + 1) * LANES)
        both = jnp.where(top, halves[SEQS_PER_STEP * pr], pltpu.roll(halves[SEQS_PER_STEP * pr + 1], tt, axis=0))
        o_ref[:, lanes] = both * gate_ref[:, lanes]


def _attention_kernel(sq_ref, skn_ref, svn_ref, sgate_ref, kc_ref, vc_ref, pq_ref, pk_ref, pv_ref, pgate_ref,
                      so_ref, po_ref, o_scr, m_scr, l_scr, band_scr, causal_scr, *, tt):
    for part in range(2):
        @pl.when(pl.program_id(0) % 2 == part)
        def _(part=part):
            _sample_attn_step(sq_ref, skn_ref, svn_ref, sgate_ref, kc_ref, vc_ref, so_ref, tt)
            _prompt_attn_part(pq_ref, pk_ref, pv_ref, pgate_ref, po_ref, o_scr, m_scr, l_scr,
                              band_scr, causal_scr, part)


def _attention(sq, sk_new, sv_new, sgate, cache_kt, cache_vt, pq, pk, pv, pgate, *, dec_batch, tt, batch, t_total):
    pairs = D_ATTN // LANES
    steps = dec_batch // SEQS_PER_STEP
    assert cache_kt.shape == (dec_batch, D_ATTN, WINDOW_MAX) and steps == 2 * batch * pairs
    tok = pl.BlockSpec((SEQS_PER_STEP * tt, D_ATTN), lambda i: (i, 0))
    cache = pl.BlockSpec((SEQS_PER_STEP, D_ATTN, WINDOW_MAX), lambda i: (i, 0, 0))
    shp = (batch, t_total, D_ATTN)
    unit = pl.BlockSpec((None, t_total, LANES), lambda i: (i // (2 * pairs), 0, (i // 2) % pairs))
    so, po = pl.pallas_call(
        functools.partial(_attention_kernel, tt=tt),
        grid=(steps,),
        in_specs=[tok, tok, tok, tok, cache, cache, unit, unit, unit, unit],
        out_specs=[tok, unit],
        out_shape=[jax.ShapeDtypeStruct((dec_batch * tt, D_ATTN), F32), jax.ShapeDtypeStruct(shp, BF16)],
        scratch_shapes=[pltpu.VMEM((3, t_total, LANES), F32)] * 3
                       + [pltpu.VMEM((2 * BAND, 2 * BAND), F32), pltpu.VMEM((2 * BAND, BAND), F32)],
        compiler_params=pltpu.CompilerParams(dimension_semantics=("arbitrary",), vmem_limit_bytes=VMEM_LIMIT),
        name="attention",
    )(sq, sk_new, sv_new, sgate, cache_kt, cache_vt,
      pq.reshape(shp), pk.reshape(shp), pv.reshape(shp), pgate.reshape(shp))
    return so, po.reshape(batch * t_total, D_ATTN)


def _out_proj_kernel(x_ref, ys_ref, ya_ref, ga_ref, w_ref, gf_ref, y_ref):
    ys = jnp.concatenate([ys_ref[s] for s in range(4)], axis=1).astype(BF16)
    ya = (_rms_scale(ya_ref[...]) * ga_ref[...]).astype(BF16)
    mix = jnp.concatenate([ys, ya], axis=1)
    h = x_ref[...] + jnp.dot(mix, w_ref[...], preferred_element_type=F32)
    y_ref[...] = _rms_scale(h) * gf_ref[...]


def _out_proj(x, ys, ya, norm_attn_g, w_out_bf, final_g):
    rows = x.shape[0]
    row_spec = lambda width: pl.BlockSpec((ROW_TILE, width), lambda i: (i, 0))
    const = lambda shape: pl.BlockSpec(shape, lambda i: (0,) * len(shape))
    return pl.pallas_call(
        _out_proj_kernel,
        grid=(rows // ROW_TILE,),
        in_specs=[row_spec(D_MODEL), pl.BlockSpec((4, ROW_TILE, LANES), lambda i: (0, i, 0)), row_spec(D_ATTN),
                  const((1, D_ATTN)), const((D_MODEL, D_MODEL)), const((1, D_MODEL))],
        out_specs=row_spec(D_MODEL),
        out_shape=jax.ShapeDtypeStruct((rows, D_MODEL), F32),
        compiler_params=pltpu.CompilerParams(dimension_semantics=("parallel",), vmem_limit_bytes=VMEM_LIMIT),
        name="out_proj",
    )(x, ys, ya, norm_attn_g.reshape(1, D_ATTN), w_out_bf, final_g.reshape(1, D_MODEL))


def kernel(x_prompt, x_sample, cache_k, cache_v, state_ssm_re, state_ssm_im, norm_in_g, w_in,
           ssm_A_re, ssm_A_im, ssm_log_dt, ssm_B_re, ssm_B_im, ssm_C_re, ssm_C_im, ssm_D,
           w_glu, b_glu, norm_ssm_g, norm_attn_g, w_out, final_norm_g):
    depth = w_in.shape[0]
    assert depth == 1
    batch, seq, _ = x_prompt.shape
    dec_batch, dec_seq, _ = x_sample.shape
    assert batch == SUBLANES and seq == WINDOW_MAX and seq % SSM_STEPS == 0

    lam, bb_re, bb_im = _ssm_prep(ssm_A_re[0], ssm_A_im[0], ssm_log_dt[0], ssm_B_re[0], ssm_B_im[0])
    wb, cc = _block_diag_weights(bb_re, bb_im, ssm_C_re[0], ssm_C_im[0])
    w_in_bf = w_in[0].astype(BF16)
    w_glu_bf = w_glu[0].astype(BF16)
    w_out_bf = w_out[0].astype(BF16)

    ssm = functools.partial(_ssm, lam=lam, wb=wb, cc=cc, d_vec=ssm_D[0], w_glu_bf=w_glu_bf, b_glu=b_glu[0],
                            norm_g=norm_ssm_g[0])

    xp = x_prompt.reshape(batch * seq, D_MODEL)
    xzp, qp, kp, vp, gp, kpt, vpt = _in_proj(xp, norm_in_g[0], w_in_bf, seq_len=seq)
    xs = x_sample.reshape(dec_batch * dec_seq, D_MODEL)
    xzs, qs, ksm, vsm, gs = _in_proj(xs, norm_in_g[0], w_in_bf)

    chan_major = lambda c: c[0].transpose(0, 2, 3, 1).reshape(dec_batch, D_ATTN, WINDOW_MAX)
    yas, yap = _attention(qs, ksm, vsm, gs, chan_major(cache_k), chan_major(cache_v), qp, kp, vp, gp,
                          dec_batch=dec_batch, tt=dec_seq, batch=batch, t_total=seq)

    yp, stp = ssm(xzp, jnp.zeros((batch, 2 * D_STATE), F32), nb=batch, t_total=seq, tt=SSM_STEPS,
                  out_proj=(x_prompt, yap.reshape(batch, seq, D_ATTN), norm_attn_g[0], w_out_bf, final_norm_g))
    h0 = jnp.concatenate([state_ssm_re[0].reshape(dec_batch, D_STATE),
                          state_ssm_im[0].reshape(dec_batch, D_STATE)], axis=1)
    yss, sts = ssm(xzs, h0, nb=dec_batch, t_total=dec_seq, tt=dec_seq)
    ysm = _out_proj(xs, yss, yas, norm_attn_g[0], w_out_bf, final_norm_g)

    heads = lambda a, n, t: a.reshape(1, n, t, N_HEADS, HEAD_DIM)
    heads_t = lambda a, n, t: a.reshape(n, N_HEADS, HEAD_DIM, t).transpose(0, 3, 1, 2)[None]
    state = lambda s, n, part: s[:, part * D_STATE:(part + 1) * D_STATE].reshape(1, n, N_SSM_GROUPS, SSM_STATE)
    return (yp.reshape(batch, seq, D_MODEL), ysm.reshape(dec_batch, dec_seq, D_MODEL),
            heads_t(kpt, batch, seq), heads_t(vpt, batch, seq), state(stp, batch, 0), state(stp, batch, 1),
            heads(ksm, dec_batch, dec_seq), heads(vsm, dec_batch, dec_seq),
            state(sts, dec_batch, 0), state(sts, dec_batch, 1))
```

```python
import functools

import jax
import jax.numpy as jnp
from jax import lax
from jax.experimental import pallas as pl
from jax.experimental.pallas import tpu as pltpu

F32 = jnp.float32
BF16 = jnp.bfloat16

D_MODEL = 1024
D_SSM = 512
D_ATTN = 512
SSM_GROUP = 16
N_SSM_GROUPS = D_SSM // SSM_GROUP
SSM_STATE = 64
D_STATE = N_SSM_GROUPS * SSM_STATE
HEAD_DIM = 64
N_HEADS = D_ATTN // HEAD_DIM
D_IN_PROJ = 2 * D_SSM + 4 * D_ATTN
RMS_EPS = 1e-6
DILATED_PATTERNS = ((128, 1), (512, 4), (2048, 16))
BAND = 128
WINDOW_MAX = 2048
LANES = 128
SUBLANES = 8
VMEM_LIMIT = 56 * 1024 * 1024
ROW_TILE = 512
IN_PROJ_ROW_TILE = 1024
SSM_STEPS = 64
SSM_SUB_STEPS = 32
NEG_INF = float("-inf")


def _sigmoid(x):
    return 1.0 / (1.0 + jnp.exp(-x))


def _rms_scale(x):
    return x * lax.rsqrt(jnp.mean(x * x, axis=-1, keepdims=True) + RMS_EPS)


def _ssm_prep_kernel(are_ref, aim_ref, ldt_ref, bre_ref, bim_ref, lam_ref, bbre_ref, bbim_ref):
    a_re = are_ref[...]
    a_im = aim_ref[...]
    dt = jnp.exp(ldt_ref[...])
    mag = jnp.exp(a_re * dt)
    ang = a_im * dt
    lam_re = mag * jnp.cos(ang)
    lam_im = mag * jnp.sin(ang)
    den = a_re * a_re + a_im * a_im
    f_re = ((lam_re - 1.0) * a_re + lam_im * a_im) / den
    f_im = (lam_im * a_re - (lam_re - 1.0) * a_im) / den
    lam_ref[:, 0:D_STATE] = jnp.broadcast_to(lam_re, (SUBLANES, D_STATE))
    lam_ref[:, D_STATE:2 * D_STATE] = jnp.broadcast_to(lam_im, (SUBLANES, D_STATE))
    b_re = bre_ref[...]
    b_im = bim_ref[...]
    bbre_ref[...] = f_re * b_re - f_im * b_im
    bbim_ref[...] = f_re * b_im + f_im * b_re


def _ssm_prep(a_re, a_im, log_dt, b_re, b_im):
    flat = lambda a: a.reshape(1, D_STATE)
    ldt = jnp.repeat(log_dt, SSM_STATE).reshape(1, D_STATE)
    bt = lambda b: b.transpose(2, 0, 1).reshape(SSM_GROUP, D_STATE)
    return pl.pallas_call(
        _ssm_prep_kernel,
        out_shape=(jax.ShapeDtypeStruct((SUBLANES, 2 * D_STATE), F32),
                   jax.ShapeDtypeStruct((SSM_GROUP, D_STATE), F32),
                   jax.ShapeDtypeStruct((SSM_GROUP, D_STATE), F32)),
        name="ssm_prep",
    )(flat(a_re), flat(a_im), ldt, bt(b_re), bt(b_im))


def _block_diag_weights(bb_re, bb_im, c_re, c_im):
    eye = jnp.eye(8, dtype=F32)
    def place_b(bb):
        b4 = bb.reshape(SSM_GROUP, 4, 8, SSM_STATE).transpose(1, 2, 0, 3)
        return jnp.einsum("kgcp,gh->kgchp", b4, eye).reshape(4, 128, 512)
    def place_c(c):
        c4 = c.reshape(4, 8, SSM_GROUP, SSM_STATE)
        return jnp.einsum("kgcp,gh->khpgc", c4, eye).reshape(4, 512, 128)
    wb = jnp.concatenate([place_b(bb_re), place_b(bb_im)], axis=2).astype(BF16)
    cc = jnp.concatenate([place_c(c_re), -place_c(c_im)], axis=1).astype(BF16)
    return wb, cc


def _in_proj_kernel(x_ref, g_ref, w_ref, xz_ref, q_ref, k_ref, v_ref, za_ref, *t_refs):
    xn = (_rms_scale(x_ref[...]) * g_ref[...]).astype(BF16)
    outs = (None, None, q_ref, k_ref, v_ref, za_ref)
    for j in range(D_IN_PROJ // 512):
        r = jnp.dot(xn, w_ref[:, j * 512:(j + 1) * 512], preferred_element_type=F32)
        if j in (1, 5):
            r = r * _sigmoid(r)
        if j == 2:
            r = r * (HEAD_DIM ** -0.5)
        if j < 2:
            for s in range(4):
                xz_ref[j * 4 + s] = r[:, s * LANES:(s + 1) * LANES]
        else:
            outs[j][...] = r.astype(outs[j].dtype)
            if t_refs and j in (3, 4):
                t_refs[j - 3][...] = r.T


def _in_proj(x, norm_g, w_in_bf, *, seq_len=None):
    rows = x.shape[0]
    tile = min(rows, IN_PROJ_ROW_TILE)
    row_spec = lambda width: pl.BlockSpec((tile, width), lambda i: (i, 0))
    out_specs = [pl.BlockSpec((8, tile, LANES), lambda i: (0, i, 0))] + [row_spec(512)] * 4
    out_shape = ([jax.ShapeDtypeStruct((8, rows, LANES), F32)] + [jax.ShapeDtypeStruct((rows, 512), F32)] * 3
                 + [jax.ShapeDtypeStruct((rows, 512), F32 if seq_len is None else BF16)])
    if seq_len is not None:
        per_seq = seq_len // tile
        out_specs += [pl.BlockSpec((None, 512, tile), lambda i: (i // per_seq, 0, i % per_seq))] * 2
        out_shape += [jax.ShapeDtypeStruct((rows // seq_len, 512, seq_len), F32)] * 2
    return pl.pallas_call(
        _in_proj_kernel,
        grid=(rows // tile,),
        in_specs=[row_spec(D_MODEL),
                  pl.BlockSpec((1, D_MODEL), lambda i: (0, 0)),
                  pl.BlockSpec((D_MODEL, D_IN_PROJ), lambda i: (0, 0))],
        out_specs=out_specs,
        out_shape=out_shape,
        compiler_params=pltpu.CompilerParams(dimension_semantics=("parallel",), vmem_limit_bytes=VMEM_LIMIT),
        name="in_proj",
    )(x, norm_g.reshape(1, D_MODEL), w_in_bf)


def _ssm_kernel(xz_ref, h0_ref, lam_ref, wb_ref, cc_ref, d_ref, wglu_ref, bglu_ref, g_ref, *rest, nb, tt):
    rows = nb * tt
    prompt = xz_ref.ndim == 4
    with_out_proj = len(rest) > 5
    if with_out_proj:
        x_ref, ya_ref, ga_ref, wout_ref, gf_ref, y_ref, st_ref, u_scr, bu_scr, ys_scr, mix_scr = rest
    else:
        ys_ref, st_ref, u_scr, bu_scr, ys_scr = rest

    @pl.when(pl.program_id(0) == 0)
    def _():
        st_ref[...] = h0_ref[...]

    for s in range(8):
        if prompt:
            for b in range(nb):
                u_scr[s, pl.ds(b, tt, stride=nb), :] = xz_ref[s, b]
        else:
            for t in range(tt):
                u_scr[s, t * nb:(t + 1) * nb, :] = xz_ref[s, pl.ds(t, nb, stride=tt), :]

    sub = min(tt, SSM_SUB_STEPS)
    sub_rows = [slice(j * sub * nb, (j + 1) * sub * nb) for j in range(tt // sub)]

    def expand(rs):
        for kt in range(4):
            r = jnp.dot(u_scr[kt, rs, :].astype(BF16), wb_ref[kt], preferred_element_type=F32)
            bu_scr[rs, kt * 512:(kt + 1) * 512] = r[:, 0:512]
            bu_scr[rs, D_STATE + kt * 512:D_STATE + (kt + 1) * 512] = r[:, 512:1024]

    width = (SUBLANES * LANES * 8) // nb if nb <= 64 else LANES
    chunks = [(slice(c * width, (c + 1) * width), slice(D_STATE + c * width, D_STATE + (c + 1) * width))
              for c in range(D_STATE // width)]

    def recur(j):
        for re_l, im_l in chunks:
            if nb == SUBLANES:
                lam_re, lam_im = lam_ref[:, re_l], lam_ref[:, im_l]
            else:
                lam_re = jnp.broadcast_to(lam_ref[0:1, re_l], (nb, width))
                lam_im = jnp.broadcast_to(lam_ref[0:1, im_l], (nb, width))
            s_re, s_im = st_ref[:, re_l], st_ref[:, im_l]
            for t in range(j * sub, (j + 1) * sub):
                tr = slice(t * nb, (t + 1) * nb)
                s_re, s_im = (lam_re * s_re - lam_im * s_im + bu_scr[tr, re_l],
                              lam_re * s_im + lam_im * s_re + bu_scr[tr, im_l])
                bu_scr[tr, re_l] = s_re
                bu_scr[tr, im_l] = s_im
            st_ref[:, re_l] = s_re
            st_ref[:, im_l] = s_im

    def project(rs):
        ys = []
        for kt in range(4):
            xc = jnp.concatenate([bu_scr[rs, kt * 512:(kt + 1) * 512],
                                  bu_scr[rs, D_STATE + kt * 512:D_STATE + (kt + 1) * 512]], axis=1).astype(BF16)
            y = jnp.dot(xc, cc_ref[kt], preferred_element_type=F32)
            ys.append(y + d_ref[:, kt * LANES:(kt + 1) * LANES] * u_scr[kt, rs, :])
        y = jnp.concatenate(ys, axis=1)
        g = jax.nn.gelu(y)
        gl = jnp.dot(g.astype(BF16), wglu_ref[...], preferred_element_type=F32) + bglu_ref[...]
        gate = jnp.concatenate([u_scr[4 + s, rs, :] for s in range(4)], axis=1)
        y = gl[:, 0:D_SSM] * _sigmoid(gl[:, D_SSM:2 * D_SSM]) * gate
        y = _rms_scale(y) * g_ref[...]
        for s in range(4):
            ys_scr[s, rs, :] = y[:, s * LANES:(s + 1) * LANES]

    n_sub = len(sub_rows)
    for j in range(n_sub + 2):
        if j < n_sub:
            expand(sub_rows[j])
        if 1 <= j <= n_sub:
            recur(j - 1)
        if j >= 2:
            project(sub_rows[j - 2])

    if with_out_proj:
        for b in range(nb):
            rb = slice(b * tt, (b + 1) * tt)
            for s in range(4):
                mix_scr[rb, s * LANES:(s + 1) * LANES] = ys_scr[s, pl.ds(b, tt, stride=nb), :].astype(BF16)
            mix_scr[rb, D_SSM:D_SSM + D_ATTN] = (_rms_scale(ya_ref[b].astype(F32)) * ga_ref[...]).astype(BF16)
        h = x_ref[...].reshape(rows, D_MODEL) + jnp.dot(mix_scr[...], wout_ref[...], preferred_element_type=F32)
        y_ref[...] = (_rms_scale(h) * gf_ref[...]).reshape(nb, tt, D_MODEL)
        return
    for s in range(4):
        if prompt:
            for b in range(nb):
                ys_ref[s, b] = ys_scr[s, pl.ds(b, tt, stride=nb), :].astype(ys_ref.dtype)
        else:
            for t in range(tt):
                ys_ref[s, pl.ds(t, nb, stride=tt), :] = ys_scr[s, t * nb:(t + 1) * nb, :]


def _ssm(xz, h0, lam, wb, cc, d_vec, w_glu_bf, b_glu, norm_g, *, nb, t_total, tt, out_proj=None):
    rows = nb * tt
    const = lambda shape: pl.BlockSpec(shape, lambda i: (0,) * len(shape))
    seq_block = lambda width: pl.BlockSpec((nb, tt, width), lambda i: (0, i, 0))
    args = [xz, h0, lam, wb, cc, d_vec.reshape(1, D_SSM), w_glu_bf, b_glu.reshape(1, 2 * D_SSM),
            norm_g.reshape(1, D_SSM)]
    in_specs = [None, const((nb, 2 * D_STATE)), const((SUBLANES, 2 * D_STATE)),
                const((4, 128, 1024)), const((4, 1024, 128)), const((1, D_SSM)),
                const((D_SSM, 2 * D_SSM)), const((1, 2 * D_SSM)), const((1, D_SSM))]
    scratch = [pltpu.VMEM((8, rows, LANES), F32), pltpu.VMEM((rows, 2 * D_STATE), F32),
               pltpu.VMEM((4, rows, LANES), F32)]
    if t_total > tt:
        args[0] = xz.reshape(8, nb, t_total, LANES)
        in_specs[0] = pl.BlockSpec((8, nb, tt, LANES), lambda i: (0, 0, i, 0))
        out_spec = pl.BlockSpec((4, nb, tt, LANES), lambda i: (0, 0, i, 0))
        out_shape = jax.ShapeDtypeStruct((4, nb, t_total, LANES), BF16)
    else:
        in_specs[0] = const((8, rows, LANES))
        out_spec = const((4, rows, LANES))
        out_shape = jax.ShapeDtypeStruct((4, rows, LANES), F32)
    if out_proj is not None:
        x, ya, norm_attn_g, w_out_bf, final_g = out_proj
        args += [x, ya, norm_attn_g.reshape(1, D_ATTN), w_out_bf, final_g.reshape(1, D_MODEL)]
        in_specs += [seq_block(D_MODEL), seq_block(D_ATTN), const((1, D_ATTN)), const((D_MODEL, D_MODEL)),
                     const((1, D_MODEL))]
        out_spec = seq_block(D_MODEL)
        out_shape = jax.ShapeDtypeStruct((nb, t_total, D_MODEL), F32)
        scratch.append(pltpu.VMEM((rows, D_MODEL), BF16))
    out, st = pl.pallas_call(
        functools.partial(_ssm_kernel, nb=nb, tt=tt),
        grid=(t_total // tt,),
        in_specs=in_specs,
        out_specs=[out_spec, const((nb, 2 * D_STATE))],
        out_shape=[out_shape, jax.ShapeDtypeStruct((nb, 2 * D_STATE), F32)],
        scratch_shapes=scratch,
        compiler_params=pltpu.CompilerParams(dimension_semantics=("arbitrary",), vmem_limit_bytes=VMEM_LIMIT),
        name="ssm_prompt" if t_total > tt else "ssm_sample",
    )(*args)
    return (out if out_proj is not None else out.reshape(4, nb * t_total, LANES)), st


PART0_SIXTEENTHS = 11
GROUP_BLOCKS = 2


def _band_scores(qs, ks, biases):
    low = lax.broadcasted_iota(jnp.int32, (BAND, LANES), 1) < HEAD_DIM
    nt = (((1,), (1,)), ((), ()))
    stacked = []
    for q in qs:
        stacked.append(jnp.concatenate([jnp.where(low, q, 0.0), jnp.where(low, 0.0, q)], axis=0).astype(BF16))
    return [lax.dot_general(q, k.astype(BF16), nt, preferred_element_type=F32) + bias
            for q, k, bias in zip(stacked, ks, biases)]


def _band_softmax_pv(ss, vs):
    low = lax.broadcasted_iota(jnp.int32, (BAND, LANES), 1) < HEAD_DIM
    ms = [jnp.max(s, axis=-1, keepdims=True) for s in ss]
    ps = [jnp.exp(s - m).astype(BF16) for s, m in zip(ss, ms)]
    outs = [jnp.dot(p, jnp.concatenate([v.astype(BF16), jnp.ones(v.shape, BF16)], axis=1),
                    preferred_element_type=F32) for p, v in zip(ps, vs)]
    wide = lambda a: jnp.broadcast_to(a, (BAND, a.shape[1] if a.shape[1] > 1 else LANES))
    pick = lambda a: jnp.where(low, wide(a[0:BAND]), wide(a[BAND:2 * BAND]))
    return [(pick(o[:, 0:LANES]), pick(m), pick(o[:, LANES:2 * LANES])) for o, m in zip(outs, ms)]


def _prompt_attn_part(q_ref, k_ref, v_ref, gate_ref, o_ref, o_scr, m_scr, l_scr, band_scr, causal_scr, part):
    t_total = q_ref.shape[0]
    if part == 0:
        for scr, kwin in ((band_scr, 2 * BAND), (causal_scr, BAND)):
            qq = lax.broadcasted_iota(jnp.int32, (2 * BAND, kwin), 0) & (BAND - 1)
            kk = lax.broadcasted_iota(jnp.int32, (2 * BAND, kwin), 1)
            valid = ((kk >= qq) & (kk <= qq + BAND)) if kwin > BAND else (kk <= qq)
            scr[...] = jnp.where(valid, 0.0, NEG_INF)

    blocks = []
    for pidx, (window, dil) in enumerate(DILATED_PATTERNS):
        nblk = t_total // dil // BAND
        rows = (lambda r0, n, dil=dil: pl.ds(r0, n, stride=dil) if dil > 1 else pl.ds(r0, n))
        for rho in range(dil):
            blocks.append((pidx, rows(rho, BAND), rows(rho, BAND), causal_scr))
            blocks += [(pidx, rows(rho + dil * BAND * i, BAND), rows(rho + dil * BAND * (i - 1), 2 * BAND), band_scr)
                       for i in range(1, nblk)]

    def scores(group):
        return _band_scores([q_ref[qr, :] for _, qr, _, _ in group], [k_ref[kr, :] for _, _, kr, _ in group],
                            [bias[...] for _, _, _, bias in group])

    def finish(group, ss):
        res = _band_softmax_pv(ss, [v_ref[kr, :] for _, _, kr, _ in group])
        for (pidx, qr, _, _), (o, m, l) in zip(group, res):
            o_scr[pidx, qr, :] = o
            m_scr[pidx, qr, :] = m
            l_scr[pidx, qr, :] = l

    groups = [blocks[g:g + GROUP_BLOCKS] for g in range(0, len(blocks), GROUP_BLOCKS)]
    half = (len(groups) * PART0_SIXTEENTHS) // 16
    pending = None
    for group in (groups[:half] if part == 0 else groups[half:]):
        ss = scores(group)
        if pending is not None:
            finish(*pending)
        pending = (group, ss)
    finish(*pending)

    if part == 1:
        chunk = 256
        def combine(c, _):
            rows = pl.ds(pl.multiple_of(c * chunk, chunk), chunk)
            ms = [m_scr[p, rows, :] for p in range(3)]
            big = jnp.maximum(jnp.maximum(ms[0], ms[1]), ms[2])
            ws = [jnp.exp(m - big) for m in ms]
            num = ws[0] * o_scr[0, rows, :] + ws[1] * o_scr[1, rows, :] + ws[2] * o_scr[2, rows, :]
            den = ws[0] * l_scr[0, rows, :] + ws[1] * l_scr[1, rows, :] + ws[2] * l_scr[2, rows, :]
            o_ref[rows, :] = ((num / den) * gate_ref[rows, :].astype(F32)).astype(o_ref.dtype)
            return 0
        lax.fori_loop(0, t_total // chunk, combine, 0)


NEAR = 512
SEQS_PER_STEP = 2


def _sample_attn_step(q_ref, kn_ref, vn_ref, gate_ref, kc_ref, vc_ref, o_ref, tt):
    assert 2 * tt == SUBLANES
    nt = (((1,), (1,)), ((), ()))
    row8 = lax.broadcasted_iota(jnp.int32, (SUBLANES, LANES), 0)
    lane8 = lax.broadcasted_iota(jnp.int32, (SUBLANES, LANES), 1)
    top = row8 < tt
    own_head = top == (lane8 < HEAD_DIM)

    def dup(x, seq):
        rolled = pltpu.roll(x, tt, axis=0)
        return jnp.where(top, x, rolled) if seq == 0 else jnp.where(top, rolled, x)

    jq = lambda shape: lax.broadcasted_iota(jnp.int32, shape, 0) & (tt - 1)
    d_full = WINDOW_MAX + jq((SUBLANES, WINDOW_MAX)) - lax.broadcasted_iota(jnp.int32, (SUBLANES, WINDOW_MAX), 1)
    d_near = NEAR + jq((SUBLANES, NEAR)) - lax.broadcasted_iota(jnp.int32, (SUBLANES, NEAR), 1)
    d_new = jq((SUBLANES, SUBLANES)) - lax.broadcasted_iota(jnp.int32, (SUBLANES, SUBLANES), 1)
    new_key = lax.broadcasted_iota(jnp.int32, (SUBLANES, SUBLANES), 1) < tt
    ok_main, ok_new = [], []
    for window, dil in DILATED_PATTERNS:
        d = d_near if window <= NEAR else d_full
        ok_main.append(((d & (dil - 1)) == 0) & (d <= window))
        ok_new.append(new_key & (d_new >= 0) & ((d_new & (dil - 1)) == 0))

    insts = [(slice(pr * LANES, (pr + 1) * LANES), seq)
             for pr in range(D_ATTN // LANES) for seq in range(SEQS_PER_STEP)]

    scored = []
    for lanes, seq in insts:
        q = jnp.where(own_head, dup(q_ref[:, lanes], seq), 0.0).astype(BF16)
        kn = dup(kn_ref[:, lanes], seq).astype(BF16)
        s_main = jnp.dot(q, kc_ref[seq, lanes, :].astype(BF16), preferred_element_type=F32)
        s_new = lax.dot_general(q, kn, nt, preferred_element_type=F32)
        scored.append((s_main, s_new))

    weighted = []
    for s_main, s_new in scored:
        stats = []
        for (window, dil), okm, okn in zip(DILATED_PATTERNS, ok_main, ok_new):
            sm = jnp.where(okm, s_main[:, WINDOW_MAX - NEAR:] if window <= NEAR else s_main, NEG_INF)
            sn = jnp.where(okn, s_new, NEG_INF)
            m = jnp.maximum(jnp.max(sm, axis=-1, keepdims=True), jnp.max(sn, axis=-1, keepdims=True))
            pm = jnp.exp(sm - m)
            pn = jnp.exp(sn - m)
            l = jnp.sum(pm, axis=-1, keepdims=True) + jnp.sum(pn, axis=-1, keepdims=True)
            stats.append((m, l, pm, pn))
        big = functools.reduce(jnp.maximum, [s[0] for s in stats])
        ws = [jnp.exp(s[0] - big) for s in stats]
        den = functools.reduce(jnp.add, [w * s[1] for w, s in zip(ws, stats)])
        p_new = functools.reduce(jnp.add, [w * s[3] for w, s in zip(ws, stats)])
        p_near = functools.reduce(jnp.add, [w * s[2] for w, s in zip(ws, stats) if s[2].shape[1] == NEAR])
        p_full = functools.reduce(jnp.add, [w * s[2] for w, s in zip(ws, stats) if s[2].shape[1] != NEAR])
        p_main = jnp.concatenate([p_full[:, 0:WINDOW_MAX - NEAR], p_full[:, WINDOW_MAX - NEAR:] + p_near], axis=1)
        weighted.append((p_main.astype(BF16), p_new.astype(BF16).astype(F32), den))

    halves = []
    for (lanes, seq), (p_main, p_new, den) in zip(insts, weighted):
        acc = lax.dot_general(p_main, vc_ref[seq, lanes, :].astype(BF16), nt, preferred_element_type=F32)
        vn = dup(vn_ref[:, lanes], seq).astype(BF16).astype(F32)
        for j in range(tt):
            acc = acc + p_new[:, j:j + 1] * vn[j:j + 1, :]
        acc = acc / den
        halves.append(jnp.where(lane8 < HEAD_DIM, acc, pltpu.roll(acc, tt, axis=0)))
    for pr in range(D_ATTN // LANES):
        lanes = slice(pr * LANES, (pr + 1) * LANES)
        both = jnp.where(top, halves[SEQS_PER_STEP * pr], pltpu.roll(halves[SEQS_PER_STEP * pr + 1], tt, axis=0))
        o_ref[:, lanes] = both * gate_ref[:, lanes]


def _attention_kernel(sq_ref, skn_ref, svn_ref, sgate_ref, kc_ref, vc_ref, pq_ref, pk_ref, pv_ref, pgate_ref,
                      so_ref, po_ref, o_scr, m_scr, l_scr, band_scr, causal_scr, *, tt):
    for part in range(2):
        @pl.when(pl.program_id(0) % 2 == part)
        def _(part=part):
            _sample_attn_step(sq_ref, skn_ref, svn_ref, sgate_ref, kc_ref, vc_ref, so_ref, tt)
            _prompt_attn_part(pq_ref, pk_ref, pv_ref, pgate_ref, po_ref, o_scr, m_scr, l_scr,
                              band_scr, causal_scr, part)


def _attention(sq, sk_new, sv_new, sgate, cache_kt, cache_vt, pq, pk, pv, pgate, *, dec_batch, tt, batch, t_total):
    pairs = D_ATTN // LANES
    steps = dec_batch // SEQS_PER_STEP
    assert cache_kt.shape == (dec_batch, D_ATTN, WINDOW_MAX) and steps == 2 * batch * pairs
    tok = pl.BlockSpec((SEQS_PER_STEP * tt, D_ATTN), lambda i: (i, 0))
    cache = pl.BlockSpec((SEQS_PER_STEP, D_ATTN, WINDOW_MAX), lambda i: (i, 0, 0))
    shp = (batch, t_total, D_ATTN)
    unit = pl.BlockSpec((None, t_total, LANES), lambda i: (i // (2 * pairs), 0, (i // 2) % pairs))
    so, po = pl.pallas_call(
        functools.partial(_attention_kernel, tt=tt),
        grid=(steps,),
        in_specs=[tok, tok, tok, tok, cache, cache, unit, unit, unit, unit],
        out_specs=[tok, unit],
        out_shape=[jax.ShapeDtypeStruct((dec_batch * tt, D_ATTN), F32), jax.ShapeDtypeStruct(shp, BF16)],
        scratch_shapes=[pltpu.VMEM((3, t_total, LANES), F32)] * 3
                       + [pltpu.VMEM((2 * BAND, 2 * BAND), F32), pltpu.VMEM((2 * BAND, BAND), F32)],
        compiler_params=pltpu.CompilerParams(dimension_semantics=("arbitrary",), vmem_limit_bytes=VMEM_LIMIT),
        name="attention",
    )(sq, sk_new, sv_new, sgate, cache_kt, cache_vt,
      pq.reshape(shp), pk.reshape(shp), pv.reshape(shp), pgate.reshape(shp))
    return so, po.reshape(batch * t_total, D_ATTN)


def _out_proj_kernel(x_ref, ys_ref, ya_ref, ga_ref, w_ref, gf_ref, y_ref):
    ys = jnp.concatenate([ys_ref[s] for s in range(4)], axis=1).astype(BF16)
    ya = (_rms_scale(ya_ref[...]) * ga_ref[...]).astype(BF16)
    mix = jnp.concatenate([ys, ya], axis=1)
    h = x_ref[...] + jnp.dot(mix, w_ref[...], preferred_element_type=F32)
    y_ref[...] = _rms_scale(h) * gf_ref[...]


def _out_proj(x, ys, ya, norm_attn_g, w_out_bf, final_g):
    rows = x.shape[0]
    row_spec = lambda width: pl.BlockSpec((ROW_TILE, width), lambda i: (i, 0))
    const = lambda shape: pl.BlockSpec(shape, lambda i: (0,) * len(shape))
    return pl.pallas_call(
        _out_proj_kernel,
        grid=(rows // ROW_TILE,),
        in_specs=[row_spec(D_MODEL), pl.BlockSpec((4, ROW_TILE, LANES), lambda i: (0, i, 0)), row_spec(D_ATTN),
                  const((1, D_ATTN)), const((D_MODEL, D_MODEL)), const((1, D_MODEL))],
        out_specs=row_spec(D_MODEL),
        out_shape=jax.ShapeDtypeStruct((rows, D_MODEL), F32),
        compiler_params=pltpu.CompilerParams(dimension_semantics=("parallel",), vmem_limit_bytes=VMEM_LIMIT),
        name="out_proj",
    )(x, ys, ya, norm_attn_g.reshape(1, D_ATTN), w_out_bf, final_g.reshape(1, D_MODEL))


def kernel(x_prompt, x_sample, cache_k, cache_v, state_ssm_re, state_ssm_im, norm_in_g, w_in,
           ssm_A_re, ssm_A_im, ssm_log_dt, ssm_B_re, ssm_B_im, ssm_C_re, ssm_C_im, ssm_D,
           w_glu, b_glu, norm_ssm_g, norm_attn_g, w_out, final_norm_g):
    depth = w_in.shape[0]
    assert depth == 1
    batch, seq, _ = x_prompt.shape
    dec_batch, dec_seq, _ = x_sample.shape
    assert batch == SUBLANES and seq == WINDOW_MAX and seq % SSM_STEPS == 0

    lam, bb_re, bb_im = _ssm_prep(ssm_A_re[0], ssm_A_im[0], ssm_log_dt[0], ssm_B_re[0], ssm_B_im[0])
    wb, cc = _block_diag_weights(bb_re, bb_im, ssm_C_re[0], ssm_C_im[0])
    w_in_bf = w_in[0].astype(BF16)
    w_glu_bf = w_glu[0].astype(BF16)
    w_out_bf = w_out[0].astype(BF16)

    ssm = functools.partial(_ssm, lam=lam, wb=wb, cc=cc, d_vec=ssm_D[0], w_glu_bf=w_glu_bf, b_glu=b_glu[0],
                            norm_g=norm_ssm_g[0])

    xp = x_prompt.reshape(batch * seq, D_MODEL)
    xzp, qp, kp, vp, gp, kpt, vpt = _in_proj(xp, norm_in_g[0], w_in_bf, seq_len=seq)
    xs = x_sample.reshape(dec_batch * dec_seq, D_MODEL)
    xzs, qs, ksm, vsm, gs = _in_proj(xs, norm_in_g[0], w_in_bf)

    chan_major = lambda c: c[0].transpose(0, 2, 3, 1).reshape(dec_batch, D_ATTN, WINDOW_MAX)
    yas, yap = _attention(qs, ksm, vsm, gs, chan_major(cache_k), chan_major(cache_v), qp, kp, vp, gp,
                          dec_batch=dec_batch, tt=dec_seq, batch=batch, t_total=seq)

    yp, stp = ssm(xzp, jnp.zeros((batch, 2 * D_STATE), F32), nb=batch, t_total=seq, tt=SSM_STEPS,
                  out_proj=(x_prompt, yap.reshape(batch, seq, D_ATTN), norm_attn_g[0], w_out_bf, final_norm_g))
    h0 = jnp.concatenate([state_ssm_re[0].reshape(dec_batch, D_STATE),
                          state_ssm_im[0].reshape(dec_batch, D_STATE)], axis=1)
    yss, sts = ssm(xzs, h0, nb=dec_batch, t_total=dec_seq, tt=dec_seq)
    ysm = _out_proj(xs, yss, yas, norm_attn_g[0], w_out_bf, final_norm_g)

    heads = lambda a, n, t: a.reshape(1, n, t, N_HEADS, HEAD_DIM)
    heads_t = lambda a, n, t: a.reshape(n, N_HEADS, HEAD_DIM, t).transpose(0, 3, 1, 2)[None]
    state = lambda s, n, part: s[:, part * D_STATE:(part + 1) * D_STATE].reshape(1, n, N_SSM_GROUPS, SSM_STATE)
    return (yp.reshape(batch, seq, D_MODEL), ysm.reshape(dec_batch, dec_seq, D_MODEL),
            heads_t(kpt, batch, seq), heads_t(vpt, batch, seq), state(stp, batch, 0), state(stp, batch, 1),
            heads(ksm, dec_batch, dec_seq), heads(vsm, dec_batch, dec_seq),
            state(sts, dec_batch, 0), state(sts, dec_batch, 1))
```

```python
import functools

import jax
import jax.numpy as jnp
from jax import lax
from jax.experimental import pallas as pl
from jax.experimental.pallas import tpu as pltpu

F32 = jnp.float32
BF16 = jnp.bfloat16

D_MODEL = 1024
D_SSM = 512
D_ATTN = 512
SSM_GROUP = 16
N_SSM_GROUPS = D_SSM // SSM_GROUP
SSM_STATE = 64
D_STATE = N_SSM_GROUPS * SSM_STATE
HEAD_DIM = 64
N_HEADS = D_ATTN // HEAD_DIM
D_IN_PROJ = 2 * D_SSM + 4 * D_ATTN
RMS_EPS = 1e-6
DILATED_PATTERNS = ((128, 1), (512, 4), (2048, 16))
BAND = 128
WINDOW_MAX = 2048
LANES = 128
SUBLANES = 8
VMEM_LIMIT = 56 * 1024 * 1024
ROW_TILE = 512
IN_PROJ_ROW_TILE = 1024
SSM_STEPS = 64
SSM_SUB_STEPS = 32
NEG_INF = float("-inf")


def _sigmoid(x):
    return 1.0 / (1.0 + jnp.exp(-x))


def _rms_scale(x):
    return x * lax.rsqrt(jnp.mean(x * x, axis=-1, keepdims=True) + RMS_EPS)


def _ssm_prep_kernel(are_ref, aim_ref, ldt_ref, bre_ref, bim_ref, lam_ref, bbre_ref, bbim_ref):
    a_re = are_ref[...]
    a_im = aim_ref[...]
    dt = jnp.exp(ldt_ref[...])
    mag = jnp.exp(a_re * dt)
    ang = a_im * dt
    lam_re = mag * jnp.cos(ang)
    lam_im = mag * jnp.sin(ang)
    den = a_re * a_re + a_im * a_im
    f_re = ((lam_re - 1.0) * a_re + lam_im * a_im) / den
    f_im = (lam_im * a_re - (lam_re - 1.0) * a_im) / den
    lam_ref[:, 0:D_STATE] = jnp.broadcast_to(lam_re, (SUBLANES, D_STATE))
    lam_ref[:, D_STATE:2 * D_STATE] = jnp.broadcast_to(lam_im, (SUBLANES, D_STATE))
    b_re = bre_ref[...]
    b_im = bim_ref[...]
    bbre_ref[...] = f_re * b_re - f_im * b_im
    bbim_ref[...] = f_re * b_im + f_im * b_re


def _ssm_prep(a_re, a_im, log_dt, b_re, b_im):
    flat = lambda a: a.reshape(1, D_STATE)
    ldt = jnp.repeat(log_dt, SSM_STATE).reshape(1, D_STATE)
    bt = lambda b: b.transpose(2, 0, 1).reshape(SSM_GROUP, D_STATE)
    return pl.pallas_call(
        _ssm_prep_kernel,
        out_shape=(jax.ShapeDtypeStruct((SUBLANES, 2 * D_STATE), F32),
                   jax.ShapeDtypeStruct((SSM_GROUP, D_STATE), F32),
                   jax.ShapeDtypeStruct((SSM_GROUP, D_STATE), F32)),
        name="ssm_prep",
    )(flat(a_re), flat(a_im), ldt, bt(b_re), bt(b_im))


def _block_diag_weights(bb_re, bb_im, c_re, c_im):
    eye = jnp.eye(8, dtype=F32)
    def place_b(bb):
        b4 = bb.reshape(SSM_GROUP, 4, 8, SSM_STATE).transpose(1, 2, 0, 3)
        return jnp.einsum("kgcp,gh->kgchp", b4, eye).reshape(4, 128, 512)
    def place_c(c):
        c4 = c.reshape(4, 8, SSM_GROUP, SSM_STATE)
        return jnp.einsum("kgcp,gh->khpgc", c4, eye).reshape(4, 512, 128)
    wb = jnp.concatenate([place_b(bb_re), place_b(bb_im)], axis=2).astype(BF16)
    cc = jnp.concatenate([place_c(c_re), -place_c(c_im)], axis=1).astype(BF16)
    return wb, cc


def _in_proj_kernel(x_ref, g_ref, w_ref, xz_ref, q_ref, k_ref, v_ref, za_ref, *t_refs):
    xn = (_rms_scale(x_ref[...]) * g_ref[...]).astype(BF16)
    outs = (None, None, q_ref, k_ref, v_ref, za_ref)
    for j in range(D_IN_PROJ // 512):
        r = jnp.dot(xn, w_ref[:, j * 512:(j + 1) * 512], preferred_element_type=F32)
        if j in (1, 5):
            r = r * _sigmoid(r)
        if j == 2:
            r = r * (HEAD_DIM ** -0.5)
        if j < 2:
            for s in range(4):
                xz_ref[j * 4 + s] = r[:, s * LANES:(s + 1) * LANES]
        else:
            outs[j][...] = r.astype(outs[j].dtype)
            if t_refs and j in (3, 4):
                t_refs[j - 3][...] = r.T


def _in_proj(x, norm_g, w_in_bf, *, seq_len=None):
    rows = x.shape[0]
    tile = min(rows, IN_PROJ_ROW_TILE)
    row_spec = lambda width: pl.BlockSpec((tile, width), lambda i: (i, 0))
    out_specs = [pl.BlockSpec((8, tile, LANES), lambda i: (0, i, 0))] + [row_spec(512)] * 4
    out_shape = ([jax.ShapeDtypeStruct((8, rows, LANES), F32)] + [jax.ShapeDtypeStruct((rows, 512), F32)] * 3
                 + [jax.ShapeDtypeStruct((rows, 512), F32 if seq_len is None else BF16)])
    if seq_len is not None:
        per_seq = seq_len // tile
        out_specs += [pl.BlockSpec((None, 512, tile), lambda i: (i // per_seq, 0, i % per_seq))] * 2
        out_shape += [jax.ShapeDtypeStruct((rows // seq_len, 512, seq_len), F32)] * 2
    return pl.pallas_call(
        _in_proj_kernel,
        grid=(rows // tile,),
        in_specs=[row_spec(D_MODEL),
                  pl.BlockSpec((1, D_MODEL), lambda i: (0, 0)),
                  pl.BlockSpec((D_MODEL, D_IN_PROJ), lambda i: (0, 0))],
        out_specs=out_specs,
        out_shape=out_shape,
        compiler_params=pltpu.CompilerParams(dimension_semantics=("parallel",), vmem_limit_bytes=VMEM_LIMIT),
        name="in_proj",
    )(x, norm_g.reshape(1, D_MODEL), w_in_bf)


def _ssm_kernel(xz_ref, h0_ref, lam_ref, wb_ref, cc_ref, d_ref, wglu_ref, bglu_ref, g_ref, *rest, nb, tt,
                guest_half=None, guest_tt=None):
    rows = nb * tt
    prompt = xz_ref.ndim == 4
    with_out_proj = len(rest) > 5
    if guest_half is not None:
        (x_ref, ya_ref, ga_ref, wout_ref, gf_ref, *guest_in, y_ref, st_ref, guest_out,
         u_scr, bu_scr, ys_scr, mix_scr) = rest
        _sample_attn_seqs([(*guest_in, guest_out, guest_half(pl.program_id(0)))], guest_tt)
    elif with_out_proj:
        x_ref, ya_ref, ga_ref, wout_ref, gf_ref, y_ref, st_ref, u_scr, bu_scr, ys_scr, mix_scr = rest
    else:
        ys_ref, st_ref, u_scr, bu_scr, ys_scr = rest

    @pl.when(pl.program_id(0) == 0)
    def _():
        st_ref[...] = h0_ref[...]

    for s in range(8):
        if prompt:
            for b in range(nb):
                u_scr[s, pl.ds(b, tt, stride=nb), :] = xz_ref[s, b]
        else:
            for t in range(tt):
                u_scr[s, t * nb:(t + 1) * nb, :] = xz_ref[s, pl.ds(t, nb, stride=tt), :]

    sub = min(tt, SSM_SUB_STEPS)
    sub_rows = [slice(j * sub * nb, (j + 1) * sub * nb) for j in range(tt // sub)]

    def expand(rs):
        for kt in range(4):
            r = jnp.dot(u_scr[kt, rs, :].astype(BF16), wb_ref[kt], preferred_element_type=F32)
            bu_scr[rs, kt * 512:(kt + 1) * 512] = r[:, 0:512]
            bu_scr[rs, D_STATE + kt * 512:D_STATE + (kt + 1) * 512] = r[:, 512:1024]

    width = (SUBLANES * LANES * 8) // nb if nb <= 64 else LANES
    chunks = [(slice(c * width, (c + 1) * width), slice(D_STATE + c * width, D_STATE + (c + 1) * width))
              for c in range(D_STATE // width)]

    def recur(j):
        for re_l, im_l in chunks:
            if nb == SUBLANES:
                lam_re, lam_im = lam_ref[:, re_l], lam_ref[:, im_l]
            else:
                lam_re = jnp.broadcast_to(lam_ref[0:1, re_l], (nb, width))
                lam_im = jnp.broadcast_to(lam_ref[0:1, im_l], (nb, width))
            s_re, s_im = st_ref[:, re_l], st_ref[:, im_l]
            for t in range(j * sub, (j + 1) * sub):
                tr = slice(t * nb, (t + 1) * nb)
                s_re, s_im = (lam_re * s_re - lam_im * s_im + bu_scr[tr, re_l],
                              lam_re * s_im + lam_im * s_re + bu_scr[tr, im_l])
                bu_scr[tr, re_l] = s_re
                bu_scr[tr, im_l] = s_im
            st_ref[:, re_l] = s_re
            st_ref[:, im_l] = s_im

    def project(rs):
        ys = []
        for kt in range(4):
            xc = jnp.concatenate([bu_scr[rs, kt * 512:(kt + 1) * 512],
                                  bu_scr[rs, D_STATE + kt * 512:D_STATE + (kt + 1) * 512]], axis=1).astype(BF16)
            y = jnp.dot(xc, cc_ref[kt], preferred_element_type=F32)
            ys.append(y + d_ref[:, kt * LANES:(kt + 1) * LANES] * u_scr[kt, rs, :])
        y = jnp.concatenate(ys, axis=1)
        g = jax.nn.gelu(y)
        gl = jnp.dot(g.astype(BF16), wglu_ref[...], preferred_element_type=F32) + bglu_ref[...]
        gate = jnp.concatenate([u_scr[4 + s, rs, :] for s in range(4)], axis=1)
        y = gl[:, 0:D_SSM] * _sigmoid(gl[:, D_SSM:2 * D_SSM]) * gate
        y = _rms_scale(y) * g_ref[...]
        for s in range(4):
            ys_scr[s, rs, :] = y[:, s * LANES:(s + 1) * LANES]

    n_sub = len(sub_rows)
    for j in range(n_sub + 2):
        if j < n_sub:
            expand(sub_rows[j])
        if 1 <= j <= n_sub:
            recur(j - 1)
        if j >= 2:
            project(sub_rows[j - 2])

    if with_out_proj:
        for b in range(nb):
            rb = slice(b * tt, (b + 1) * tt)
            for s in range(4):
                mix_scr[rb, s * LANES:(s + 1) * LANES] = ys_scr[s, pl.ds(b, tt, stride=nb), :].astype(BF16)
            mix_scr[rb, D_SSM:D_SSM + D_ATTN] = (_rms_scale(ya_ref[b].astype(F32)) * ga_ref[...]).astype(BF16)
        h = x_ref[...].reshape(rows, D_MODEL) + jnp.dot(mix_scr[...], wout_ref[...], preferred_element_type=F32)
        y_ref[...] = (_rms_scale(h) * gf_ref[...]).reshape(nb, tt, D_MODEL)
        return
    for s in range(4):
        if prompt:
            for b in range(nb):
                ys_ref[s, b] = ys_scr[s, pl.ds(b, tt, stride=nb), :].astype(ys_ref.dtype)
        else:
            for t in range(tt):
                ys_ref[s, pl.ds(t, nb, stride=tt), :] = ys_scr[s, t * nb:(t + 1) * nb, :]


def _ssm(xz, h0, lam, wb, cc, d_vec, w_glu_bf, b_glu, norm_g, *, nb, t_total, tt, out_proj=None, guest=None):
    rows = nb * tt
    const = lambda shape: pl.BlockSpec(shape, lambda i: (0,) * len(shape))
    seq_block = lambda width: pl.BlockSpec((nb, tt, width), lambda i: (0, i, 0))
    args = [xz, h0, lam, wb, cc, d_vec.reshape(1, D_SSM), w_glu_bf, b_glu.reshape(1, 2 * D_SSM),
            norm_g.reshape(1, D_SSM)]
    in_specs = [None, const((nb, 2 * D_STATE)), const((SUBLANES, 2 * D_STATE)),
                const((4, 128, 1024)), const((4, 1024, 128)), const((1, D_SSM)),
                const((D_SSM, 2 * D_SSM)), const((1, 2 * D_SSM)), const((1, D_SSM))]
    scratch = [pltpu.VMEM((8, rows, LANES), F32), pltpu.VMEM((rows, 2 * D_STATE), F32),
               pltpu.VMEM((4, rows, LANES), F32)]
    if t_total > tt:
        args[0] = xz.reshape(8, nb, t_total, LANES)
        in_specs[0] = pl.BlockSpec((8, nb, tt, LANES), lambda i: (0, 0, i, 0))
        out_spec = pl.BlockSpec((4, nb, tt, LANES), lambda i: (0, 0, i, 0))
        out_shape = jax.ShapeDtypeStruct((4, nb, t_total, LANES), BF16)
    else:
        in_specs[0] = const((8, rows, LANES))
        out_spec = const((4, rows, LANES))
        out_shape = jax.ShapeDtypeStruct((4, rows, LANES), F32)
    out_specs = [out_spec, const((nb, 2 * D_STATE))]
    out_shapes = [out_shape, jax.ShapeDtypeStruct((nb, 2 * D_STATE), F32)]
    kernel_kwargs = dict(nb=nb, tt=tt)
    if out_proj is not None:
        x, ya, norm_attn_g, w_out_bf, final_g = out_proj
        args += [x, ya, norm_attn_g.reshape(1, D_ATTN), w_out_bf, final_g.reshape(1, D_MODEL)]
        in_specs += [seq_block(D_MODEL), seq_block(D_ATTN), const((1, D_ATTN)), const((D_MODEL, D_MODEL)),
                     const((1, D_MODEL))]
        out_specs[0] = seq_block(D_MODEL)
        out_shapes[0] = jax.ShapeDtypeStruct((nb, t_total, D_MODEL), F32)
        scratch.append(pltpu.VMEM((rows, D_MODEL), BF16))
    if guest is not None:
        stok, cache_kt, cache_vt, first_seq, guest_tt = guest
        tok, cache, guest_out, half = _sample_specs(first_seq, 1, guest_tt)
        args += [*stok, cache_kt, cache_vt]
        in_specs += [tok] * 4 + [cache] * 2
        out_specs.append(guest_out)
        out_shapes.append(jax.ShapeDtypeStruct((t_total // tt, SUBLANES, D_ATTN), F32))
        kernel_kwargs.update(guest_half=half, guest_tt=guest_tt)
    out, st, *guest_res = pl.pallas_call(
        functools.partial(_ssm_kernel, **kernel_kwargs),
        grid=(t_total // tt,),
        in_specs=in_specs,
        out_specs=out_specs,
        out_shape=out_shapes,
        scratch_shapes=scratch,
        compiler_params=pltpu.CompilerParams(dimension_semantics=("arbitrary",), vmem_limit_bytes=VMEM_LIMIT),
        name="ssm_prompt" if t_total > tt else "ssm_sample",
    )(*args)
    return ((out if out_proj is not None else out.reshape(4, nb * t_total, LANES)), st, *guest_res)


PART0_SIXTEENTHS = 11
GROUP_BLOCKS = 2


def _band_scores(qs, ks, biases):
    low = lax.broadcasted_iota(jnp.int32, (BAND, LANES), 1) < HEAD_DIM
    nt = (((1,), (1,)), ((), ()))
    stacked = []
    for q in qs:
        stacked.append(jnp.concatenate([jnp.where(low, q, 0.0), jnp.where(low, 0.0, q)], axis=0).astype(BF16))
    return [lax.dot_general(q, k.astype(BF16), nt, preferred_element_type=F32) + bias
            for q, k, bias in zip(stacked, ks, biases)]


def _band_softmax_pv(ss, vs):
    low = lax.broadcasted_iota(jnp.int32, (BAND, LANES), 1) < HEAD_DIM
    ms = [jnp.max(s, axis=-1, keepdims=True) for s in ss]
    ps = [jnp.exp(s - m).astype(BF16) for s, m in zip(ss, ms)]
    outs = [jnp.dot(p, jnp.concatenate([v.astype(BF16), jnp.ones(v.shape, BF16)], axis=1),
                    preferred_element_type=F32) for p, v in zip(ps, vs)]
    wide = lambda a: jnp.broadcast_to(a, (BAND, a.shape[1] if a.shape[1] > 1 else LANES))
    pick = lambda a: jnp.where(low, wide(a[0:BAND]), wide(a[BAND:2 * BAND]))
    return [(pick(o[:, 0:LANES]), pick(m), pick(o[:, LANES:2 * LANES])) for o, m in zip(outs, ms)]


def _prompt_attn_part(q_ref, k_ref, v_ref, gate_ref, o_ref, o_scr, m_scr, l_scr, band_scr, causal_scr, part):
    t_total = q_ref.shape[0]
    if part == 0:
        for scr, kwin in ((band_scr, 2 * BAND), (causal_scr, BAND)):
            qq = lax.broadcasted_iota(jnp.int32, (2 * BAND, kwin), 0) & (BAND - 1)
            kk = lax.broadcasted_iota(jnp.int32, (2 * BAND, kwin), 1)
            valid = ((kk >= qq) & (kk <= qq + BAND)) if kwin > BAND else (kk <= qq)
            scr[...] = jnp.where(valid, 0.0, NEG_INF)

    blocks = []
    for pidx, (window, dil) in enumerate(DILATED_PATTERNS):
        nblk = t_total // dil // BAND
        rows = (lambda r0, n, dil=dil: pl.ds(r0, n, stride=dil) if dil > 1 else pl.ds(r0, n))
        for rho in range(dil):
            blocks.append((pidx, rows(rho, BAND), rows(rho, BAND), causal_scr))
            blocks += [(pidx, rows(rho + dil * BAND * i, BAND), rows(rho + dil * BAND * (i - 1), 2 * BAND), band_scr)
                       for i in range(1, nblk)]

    def scores(group):
        return _band_scores([q_ref[qr, :] for _, qr, _, _ in group], [k_ref[kr, :] for _, _, kr, _ in group],
                            [bias[...] for _, _, _, bias in group])

    def finish(group, ss):
        res = _band_softmax_pv(ss, [v_ref[kr, :] for _, _, kr, _ in group])
        for (pidx, qr, _, _), (o, m, l) in zip(group, res):
            o_scr[pidx, qr, :] = o
            m_scr[pidx, qr, :] = m
            l_scr[pidx, qr, :] = l

    groups = [blocks[g:g + GROUP_BLOCKS] for g in range(0, len(blocks), GROUP_BLOCKS)]
    half = (len(groups) * PART0_SIXTEENTHS) // 16
    pending = None
    for group in (groups[:half] if part == 0 else groups[half:]):
        ss = scores(group)
        if pending is not None:
            finish(*pending)
        pending = (group, ss)
    finish(*pending)

    if part == 1:
        chunk = 256
        def combine(c, _):
            rows = pl.ds(pl.multiple_of(c * chunk, chunk), chunk)
            ms = [m_scr[p, rows, :] for p in range(3)]
            big = jnp.maximum(jnp.maximum(ms[0], ms[1]), ms[2])
            ws = [jnp.exp(m - big) for m in ms]
            num = ws[0] * o_scr[0, rows, :] + ws[1] * o_scr[1, rows, :] + ws[2] * o_scr[2, rows, :]
            den = ws[0] * l_scr[0, rows, :] + ws[1] * l_scr[1, rows, :] + ws[2] * l_scr[2, rows, :]
            o_ref[rows, :] = ((num / den) * gate_ref[rows, :].astype(F32)).astype(o_ref.dtype)
            return 0
        lax.fori_loop(0, t_total // chunk, combine, 0)


NEAR = 512


def _sample_attn_seqs(sources, tt):
    assert 2 * tt == SUBLANES
    nt = (((1,), (1,)), ((), ()))
    row8 = lax.broadcasted_iota(jnp.int32, (SUBLANES, LANES), 0)
    lane8 = lax.broadcasted_iota(jnp.int32, (SUBLANES, LANES), 1)
    top = row8 < tt
    own_head = top == (lane8 < HEAD_DIM)

    def dup(x, half):
        return jnp.where(row8 // tt == half, x, pltpu.roll(x, tt, axis=0))

    jq = lambda shape: lax.broadcasted_iota(jnp.int32, shape, 0) & (tt - 1)
    d_full = WINDOW_MAX + jq((SUBLANES, WINDOW_MAX)) - lax.broadcasted_iota(jnp.int32, (SUBLANES, WINDOW_MAX), 1)
    d_near = NEAR + jq((SUBLANES, NEAR)) - lax.broadcasted_iota(jnp.int32, (SUBLANES, NEAR), 1)
    d_new = jq((SUBLANES, SUBLANES)) - lax.broadcasted_iota(jnp.int32, (SUBLANES, SUBLANES), 1)
    new_key = lax.broadcasted_iota(jnp.int32, (SUBLANES, SUBLANES), 1) < tt
    ok_main, ok_new = [], []
    for window, dil in DILATED_PATTERNS:
        d = d_near if window <= NEAR else d_full
        ok_main.append(((d & (dil - 1)) == 0) & (d <= window))
        ok_new.append(new_key & (d_new >= 0) & ((d_new & (dil - 1)) == 0))

    insts = [(slice(pr * LANES, (pr + 1) * LANES), src) for src in sources for pr in range(D_ATTN // LANES)]

    scored = []
    for lanes, (q_ref, kn_ref, _, _, kc_ref, _, _, half) in insts:
        q = jnp.where(own_head, dup(q_ref[:, lanes], half), 0.0).astype(BF16)
        kn = dup(kn_ref[:, lanes], half).astype(BF16)
        s_main = jnp.dot(q, kc_ref[lanes, :].astype(BF16), preferred_element_type=F32)
        s_new = lax.dot_general(q, kn, nt, preferred_element_type=F32)
        scored.append((s_main, s_new))

    weighted = []
    for s_main, s_new in scored:
        stats = []
        for (window, dil), okm, okn in zip(DILATED_PATTERNS, ok_main, ok_new):
            sm = jnp.where(okm, s_main[:, WINDOW_MAX - NEAR:] if window <= NEAR else s_main, NEG_INF)
            sn = jnp.where(okn, s_new, NEG_INF)
            m = jnp.maximum(jnp.max(sm, axis=-1, keepdims=True), jnp.max(sn, axis=-1, keepdims=True))
            pm = jnp.exp(sm - m)
            pn = jnp.exp(sn - m)
            l = jnp.sum(pm, axis=-1, keepdims=True) + jnp.sum(pn, axis=-1, keepdims=True)
            stats.append((m, l, pm, pn))
        big = functools.reduce(jnp.maximum, [s[0] for s in stats])
        ws = [jnp.exp(s[0] - big) for s in stats]
        den = functools.reduce(jnp.add, [w * s[1] for w, s in zip(ws, stats)])
        p_new = functools.reduce(jnp.add, [w * s[3] for w, s in zip(ws, stats)])
        p_near = functools.reduce(jnp.add, [w * s[2] for w, s in zip(ws, stats) if s[2].shape[1] == NEAR])
        p_full = functools.reduce(jnp.add, [w * s[2] for w, s in zip(ws, stats) if s[2].shape[1] != NEAR])
        p_main = jnp.concatenate([p_full[:, 0:WINDOW_MAX - NEAR], p_full[:, WINDOW_MAX - NEAR:] + p_near], axis=1)
        weighted.append((p_main.astype(BF16), p_new.astype(BF16).astype(F32), den))

    results = []
    for (lanes, (_, _, vn_ref, gate_ref, _, vc_ref, _, half)), (p_main, p_new, den) in zip(insts, weighted):
        acc = lax.dot_general(p_main, vc_ref[lanes, :].astype(BF16), nt, preferred_element_type=F32)
        vn = dup(vn_ref[:, lanes], half).astype(BF16).astype(F32)
        for j in range(tt):
            acc = acc + p_new[:, j:j + 1] * vn[j:j + 1, :]
        acc = acc / den
        merged = jnp.where(lane8 < HEAD_DIM, acc, pltpu.roll(acc, tt, axis=0))
        results.append(merged * dup(gate_ref[:, lanes], half))

    n_pairs = D_ATTN // LANES
    for s, src in enumerate(sources):
        src[6][...] = jnp.concatenate(results[s * n_pairs:(s + 1) * n_pairs], axis=1)


def _sample_specs(first_seq, steps_per_seq, tt):
    seq = lambda i: first_seq + i // steps_per_seq
    tok = pl.BlockSpec((SUBLANES, D_ATTN), lambda i: (seq(i) * tt // SUBLANES, 0))
    cache = pl.BlockSpec((None, D_ATTN, WINDOW_MAX), lambda i: (seq(i), 0, 0))
    out = pl.BlockSpec((None, SUBLANES, D_ATTN), lambda i: (i // steps_per_seq, 0, 0))
    half = lambda i: seq(i) % (SUBLANES // tt)
    return tok, cache, out, half


def _attention_kernel(*refs, tt, halves):
    tok_a, tok_b = refs[0:4], refs[4:8]
    (kc_a, vc_a, kc_b, vc_b, pq_ref, pk_ref, pv_ref, pgate_ref,
     so_a, so_b, po_ref, o_scr, m_scr, l_scr, band_scr, causal_scr) = refs[8:]
    i = pl.program_id(0)
    src_a = (*tok_a, kc_a, vc_a, so_a, halves[0](i))
    src_b = (*tok_b, kc_b, vc_b, so_b, halves[1](i))
    for part in range(2):
        @pl.when(i % 2 == part)
        def _(part=part):
            _sample_attn_seqs([src_a, src_b] if part == 0 else [src_a], tt)
            _prompt_attn_part(pq_ref, pk_ref, pv_ref, pgate_ref, po_ref, o_scr, m_scr, l_scr,
                              band_scr, causal_scr, part)


def _attention(stok, cache_kt, cache_vt, pq, pk, pv, pgate, *, n_seqs, tt, batch, t_total):
    pairs = D_ATTN // LANES
    steps = 2 * batch * pairs
    n_a, n_b = steps, steps // 2
    assert n_seqs == n_a + n_b
    tok_a, cache_a, out_a, half_a = _sample_specs(0, 1, tt)
    tok_b, cache_b, out_b, half_b = _sample_specs(n_a, 2, tt)
    shp = (batch, t_total, D_ATTN)
    unit = pl.BlockSpec((None, t_total, LANES), lambda i: (i // (2 * pairs), 0, (i // 2) % pairs))
    so_a, so_b, po = pl.pallas_call(
        functools.partial(_attention_kernel, tt=tt, halves=(half_a, half_b)),
        grid=(steps,),
        in_specs=[tok_a] * 4 + [tok_b] * 4 + [cache_a, cache_a, cache_b, cache_b] + [unit] * 4,
        out_specs=[out_a, out_b, unit],
        out_shape=[jax.ShapeDtypeStruct((n_a, SUBLANES, D_ATTN), F32),
                   jax.ShapeDtypeStruct((n_b, SUBLANES, D_ATTN), F32), jax.ShapeDtypeStruct(shp, BF16)],
        scratch_shapes=[pltpu.VMEM((3, t_total, LANES), F32)] * 3
                       + [pltpu.VMEM((2 * BAND, 2 * BAND), F32), pltpu.VMEM((2 * BAND, BAND), F32)],
        compiler_params=pltpu.CompilerParams(dimension_semantics=("arbitrary",), vmem_limit_bytes=VMEM_LIMIT),
        name="attention",
    )(*stok, *stok, cache_kt, cache_vt, cache_kt, cache_vt,
      pq.reshape(shp), pk.reshape(shp), pv.reshape(shp), pgate.reshape(shp))
    return jnp.concatenate([so_a, so_b], axis=0), po.reshape(batch * t_total, D_ATTN)


def _out_proj_kernel(x_ref, ys_ref, ya_ref, ga_ref, w_ref, gf_ref, y_ref):
    ys = jnp.concatenate([ys_ref[s] for s in range(4)], axis=1).astype(BF16)
    ya = (_rms_scale(ya_ref[...]) * ga_ref[...]).astype(BF16)
    mix = jnp.concatenate([ys, ya], axis=1)
    h = x_ref[...] + jnp.dot(mix, w_ref[...], preferred_element_type=F32)
    y_ref[...] = _rms_scale(h) * gf_ref[...]


def _out_proj(x, ys, ya, norm_attn_g, w_out_bf, final_g):
    rows = x.shape[0]
    row_spec = lambda width: pl.BlockSpec((ROW_TILE, width), lambda i: (i, 0))
    const = lambda shape: pl.BlockSpec(shape, lambda i: (0,) * len(shape))
    return pl.pallas_call(
        _out_proj_kernel,
        grid=(rows // ROW_TILE,),
        in_specs=[row_spec(D_MODEL), pl.BlockSpec((4, ROW_TILE, LANES), lambda i: (0, i, 0)), row_spec(D_ATTN),
                  const((1, D_ATTN)), const((D_MODEL, D_MODEL)), const((1, D_MODEL))],
        out_specs=row_spec(D_MODEL),
        out_shape=jax.ShapeDtypeStruct((rows, D_MODEL), F32),
        compiler_params=pltpu.CompilerParams(dimension_semantics=("parallel",), vmem_limit_bytes=VMEM_LIMIT),
        name="out_proj",
    )(x, ys, ya, norm_attn_g.reshape(1, D_ATTN), w_out_bf, final_g.reshape(1, D_MODEL))


def kernel(x_prompt, x_sample, cache_k, cache_v, state_ssm_re, state_ssm_im, norm_in_g, w_in,
           ssm_A_re, ssm_A_im, ssm_log_dt, ssm_B_re, ssm_B_im, ssm_C_re, ssm_C_im, ssm_D,
           w_glu, b_glu, norm_ssm_g, norm_attn_g, w_out, final_norm_g):
    depth = w_in.shape[0]
    assert depth == 1
    batch, seq, _ = x_prompt.shape
    dec_batch, dec_seq, _ = x_sample.shape
    assert batch == SUBLANES and seq == WINDOW_MAX and seq % SSM_STEPS == 0

    lam, bb_re, bb_im = _ssm_prep(ssm_A_re[0], ssm_A_im[0], ssm_log_dt[0], ssm_B_re[0], ssm_B_im[0])
    wb, cc = _block_diag_weights(bb_re, bb_im, ssm_C_re[0], ssm_C_im[0])
    w_in_bf = w_in[0].astype(BF16)
    w_glu_bf = w_glu[0].astype(BF16)
    w_out_bf = w_out[0].astype(BF16)

    ssm = functools.partial(_ssm, lam=lam, wb=wb, cc=cc, d_vec=ssm_D[0], w_glu_bf=w_glu_bf, b_glu=b_glu[0],
                            norm_g=norm_ssm_g[0])

    xp = x_prompt.reshape(batch * seq, D_MODEL)
    xzp, qp, kp, vp, gp, kpt, vpt = _in_proj(xp, norm_in_g[0], w_in_bf, seq_len=seq)
    xs = x_sample.reshape(dec_batch * dec_seq, D_MODEL)
    xzs, qs, ksm, vsm, gs = _in_proj(xs, norm_in_g[0], w_in_bf)

    chan_major = lambda c: c[0].transpose(0, 2, 3, 1).reshape(dec_batch, D_ATTN, WINDOW_MAX)
    cache_kt, cache_vt = chan_major(cache_k), chan_major(cache_v)
    stok = (qs, ksm, vsm, gs)
    n_guest = seq // SSM_STEPS
    yas_main, yap = _attention(stok, cache_kt, cache_vt, qp, kp, vp, gp,
                               n_seqs=dec_batch - n_guest, tt=dec_seq, batch=batch, t_total=seq)

    yp, stp, yas_guest = ssm(xzp, jnp.zeros((batch, 2 * D_STATE), F32), nb=batch, t_total=seq, tt=SSM_STEPS,
                             out_proj=(x_prompt, yap.reshape(batch, seq, D_ATTN), norm_attn_g[0], w_out_bf,
                                       final_norm_g),
                             guest=(stok, cache_kt, cache_vt, dec_batch - n_guest, dec_seq))
    yas = jnp.concatenate([yas_main, yas_guest], axis=0)[:, 0:dec_seq].reshape(dec_batch * dec_seq, D_ATTN)
    h0 = jnp.concatenate([state_ssm_re[0].reshape(dec_batch, D_STATE),
                          state_ssm_im[0].reshape(dec_batch, D_STATE)], axis=1)
    yss, sts = ssm(xzs, h0, nb=dec_batch, t_total=dec_seq, tt=dec_seq)
    ysm = _out_proj(xs, yss, yas, norm_attn_g[0], w_out_bf, final_norm_g)

    heads = lambda a, n, t: a.reshape(1, n, t, N_HEADS, HEAD_DIM)
    heads_t = lambda a, n, t: a.reshape(n, N_HEADS, HEAD_DIM, t).transpose(0, 3, 1, 2)[None]
    state = lambda s, n, part: s[:, part * D_STATE:(part + 1) * D_STATE].reshape(1, n, N_SSM_GROUPS, SSM_STATE)
    return (yp.reshape(batch, seq, D_MODEL), ysm.reshape(dec_batch, dec_seq, D_MODEL),
            heads_t(kpt, batch, seq), heads_t(vpt, batch, seq), state(stp, batch, 0), state(stp, batch, 1),
            heads(ksm, dec_batch, dec_seq), heads(vsm, dec_batch, dec_seq),
            state(sts, dec_batch, 0), state(sts, dec_batch, 1))
```

```python
import functools

import jax
import jax.numpy as jnp
from jax import lax
from jax.experimental import pallas as pl
from jax.experimental.pallas import tpu as pltpu

F32 = jnp.float32
BF16 = jnp.bfloat16

D_MODEL = 1024
D_SSM = 512
D_ATTN = 512
SSM_GROUP = 16
N_SSM_GROUPS = D_SSM // SSM_GROUP
SSM_STATE = 64
D_STATE = N_SSM_GROUPS * SSM_STATE
HEAD_DIM = 64
N_HEADS = D_ATTN // HEAD_DIM
D_IN_PROJ = 2 * D_SSM + 4 * D_ATTN
RMS_EPS = 1e-6
DILATED_PATTERNS = ((128, 1), (512, 4), (2048, 16))
BAND = 128
WINDOW_MAX = 2048
LANES = 128
SUBLANES = 8
VMEM_LIMIT = 56 * 1024 * 1024
ROW_TILE = 512
IN_PROJ_ROW_TILE = 1024
SSM_STEPS = 64
SSM_SUB_STEPS = 32
NEG_INF = float("-inf")


def _sigmoid(x):
    return 1.0 / (1.0 + jnp.exp(-x))


def _rms_scale(x):
    return x * lax.rsqrt(jnp.mean(x * x, axis=-1, keepdims=True) + RMS_EPS)


def _ssm_prep_kernel(are_ref, aim_ref, ldt_ref, bre_ref, bim_ref, lam_ref, bbre_ref, bbim_ref):
    a_re = are_ref[...]
    a_im = aim_ref[...]
    dt = jnp.exp(ldt_ref[...])
    mag = jnp.exp(a_re * dt)
    ang = a_im * dt
    lam_re = mag * jnp.cos(ang)
    lam_im = mag * jnp.sin(ang)
    den = a_re * a_re + a_im * a_im
    f_re = ((lam_re - 1.0) * a_re + lam_im * a_im) / den
    f_im = (lam_im * a_re - (lam_re - 1.0) * a_im) / den
    lam_ref[:, 0:D_STATE] = jnp.broadcast_to(lam_re, (SUBLANES, D_STATE))
    lam_ref[:, D_STATE:2 * D_STATE] = jnp.broadcast_to(lam_im, (SUBLANES, D_STATE))
    b_re = bre_ref[...]
    b_im = bim_ref[...]
    bbre_ref[...] = f_re * b_re - f_im * b_im
    bbim_ref[...] = f_re * b_im + f_im * b_re


def _ssm_prep(a_re, a_im, log_dt, b_re, b_im):
    flat = lambda a: a.reshape(1, D_STATE)
    ldt = jnp.repeat(log_dt, SSM_STATE).reshape(1, D_STATE)
    bt = lambda b: b.transpose(2, 0, 1).reshape(SSM_GROUP, D_STATE)
    return pl.pallas_call(
        _ssm_prep_kernel,
        out_shape=(jax.ShapeDtypeStruct((SUBLANES, 2 * D_STATE), F32),
                   jax.ShapeDtypeStruct((SSM_GROUP, D_STATE), F32),
                   jax.ShapeDtypeStruct((SSM_GROUP, D_STATE), F32)),
        name="ssm_prep",
    )(flat(a_re), flat(a_im), ldt, bt(b_re), bt(b_im))


def _block_diag_weights(bb_re, bb_im, c_re, c_im):
    eye = jnp.eye(8, dtype=F32)
    def place_b(bb):
        b4 = bb.reshape(SSM_GROUP, 4, 8, SSM_STATE).transpose(1, 2, 0, 3)
        return jnp.einsum("kgcp,gh->kgchp", b4, eye).reshape(4, 128, 512)
    def place_c(c):
        c4 = c.reshape(4, 8, SSM_GROUP, SSM_STATE)
        return jnp.einsum("kgcp,gh->khpgc", c4, eye).reshape(4, 512, 128)
    wb = jnp.concatenate([place_b(bb_re), place_b(bb_im)], axis=2).astype(BF16)
    cc = jnp.concatenate([place_c(c_re), -place_c(c_im)], axis=1).astype(BF16)
    return wb, cc


def _in_proj_kernel(x_ref, g_ref, w_ref, xz_ref, q_ref, k_ref, v_ref, za_ref, *t_refs):
    xn = (_rms_scale(x_ref[...]) * g_ref[...]).astype(BF16)
    outs = (None, None, q_ref, k_ref, v_ref, za_ref)
    for j in range(D_IN_PROJ // 512):
        r = jnp.dot(xn, w_ref[:, j * 512:(j + 1) * 512], preferred_element_type=F32)
        if j in (1, 5):
            r = r * _sigmoid(r)
        if j == 2:
            r = r * (HEAD_DIM ** -0.5)
        if j < 2:
            for s in range(4):
                xz_ref[j * 4 + s] = r[:, s * LANES:(s + 1) * LANES]
        else:
            outs[j][...] = r.astype(outs[j].dtype)
            if t_refs and j in (3, 4):
                t_refs[j - 3][...] = r.T


def _in_proj(x, norm_g, w_in_bf, *, seq_len=None):
    rows = x.shape[0]
    tile = min(rows, IN_PROJ_ROW_TILE)
    row_spec = lambda width: pl.BlockSpec((tile, width), lambda i: (i, 0))
    out_specs = [pl.BlockSpec((8, tile, LANES), lambda i: (0, i, 0))] + [row_spec(512)] * 4
    out_shape = ([jax.ShapeDtypeStruct((8, rows, LANES), F32)] + [jax.ShapeDtypeStruct((rows, 512), F32)] * 3
                 + [jax.ShapeDtypeStruct((rows, 512), F32 if seq_len is None else BF16)])
    if seq_len is not None:
        per_seq = seq_len // tile
        out_specs += [pl.BlockSpec((None, 512, tile), lambda i: (i // per_seq, 0, i % per_seq))] * 2
        out_shape += [jax.ShapeDtypeStruct((rows // seq_len, 512, seq_len), F32)] * 2
    return pl.pallas_call(
        _in_proj_kernel,
        grid=(rows // tile,),
        in_specs=[row_spec(D_MODEL),
                  pl.BlockSpec((1, D_MODEL), lambda i: (0, 0)),
                  pl.BlockSpec((D_MODEL, D_IN_PROJ), lambda i: (0, 0))],
        out_specs=out_specs,
        out_shape=out_shape,
        compiler_params=pltpu.CompilerParams(dimension_semantics=("parallel",), vmem_limit_bytes=VMEM_LIMIT),
        name="in_proj",
    )(x, norm_g.reshape(1, D_MODEL), w_in_bf)


def _ssm_kernel(xz_ref, h0_ref, lam_ref, wb_ref, cc_ref, d_ref, wglu_ref, bglu_ref, g_ref, *rest, nb, tt,
                guest_half=None, guest_tt=None):
    rows = nb * tt
    prompt = xz_ref.ndim == 4
    with_out_proj = len(rest) > 5
    if guest_half is not None:
        (x_ref, ya_ref, ga_ref, wout_ref, gf_ref, *guest_in, y_ref, st_ref, guest_out,
         u_scr, bu_scr, ys_scr, mix_scr) = rest
        _sample_attn_seqs([(*guest_in, guest_out, guest_half(pl.program_id(0)))], guest_tt)
    elif with_out_proj:
        x_ref, ya_ref, ga_ref, wout_ref, gf_ref, y_ref, st_ref, u_scr, bu_scr, ys_scr, mix_scr = rest
    else:
        ys_ref, st_ref, u_scr, bu_scr, ys_scr = rest

    @pl.when(pl.program_id(0) == 0)
    def _():
        st_ref[...] = h0_ref[...]

    for s in range(8):
        if prompt:
            for b in range(nb):
                u_scr[s, pl.ds(b, tt, stride=nb), :] = xz_ref[s, b]
        else:
            for t in range(tt):
                u_scr[s, t * nb:(t + 1) * nb, :] = xz_ref[s, pl.ds(t, nb, stride=tt), :]

    sub = min(tt, SSM_SUB_STEPS)
    sub_rows = [slice(j * sub * nb, (j + 1) * sub * nb) for j in range(tt // sub)]

    def expand(rs):
        for kt in range(4):
            r = jnp.dot(u_scr[kt, rs, :].astype(BF16), wb_ref[kt], preferred_element_type=F32)
            bu_scr[rs, kt * 512:(kt + 1) * 512] = r[:, 0:512]
            bu_scr[rs, D_STATE + kt * 512:D_STATE + (kt + 1) * 512] = r[:, 512:1024]

    width = (SUBLANES * LANES * 8) // nb if nb <= 64 else LANES
    chunks = [(slice(c * width, (c + 1) * width), slice(D_STATE + c * width, D_STATE + (c + 1) * width))
              for c in range(D_STATE // width)]

    def recur(j):
        for re_l, im_l in chunks:
            if nb == SUBLANES:
                lam_re, lam_im = lam_ref[:, re_l], lam_ref[:, im_l]
            else:
                lam_re = jnp.broadcast_to(lam_ref[0:1, re_l], (nb, width))
                lam_im = jnp.broadcast_to(lam_ref[0:1, im_l], (nb, width))
            s_re, s_im = st_ref[:, re_l], st_ref[:, im_l]
            for t in range(j * sub, (j + 1) * sub):
                tr = slice(t * nb, (t + 1) * nb)
                s_re, s_im = (lam_re * s_re - lam_im * s_im + bu_scr[tr, re_l],
                              lam_re * s_im + lam_im * s_re + bu_scr[tr, im_l])
                bu_scr[tr, re_l] = s_re
                bu_scr[tr, im_l] = s_im
            st_ref[:, re_l] = s_re
            st_ref[:, im_l] = s_im

    def project(rs):
        ys = []
        for kt in range(4):
            xc = jnp.concatenate([bu_scr[rs, kt * 512:(kt + 1) * 512],
                                  bu_scr[rs, D_STATE + kt * 512:D_STATE + (kt + 1) * 512]], axis=1).astype(BF16)
            y = jnp.dot(xc, cc_ref[kt], preferred_element_type=F32)
            ys.append(y + d_ref[:, kt * LANES:(kt + 1) * LANES] * u_scr[kt, rs, :])
        y = jnp.concatenate(ys, axis=1)
        g = jax.nn.gelu(y)
        gl = jnp.dot(g.astype(BF16), wglu_ref[...], preferred_element_type=F32) + bglu_ref[...]
        gate = jnp.concatenate([u_scr[4 + s, rs, :] for s in range(4)], axis=1)
        y = gl[:, 0:D_SSM] * _sigmoid(gl[:, D_SSM:2 * D_SSM]) * gate
        y = _rms_scale(y) * g_ref[...]
        for s in range(4):
            ys_scr[s, rs, :] = y[:, s * LANES:(s + 1) * LANES]

    n_sub = len(sub_rows)
    for j in range(n_sub + 2):
        if j < n_sub:
            expand(sub_rows[j])
        if 1 <= j <= n_sub:
            recur(j - 1)
        if j >= 2:
            project(sub_rows[j - 2])

    if with_out_proj:
        for b in range(nb):
            rb = slice(b * tt, (b + 1) * tt)
            for s in range(4):
                mix_scr[rb, s * LANES:(s + 1) * LANES] = ys_scr[s, pl.ds(b, tt, stride=nb), :].astype(BF16)
            mix_scr[rb, D_SSM:D_SSM + D_ATTN] = (_rms_scale(ya_ref[b].astype(F32)) * ga_ref[...]).astype(BF16)
        h = x_ref[...].reshape(rows, D_MODEL) + jnp.dot(mix_scr[...], wout_ref[...], preferred_element_type=F32)
        y_ref[...] = (_rms_scale(h) * gf_ref[...]).reshape(nb, tt, D_MODEL)
        return
    for s in range(4):
        if prompt:
            for b in range(nb):
                ys_ref[s, b] = ys_scr[s, pl.ds(b, tt, stride=nb), :].astype(ys_ref.dtype)
        else:
            for t in range(tt):
                ys_ref[s, pl.ds(t, nb, stride=tt), :] = ys_scr[s, t * nb:(t + 1) * nb, :]


def _ssm(xz, h0, lam, wb, cc, d_vec, w_glu_bf, b_glu, norm_g, *, nb, t_total, tt, out_proj=None, guest=None):
    rows = nb * tt
    const = lambda shape: pl.BlockSpec(shape, lambda i: (0,) * len(shape))
    seq_block = lambda width: pl.BlockSpec((nb, tt, width), lambda i: (0, i, 0))
    args = [xz, h0, lam, wb, cc, d_vec.reshape(1, D_SSM), w_glu_bf, b_glu.reshape(1, 2 * D_SSM),
            norm_g.reshape(1, D_SSM)]
    in_specs = [None, const((nb, 2 * D_STATE)), const((SUBLANES, 2 * D_STATE)),
                const((4, 128, 1024)), const((4, 1024, 128)), const((1, D_SSM)),
                const((D_SSM, 2 * D_SSM)), const((1, 2 * D_SSM)), const((1, D_SSM))]
    scratch = [pltpu.VMEM((8, rows, LANES), F32), pltpu.VMEM((rows, 2 * D_STATE), F32),
               pltpu.VMEM((4, rows, LANES), F32)]
    if t_total > tt:
        args[0] = xz.reshape(8, nb, t_total, LANES)
        in_specs[0] = pl.BlockSpec((8, nb, tt, LANES), lambda i: (0, 0, i, 0))
        out_spec = pl.BlockSpec((4, nb, tt, LANES), lambda i: (0, 0, i, 0))
        out_shape = jax.ShapeDtypeStruct((4, nb, t_total, LANES), BF16)
    else:
        in_specs[0] = const((8, rows, LANES))
        out_spec = const((4, rows, LANES))
        out_shape = jax.ShapeDtypeStruct((4, rows, LANES), F32)
    out_specs = [out_spec, const((nb, 2 * D_STATE))]
    out_shapes = [out_shape, jax.ShapeDtypeStruct((nb, 2 * D_STATE), F32)]
    kernel_kwargs = dict(nb=nb, tt=tt)
    if out_proj is not None:
        x, ya, norm_attn_g, w_out_bf, final_g = out_proj
        args += [x, ya, norm_attn_g.reshape(1, D_ATTN), w_out_bf, final_g.reshape(1, D_MODEL)]
        in_specs += [seq_block(D_MODEL), seq_block(D_ATTN), const((1, D_ATTN)), const((D_MODEL, D_MODEL)),
                     const((1, D_MODEL))]
        out_specs[0] = seq_block(D_MODEL)
        out_shapes[0] = jax.ShapeDtypeStruct((nb, t_total, D_MODEL), F32)
        scratch.append(pltpu.VMEM((rows, D_MODEL), BF16))
    if guest is not None:
        stok, cache_kt, cache_vt, first_seq, guest_tt = guest
        tok, cache, guest_out, half = _sample_specs(first_seq, 1, 0, guest_tt)
        args += [*stok, cache_kt, cache_vt]
        in_specs += [tok] * 4 + [cache] * 2
        out_specs.append(guest_out)
        out_shapes.append(jax.ShapeDtypeStruct((t_total // tt, SUBLANES, D_ATTN), F32))
        kernel_kwargs.update(guest_half=half, guest_tt=guest_tt)
    out, st, *guest_res = pl.pallas_call(
        functools.partial(_ssm_kernel, **kernel_kwargs),
        grid=(t_total // tt,),
        in_specs=in_specs,
        out_specs=out_specs,
        out_shape=out_shapes,
        scratch_shapes=scratch,
        compiler_params=pltpu.CompilerParams(dimension_semantics=("arbitrary",), vmem_limit_bytes=VMEM_LIMIT),
        name="ssm_prompt" if t_total > tt else "ssm_sample",
    )(*args)
    return ((out if out_proj is not None else out.reshape(4, nb * t_total, LANES)), st, *guest_res)


PART0_SIXTEENTHS = 14
GROUP_BLOCKS = 2


def _band_scores(qs, ks, biases):
    low = lax.broadcasted_iota(jnp.int32, (BAND, LANES), 1) < HEAD_DIM
    nt = (((1,), (1,)), ((), ()))
    stacked = []
    for q in qs:
        stacked.append(jnp.concatenate([jnp.where(low, q, 0.0), jnp.where(low, 0.0, q)], axis=0).astype(BF16))
    return [lax.dot_general(q, k.astype(BF16), nt, preferred_element_type=F32) + bias
            for q, k, bias in zip(stacked, ks, biases)]


def _band_softmax_pv(ss, vs):
    low = lax.broadcasted_iota(jnp.int32, (BAND, LANES), 1) < HEAD_DIM
    ms = [jnp.max(s, axis=-1, keepdims=True) for s in ss]
    ps = [jnp.exp(s - m).astype(BF16) for s, m in zip(ss, ms)]
    outs = [jnp.dot(p, jnp.concatenate([v.astype(BF16), jnp.ones(v.shape, BF16)], axis=1),
                    preferred_element_type=F32) for p, v in zip(ps, vs)]
    wide = lambda a: jnp.broadcast_to(a, (BAND, a.shape[1] if a.shape[1] > 1 else LANES))
    pick = lambda a: jnp.where(low, wide(a[0:BAND]), wide(a[BAND:2 * BAND]))
    return [(pick(o[:, 0:LANES]), pick(m), pick(o[:, LANES:2 * LANES])) for o, m in zip(outs, ms)]


def _prompt_attn_part(q_ref, k_ref, v_ref, gate_ref, o_ref, o_scr, m_scr, l_scr, band_scr, causal_scr, part):
    t_total = q_ref.shape[0]
    if part == 0:
        for scr, kwin in ((band_scr, 2 * BAND), (causal_scr, BAND)):
            qq = lax.broadcasted_iota(jnp.int32, (2 * BAND, kwin), 0) & (BAND - 1)
            kk = lax.broadcasted_iota(jnp.int32, (2 * BAND, kwin), 1)
            valid = ((kk >= qq) & (kk <= qq + BAND)) if kwin > BAND else (kk <= qq)
            scr[...] = jnp.where(valid, 0.0, NEG_INF)

    blocks = []
    for pidx, (window, dil) in enumerate(DILATED_PATTERNS):
        nblk = t_total // dil // BAND
        rows = (lambda r0, n, dil=dil: pl.ds(r0, n, stride=dil) if dil > 1 else pl.ds(r0, n))
        for rho in range(dil):
            blocks.append((pidx, rows(rho, BAND), rows(rho, BAND), causal_scr))
            blocks += [(pidx, rows(rho + dil * BAND * i, BAND), rows(rho + dil * BAND * (i - 1), 2 * BAND), band_scr)
                       for i in range(1, nblk)]

    def scores(group):
        return _band_scores([q_ref[qr, :] for _, qr, _, _ in group], [k_ref[kr, :] for _, _, kr, _ in group],
                            [bias[...] for _, _, _, bias in group])

    def finish(group, ss):
        res = _band_softmax_pv(ss, [v_ref[kr, :] for _, _, kr, _ in group])
        for (pidx, qr, _, _), (o, m, l) in zip(group, res):
            o_scr[pidx, qr, :] = o
            m_scr[pidx, qr, :] = m
            l_scr[pidx, qr, :] = l

    groups = [blocks[g:g + GROUP_BLOCKS] for g in range(0, len(blocks), GROUP_BLOCKS)]
    half = (len(groups) * PART0_SIXTEENTHS) // 16
    pending = None
    for group in (groups[:half] if part == 0 else groups[half:]):
        ss = scores(group)
        if pending is not None:
            finish(*pending)
        pending = (group, ss)
    finish(*pending)

    if part == 1:
        chunk = 256
        def combine(c, _):
            rows = pl.ds(pl.multiple_of(c * chunk, chunk), chunk)
            ms = [m_scr[p, rows, :] for p in range(3)]
            big = jnp.maximum(jnp.maximum(ms[0], ms[1]), ms[2])
            ws = [jnp.exp(m - big) for m in ms]
            num = ws[0] * o_scr[0, rows, :] + ws[1] * o_scr[1, rows, :] + ws[2] * o_scr[2, rows, :]
            den = ws[0] * l_scr[0, rows, :] + ws[1] * l_scr[1, rows, :] + ws[2] * l_scr[2, rows, :]
            o_ref[rows, :] = ((num / den) * gate_ref[rows, :].astype(F32)).astype(o_ref.dtype)
            return 0
        lax.fori_loop(0, t_total // chunk, combine, 0)


NEAR = 512


def _sample_attn_seqs(sources, tt):
    assert 2 * tt == SUBLANES
    nt = (((1,), (1,)), ((), ()))
    row8 = lax.broadcasted_iota(jnp.int32, (SUBLANES, LANES), 0)
    lane8 = lax.broadcasted_iota(jnp.int32, (SUBLANES, LANES), 1)
    top = row8 < tt
    own_head = top == (lane8 < HEAD_DIM)

    def dup(x, half):
        return jnp.where(row8 // tt == half, x, pltpu.roll(x, tt, axis=0))

    jq = lambda shape: lax.broadcasted_iota(jnp.int32, shape, 0) & (tt - 1)
    d_full = WINDOW_MAX + jq((SUBLANES, WINDOW_MAX)) - lax.broadcasted_iota(jnp.int32, (SUBLANES, WINDOW_MAX), 1)
    d_near = NEAR + jq((SUBLANES, NEAR)) - lax.broadcasted_iota(jnp.int32, (SUBLANES, NEAR), 1)
    d_new = jq((SUBLANES, SUBLANES)) - lax.broadcasted_iota(jnp.int32, (SUBLANES, SUBLANES), 1)
    new_key = lax.broadcasted_iota(jnp.int32, (SUBLANES, SUBLANES), 1) < tt
    ok_main, ok_new = [], []
    for window, dil in DILATED_PATTERNS:
        d = d_near if window <= NEAR else d_full
        ok_main.append(((d & (dil - 1)) == 0) & (d <= window))
        ok_new.append(new_key & (d_new >= 0) & ((d_new & (dil - 1)) == 0))

    insts = [(slice(pr * LANES, (pr + 1) * LANES), src) for src in sources for pr in range(D_ATTN // LANES)]

    scored = []
    for lanes, (q_ref, kn_ref, _, _, kc_ref, _, _, half) in insts:
        q = jnp.where(own_head, dup(q_ref[:, lanes], half), 0.0).astype(BF16)
        kn = dup(kn_ref[:, lanes], half).astype(BF16)
        s_main = jnp.dot(q, kc_ref[lanes, :].astype(BF16), preferred_element_type=F32)
        s_new = lax.dot_general(q, kn, nt, preferred_element_type=F32)
        scored.append((s_main, s_new))

    weighted = []
    for s_main, s_new in scored:
        stats = []
        for (window, dil), okm, okn in zip(DILATED_PATTERNS, ok_main, ok_new):
            sm = jnp.where(okm, s_main[:, WINDOW_MAX - NEAR:] if window <= NEAR else s_main, NEG_INF)
            sn = jnp.where(okn, s_new, NEG_INF)
            m = jnp.maximum(jnp.max(sm, axis=-1, keepdims=True), jnp.max(sn, axis=-1, keepdims=True))
            pm = jnp.exp(sm - m)
            pn = jnp.exp(sn - m)
            l = jnp.sum(pm, axis=-1, keepdims=True) + jnp.sum(pn, axis=-1, keepdims=True)
            stats.append((m, l, pm, pn))
        big = functools.reduce(jnp.maximum, [s[0] for s in stats])
        ws = [jnp.exp(s[0] - big) for s in stats]
        den = functools.reduce(jnp.add, [w * s[1] for w, s in zip(ws, stats)])
        p_new = functools.reduce(jnp.add, [w * s[3] for w, s in zip(ws, stats)])
        p_near = functools.reduce(jnp.add, [w * s[2] for w, s in zip(ws, stats) if s[2].shape[1] == NEAR])
        p_full = functools.reduce(jnp.add, [w * s[2] for w, s in zip(ws, stats) if s[2].shape[1] != NEAR])
        p_main = jnp.concatenate([p_full[:, 0:WINDOW_MAX - NEAR], p_full[:, WINDOW_MAX - NEAR:] + p_near], axis=1)
        weighted.append((p_main.astype(BF16), p_new.astype(BF16).astype(F32), den))

    results = []
    for (lanes, (_, _, vn_ref, gate_ref, _, vc_ref, _, half)), (p_main, p_new, den) in zip(insts, weighted):
        acc = lax.dot_general(p_main, vc_ref[lanes, :].astype(BF16), nt, preferred_element_type=F32)
        vn = dup(vn_ref[:, lanes], half).astype(BF16).astype(F32)
        for j in range(tt):
            acc = acc + p_new[:, j:j + 1] * vn[j:j + 1, :]
        acc = acc / den
        merged = jnp.where(lane8 < HEAD_DIM, acc, pltpu.roll(acc, tt, axis=0))
        results.append(merged * dup(gate_ref[:, lanes], half))

    n_pairs = D_ATTN // LANES
    for s, src in enumerate(sources):
        src[6][...] = jnp.concatenate(results[s * n_pairs:(s + 1) * n_pairs], axis=1)


def _sample_specs(first_seq, steps_per_seq, lag, tt):
    local = lambda i: jnp.maximum(i - lag, 0) // steps_per_seq
    seq = lambda i: first_seq + local(i)
    tok = pl.BlockSpec((SUBLANES, D_ATTN), lambda i: (seq(i) * tt // SUBLANES, 0))
    cache = pl.BlockSpec((None, D_ATTN, WINDOW_MAX), lambda i: (seq(i), 0, 0))
    out = pl.BlockSpec((None, SUBLANES, D_ATTN), lambda i: (local(i), 0, 0))
    half = lambda i: seq(i) % (SUBLANES // tt)
    return tok, cache, out, half


def _attention_kernel(*refs, tt, halves):
    tok_a, tok_b = refs[0:4], refs[4:8]
    (kc_a, vc_a, kc_b, vc_b, pq_ref, pk_ref, pv_ref, pgate_ref,
     so_a, so_b, po_ref, o_scr, m_scr, l_scr, band_scr, causal_scr) = refs[8:]
    i = pl.program_id(0)
    src_a = (*tok_a, kc_a, vc_a, so_a, halves[0](i))
    src_b = (*tok_b, kc_b, vc_b, so_b, halves[1](i))
    for part in range(2):
        @pl.when(i % 2 == part)
        def _(part=part):
            _sample_attn_seqs([src_a, src_b] if part == 1 else [src_a], tt)
            _prompt_attn_part(pq_ref, pk_ref, pv_ref, pgate_ref, po_ref, o_scr, m_scr, l_scr,
                              band_scr, causal_scr, part)


def _attention(stok, cache_kt, cache_vt, pq, pk, pv, pgate, *, n_seqs, tt, batch, t_total):
    pairs = D_ATTN // LANES
    steps = 2 * batch * pairs
    n_a, n_b = steps, steps // 2
    assert n_seqs == n_a + n_b
    tok_a, cache_a, out_a, half_a = _sample_specs(0, 1, 0, tt)
    tok_b, cache_b, out_b, half_b = _sample_specs(n_a, 2, 1, tt)
    shp = (batch, t_total, D_ATTN)
    unit = pl.BlockSpec((None, t_total, LANES), lambda i: (i // (2 * pairs), 0, (i // 2) % pairs))
    so_a, so_b, po = pl.pallas_call(
        functools.partial(_attention_kernel, tt=tt, halves=(half_a, half_b)),
        grid=(steps,),
        in_specs=[tok_a] * 4 + [tok_b] * 4 + [cache_a, cache_a, cache_b, cache_b] + [unit] * 4,
        out_specs=[out_a, out_b, unit],
        out_shape=[jax.ShapeDtypeStruct((n_a, SUBLANES, D_ATTN), F32),
                   jax.ShapeDtypeStruct((n_b, SUBLANES, D_ATTN), F32), jax.ShapeDtypeStruct(shp, BF16)],
        scratch_shapes=[pltpu.VMEM((3, t_total, LANES), F32)] * 3
                       + [pltpu.VMEM((2 * BAND, 2 * BAND), F32), pltpu.VMEM((2 * BAND, BAND), F32)],
        compiler_params=pltpu.CompilerParams(dimension_semantics=("arbitrary",), vmem_limit_bytes=VMEM_LIMIT),
        name="attention",
    )(*stok, *stok, cache_kt, cache_vt, cache_kt, cache_vt,
      pq.reshape(shp), pk.reshape(shp), pv.reshape(shp), pgate.reshape(shp))
    return jnp.concatenate([so_a, so_b], axis=0), po.reshape(batch * t_total, D_ATTN)


def _out_proj_kernel(x_ref, ys_ref, ya_ref, ga_ref, w_ref, gf_ref, y_ref):
    ys = jnp.concatenate([ys_ref[s] for s in range(4)], axis=1).astype(BF16)
    ya = (_rms_scale(ya_ref[...]) * ga_ref[...]).astype(BF16)
    mix = jnp.concatenate([ys, ya], axis=1)
    h = x_ref[...] + jnp.dot(mix, w_ref[...], preferred_element_type=F32)
    y_ref[...] = _rms_scale(h) * gf_ref[...]


def _out_proj(x, ys, ya, norm_attn_g, w_out_bf, final_g):
    rows = x.shape[0]
    row_spec = lambda width: pl.BlockSpec((ROW_TILE, width), lambda i: (i, 0))
    const = lambda shape: pl.BlockSpec(shape, lambda i: (0,) * len(shape))
    return pl.pallas_call(
        _out_proj_kernel,
        grid=(rows // ROW_TILE,),
        in_specs=[row_spec(D_MODEL), pl.BlockSpec((4, ROW_TILE, LANES), lambda i: (0, i, 0)), row_spec(D_ATTN),
                  const((1, D_ATTN)), const((D_MODEL, D_MODEL)), const((1, D_MODEL))],
        out_specs=row_spec(D_MODEL),
        out_shape=jax.ShapeDtypeStruct((rows, D_MODEL), F32),
        compiler_params=pltpu.CompilerParams(dimension_semantics=("parallel",), vmem_limit_bytes=VMEM_LIMIT),
        name="out_proj",
    )(x, ys, ya, norm_attn_g.reshape(1, D_ATTN), w_out_bf, final_g.reshape(1, D_MODEL))


def kernel(x_prompt, x_sample, cache_k, cache_v, state_ssm_re, state_ssm_im, norm_in_g, w_in,
           ssm_A_re, ssm_A_im, ssm_log_dt, ssm_B_re, ssm_B_im, ssm_C_re, ssm_C_im, ssm_D,
           w_glu, b_glu, norm_ssm_g, norm_attn_g, w_out, final_norm_g):
    depth = w_in.shape[0]
    assert depth == 1
    batch, seq, _ = x_prompt.shape
    dec_batch, dec_seq, _ = x_sample.shape
    assert batch == SUBLANES and seq == WINDOW_MAX and seq % SSM_STEPS == 0

    lam, bb_re, bb_im = _ssm_prep(ssm_A_re[0], ssm_A_im[0], ssm_log_dt[0], ssm_B_re[0], ssm_B_im[0])
    wb, cc = _block_diag_weights(bb_re, bb_im, ssm_C_re[0], ssm_C_im[0])
    w_in_bf = w_in[0].astype(BF16)
    w_glu_bf = w_glu[0].astype(BF16)
    w_out_bf = w_out[0].astype(BF16)

    ssm = functools.partial(_ssm, lam=lam, wb=wb, cc=cc, d_vec=ssm_D[0], w_glu_bf=w_glu_bf, b_glu=b_glu[0],
                            norm_g=norm_ssm_g[0])

    xp = x_prompt.reshape(batch * seq, D_MODEL)
    xzp, qp, kp, vp, gp, kpt, vpt = _in_proj(xp, norm_in_g[0], w_in_bf, seq_len=seq)
    xs = x_sample.reshape(dec_batch * dec_seq, D_MODEL)
    xzs, qs, ksm, vsm, gs = _in_proj(xs, norm_in_g[0], w_in_bf)

    chan_major = lambda c: c[0].transpose(0, 2, 3, 1).reshape(dec_batch, D_ATTN, WINDOW_MAX)
    cache_kt, cache_vt = chan_major(cache_k), chan_major(cache_v)
    stok = (qs, ksm, vsm, gs)
    n_guest = seq // SSM_STEPS
    yas_main, yap = _attention(stok, cache_kt, cache_vt, qp, kp, vp, gp,
                               n_seqs=dec_batch - n_guest, tt=dec_seq, batch=batch, t_total=seq)

    yp, stp, yas_guest = ssm(xzp, jnp.zeros((batch, 2 * D_STATE), F32), nb=batch, t_total=seq, tt=SSM_STEPS,
                             out_proj=(x_prompt, yap.reshape(batch, seq, D_ATTN), norm_attn_g[0], w_out_bf,
                                       final_norm_g),
                             guest=(stok, cache_kt, cache_vt, dec_batch - n_guest, dec_seq))
    yas = jnp.concatenate([yas_main, yas_guest], axis=0)[:, 0:dec_seq].reshape(dec_batch * dec_seq, D_ATTN)
    h0 = jnp.concatenate([state_ssm_re[0].reshape(dec_batch, D_STATE),
                          state_ssm_im[0].reshape(dec_batch, D_STATE)], axis=1)
    yss, sts = ssm(xzs, h0, nb=dec_batch, t_total=dec_seq, tt=dec_seq)
    ysm = _out_proj(xs, yss, yas, norm_attn_g[0], w_out_bf, final_norm_g)

    heads = lambda a, n, t: a.reshape(1, n, t, N_HEADS, HEAD_DIM)
    heads_t = lambda a, n, t: a.reshape(n, N_HEADS, HEAD_DIM, t).transpose(0, 3, 1, 2)[None]
    state = lambda s, n, part: s[:, part * D_STATE:(part + 1) * D_STATE].reshape(1, n, N_SSM_GROUPS, SSM_STATE)
    return (yp.reshape(batch, seq, D_MODEL), ysm.reshape(dec_batch, dec_seq, D_MODEL),
            heads_t(kpt, batch, seq), heads_t(vpt, batch, seq), state(stp, batch, 0), state(stp, batch, 1),
            heads(ksm, dec_batch, dec_seq), heads(vsm, dec_batch, dec_seq),
            state(sts, dec_batch, 0), state(sts, dec_batch, 1))
```

```python
import functools

import jax
import jax.numpy as jnp
from jax import lax
from jax.experimental import pallas as pl
from jax.experimental.pallas import tpu as pltpu

F32 = jnp.float32
BF16 = jnp.bfloat16

D_MODEL = 1024
D_SSM = 512
D_ATTN = 512
SSM_GROUP = 16
N_SSM_GROUPS = D_SSM // SSM_GROUP
SSM_STATE = 64
D_STATE = N_SSM_GROUPS * SSM_STATE
PROJ_CHUNK = 512
HEAD_DIM = 64
N_HEADS = D_ATTN // HEAD_DIM
D_IN_PROJ = 2 * D_SSM + 4 * D_ATTN
RMS_EPS = 1e-6
DILATED_PATTERNS = ((128, 1), (512, 4), (2048, 16))
BAND = 128
WINDOW_MAX = 2048
LANES = 128
SUBLANES = 8
N_CH_TILES = D_SSM // LANES
GROUPS_PER_TILE = LANES // SSM_GROUP
STATES_PER_TILE = GROUPS_PER_TILE * SSM_STATE
VMEM_LIMIT = 56 * 1024 * 1024
ROW_TILE = 512
IN_PROJ_ROW_TILE = 1024
SSM_STEPS = 64
SSM_SUB_STEPS = 32
GUEST_STAGE = 3
NEG_INF = float("-inf")


def _sigmoid(x):
    return 1.0 / (1.0 + jnp.exp(-x))


def _rms_scale(x):
    return x * lax.rsqrt(jnp.mean(x * x, axis=-1, keepdims=True) + RMS_EPS)


def _ssm_prep_kernel(are_ref, aim_ref, ldt_ref, bre_ref, bim_ref, lam_ref, bbre_ref, bbim_ref):
    a_re = are_ref[...]
    a_im = aim_ref[...]
    dt = jnp.exp(ldt_ref[...])
    mag = jnp.exp(a_re * dt)
    ang = a_im * dt
    lam_re = mag * jnp.cos(ang)
    lam_im = mag * jnp.sin(ang)
    den = a_re * a_re + a_im * a_im
    f_re = ((lam_re - 1.0) * a_re + lam_im * a_im) / den
    f_im = (lam_im * a_re - (lam_re - 1.0) * a_im) / den
    lam_ref[:, 0:D_STATE] = jnp.broadcast_to(lam_re, (SUBLANES, D_STATE))
    lam_ref[:, D_STATE:2 * D_STATE] = jnp.broadcast_to(lam_im, (SUBLANES, D_STATE))
    b_re = bre_ref[...]
    b_im = bim_ref[...]
    bbre_ref[...] = f_re * b_re - f_im * b_im
    bbim_ref[...] = f_re * b_im + f_im * b_re


def _ssm_prep(a_re, a_im, log_dt, b_re, b_im):
    flat = lambda a: a.reshape(1, D_STATE)
    ldt = jnp.repeat(log_dt, SSM_STATE).reshape(1, D_STATE)
    bt = lambda b: b.transpose(2, 0, 1).reshape(SSM_GROUP, D_STATE)
    return pl.pallas_call(
        _ssm_prep_kernel,
        out_shape=(jax.ShapeDtypeStruct((SUBLANES, 2 * D_STATE), F32),
                   jax.ShapeDtypeStruct((SSM_GROUP, D_STATE), F32),
                   jax.ShapeDtypeStruct((SSM_GROUP, D_STATE), F32)),
        name="ssm_prep",
    )(flat(a_re), flat(a_im), ldt, bt(b_re), bt(b_im))


def _block_diag_weights(bb_re, bb_im, c_re, c_im):
    eye = jnp.eye(GROUPS_PER_TILE, dtype=F32)
    def place_b(bb):
        b4 = bb.reshape(SSM_GROUP, N_CH_TILES, GROUPS_PER_TILE, SSM_STATE).transpose(1, 2, 0, 3)
        return jnp.einsum("kgcp,gh->kgchp", b4, eye).reshape(N_CH_TILES, LANES, STATES_PER_TILE)
    def place_c(c):
        c4 = c.reshape(N_CH_TILES, GROUPS_PER_TILE, SSM_GROUP, SSM_STATE)
        return jnp.einsum("kgcp,gh->khpgc", c4, eye).reshape(N_CH_TILES, STATES_PER_TILE, LANES)
    wb = jnp.concatenate([place_b(bb_re), place_b(bb_im)], axis=2).astype(BF16)
    cc = jnp.concatenate([place_c(c_re), -place_c(c_im)], axis=1).astype(BF16)
    return wb, cc


def _in_proj_kernel(x_ref, g_ref, w_ref, xz_ref, q_ref, k_ref, v_ref, za_ref, *t_refs):
    xn = (_rms_scale(x_ref[...]) * g_ref[...]).astype(BF16)
    outs = (None, None, q_ref, k_ref, v_ref, za_ref)
    for j in range(D_IN_PROJ // PROJ_CHUNK):
        w = w_ref[:, j * PROJ_CHUNK:(j + 1) * PROJ_CHUNK].astype(BF16)
        r = jnp.dot(xn, w, preferred_element_type=F32)
        if j in (1, 5):
            r = r * _sigmoid(r)
        if j == 2:
            r = r * (HEAD_DIM ** -0.5)
        if j < 2:
            for s in range(4):
                xz_ref[j * 4 + s] = r[:, s * LANES:(s + 1) * LANES]
        else:
            outs[j][...] = r.astype(outs[j].dtype)
            if t_refs and j in (3, 4):
                t_refs[j - 3][...] = r.T


def _in_proj(x, norm_g, w_in, *, seq_len=None):
    rows = x.shape[0]
    tile = min(rows, IN_PROJ_ROW_TILE)
    row_spec = lambda width: pl.BlockSpec((tile, width), lambda i: (i, 0))
    out_specs = [pl.BlockSpec((8, tile, LANES), lambda i: (0, i, 0))] + [row_spec(512)] * 4
    out_shape = ([jax.ShapeDtypeStruct((8, rows, LANES), F32)] + [jax.ShapeDtypeStruct((rows, 512), F32)] * 3
                 + [jax.ShapeDtypeStruct((rows, 512), F32 if seq_len is None else BF16)])
    if seq_len is not None:
        per_seq = seq_len // tile
        out_specs += [pl.BlockSpec((None, 512, tile), lambda i: (i // per_seq, 0, i % per_seq))] * 2
        out_shape += [jax.ShapeDtypeStruct((rows // seq_len, 512, seq_len), F32)] * 2
    return pl.pallas_call(
        _in_proj_kernel,
        grid=(rows // tile,),
        in_specs=[row_spec(D_MODEL),
                  pl.BlockSpec((1, D_MODEL), lambda i: (0, 0)),
                  pl.BlockSpec((D_MODEL, D_IN_PROJ), lambda i: (0, 0), pipeline_mode=pl.Buffered(1))],
        out_specs=out_specs,
        out_shape=out_shape,
        compiler_params=pltpu.CompilerParams(dimension_semantics=("parallel",), vmem_limit_bytes=VMEM_LIMIT),
        name="in_proj",
    )(x, norm_g.reshape(1, D_MODEL), w_in)


def _ssm_kernel(xz_ref, h0_ref, lam_ref, wb_ref, cc_ref, d_ref, wglu_ref, bglu_ref, g_ref, *rest, nb, tt,
                guest_half=None, guest_tt=None):
    rows = nb * tt
    prompt = xz_ref.ndim == 4
    with_out_proj = len(rest) > 5
    if guest_half is not None:
        (x_ref, ya_ref, ga_ref, wout_ref, gf_ref, *guest_in, y_ref, st_ref, guest_out,
         u_scr, bu_scr, ys_scr, mix_scr) = rest
        guest_src = (*guest_in, guest_out, guest_half(pl.program_id(0)))
    elif with_out_proj:
        x_ref, ya_ref, ga_ref, wout_ref, gf_ref, y_ref, st_ref, u_scr, bu_scr, ys_scr, mix_scr = rest
    else:
        ys_ref, st_ref, u_scr, bu_scr, ys_scr = rest

    @pl.when(pl.program_id(0) == 0)
    def _():
        st_ref[...] = h0_ref[...]

    for s in range(8):
        if prompt:
            for b in range(nb):
                u_scr[s, pl.ds(b, tt, stride=nb), :] = xz_ref[s, b]
        else:
            for t in range(tt):
                u_scr[s, t * nb:(t + 1) * nb, :] = xz_ref[s, pl.ds(t, nb, stride=tt), :]

    sub = min(tt, SSM_SUB_STEPS)
    sub_rows = [slice(j * sub * nb, (j + 1) * sub * nb) for j in range(tt // sub)]

    def expand(rs):
        w = STATES_PER_TILE
        for kt in range(N_CH_TILES):
            r = jnp.dot(u_scr[kt, rs, :].astype(BF16), wb_ref[kt], preferred_element_type=F32)
            bu_scr[rs, kt * w:(kt + 1) * w] = r[:, 0:w]
            bu_scr[rs, D_STATE + kt * w:D_STATE + (kt + 1) * w] = r[:, w:2 * w]

    width = (SUBLANES * LANES * 8) // nb if nb <= 64 else LANES
    chunks = [(slice(c * width, (c + 1) * width), slice(D_STATE + c * width, D_STATE + (c + 1) * width))
              for c in range(D_STATE // width)]

    def recur(j):
        for re_l, im_l in chunks:
            if nb == SUBLANES:
                lam_re, lam_im = lam_ref[:, re_l], lam_ref[:, im_l]
            else:
                lam_re = jnp.broadcast_to(lam_ref[0:1, re_l], (nb, width))
                lam_im = jnp.broadcast_to(lam_ref[0:1, im_l], (nb, width))
            s_re, s_im = st_ref[:, re_l], st_ref[:, im_l]
            for t in range(j * sub, (j + 1) * sub):
                tr = slice(t * nb, (t + 1) * nb)
                s_re, s_im = (lam_re * s_re - lam_im * s_im + bu_scr[tr, re_l],
                              lam_re * s_im + lam_im * s_re + bu_scr[tr, im_l])
                bu_scr[tr, re_l] = s_re
                bu_scr[tr, im_l] = s_im
            st_ref[:, re_l] = s_re
            st_ref[:, im_l] = s_im

    def project(rs):
        ys = []
        w = STATES_PER_TILE
        for kt in range(N_CH_TILES):
            xc = jnp.concatenate([bu_scr[rs, kt * w:(kt + 1) * w],
                                  bu_scr[rs, D_STATE + kt * w:D_STATE + (kt + 1) * w]], axis=1).astype(BF16)
            y = jnp.dot(xc, cc_ref[kt], preferred_element_type=F32)
            ys.append(y + d_ref[:, kt * LANES:(kt + 1) * LANES] * u_scr[kt, rs, :])
        y = jnp.concatenate(ys, axis=1)
        g = jax.nn.gelu(y)
        gl = jnp.dot(g.astype(BF16), wglu_ref[...], preferred_element_type=F32) + bglu_ref[...]
        gate = jnp.concatenate([u_scr[4 + s, rs, :] for s in range(4)], axis=1)
        y = gl[:, 0:D_SSM] * _sigmoid(gl[:, D_SSM:2 * D_SSM]) * gate
        y = _rms_scale(y) * g_ref[...]
        for s in range(4):
            ys_scr[s, rs, :] = y[:, s * LANES:(s + 1) * LANES]

    n_sub = len(sub_rows)
    for j in range(n_sub + 2):
        if j == GUEST_STAGE and guest_half is not None:
            _sample_attn_seqs([guest_src], guest_tt)
        if j < n_sub:
            expand(sub_rows[j])
        if 1 <= j <= n_sub:
            recur(j - 1)
        if j >= 2:
            project(sub_rows[j - 2])

    if with_out_proj:
        for b in range(nb):
            rb = slice(b * tt, (b + 1) * tt)
            for s in range(4):
                mix_scr[rb, s * LANES:(s + 1) * LANES] = ys_scr[s, pl.ds(b, tt, stride=nb), :].astype(BF16)
            mix_scr[rb, D_SSM:D_SSM + D_ATTN] = (_rms_scale(ya_ref[b].astype(F32)) * ga_ref[...]).astype(BF16)
        h = x_ref[...].reshape(rows, D_MODEL) + jnp.dot(mix_scr[...], wout_ref[...], preferred_element_type=F32)
        y_ref[...] = (_rms_scale(h) * gf_ref[...]).reshape(nb, tt, D_MODEL)
        return
    for s in range(4):
        if prompt:
            for b in range(nb):
                ys_ref[s, b] = ys_scr[s, pl.ds(b, tt, stride=nb), :].astype(ys_ref.dtype)
        else:
            for t in range(tt):
                ys_ref[s, pl.ds(t, nb, stride=tt), :] = ys_scr[s, t * nb:(t + 1) * nb, :]


def _ssm(xz, h0, lam, wb, cc, d_vec, w_glu_bf, b_glu, norm_g, *, nb, t_total, tt, out_proj=None, guest=None):
    rows = nb * tt
    const = lambda shape: pl.BlockSpec(shape, lambda i: (0,) * len(shape))
    seq_block = lambda width: pl.BlockSpec((nb, tt, width), lambda i: (0, i, 0))
    args = [xz, h0, lam, wb, cc, d_vec.reshape(1, D_SSM), w_glu_bf, b_glu.reshape(1, 2 * D_SSM),
            norm_g.reshape(1, D_SSM)]
    in_specs = [None, const((nb, 2 * D_STATE)), const((SUBLANES, 2 * D_STATE)),
                const((N_CH_TILES, LANES, 2 * STATES_PER_TILE)), const((N_CH_TILES, 2 * STATES_PER_TILE, LANES)),
                const((1, D_SSM)),
                const((D_SSM, 2 * D_SSM)), const((1, 2 * D_SSM)), const((1, D_SSM))]
    scratch = [pltpu.VMEM((8, rows, LANES), F32), pltpu.VMEM((rows, 2 * D_STATE), F32),
               pltpu.VMEM((4, rows, LANES), F32)]
    if t_total > tt:
        args[0] = xz.reshape(8, nb, t_total, LANES)
        in_specs[0] = pl.BlockSpec((8, nb, tt, LANES), lambda i: (0, 0, i, 0))
        out_spec = pl.BlockSpec((4, nb, tt, LANES), lambda i: (0, 0, i, 0))
        out_shape = jax.ShapeDtypeStruct((4, nb, t_total, LANES), BF16)
    else:
        in_specs[0] = const((8, rows, LANES))
        out_spec = const((4, rows, LANES))
        out_shape = jax.ShapeDtypeStruct((4, rows, LANES), F32)
    out_specs = [out_spec, const((nb, 2 * D_STATE))]
    out_shapes = [out_shape, jax.ShapeDtypeStruct((nb, 2 * D_STATE), F32)]
    kernel_kwargs = dict(nb=nb, tt=tt)
    if out_proj is not None:
        x, ya, norm_attn_g, w_out_bf, final_g = out_proj
        args += [x, ya, norm_attn_g.reshape(1, D_ATTN), w_out_bf, final_g.reshape(1, D_MODEL)]
        in_specs += [seq_block(D_MODEL), seq_block(D_ATTN), const((1, D_ATTN)), const((D_MODEL, D_MODEL)),
                     const((1, D_MODEL))]
        out_specs[0] = seq_block(D_MODEL)
        out_shapes[0] = jax.ShapeDtypeStruct((nb, t_total, D_MODEL), F32)
        scratch.append(pltpu.VMEM((rows, D_MODEL), BF16))
    if guest is not None:
        stok, cache_kt, cache_vt, first_seq, guest_tt = guest
        tok, cache, guest_out, half = _sample_specs(first_seq, 1, 0, guest_tt)
        args += [*stok, cache_kt, cache_vt]
        in_specs += [tok] * 4 + [cache] * 2
        out_specs.append(guest_out)
        out_shapes.append(jax.ShapeDtypeStruct((t_total // tt, SUBLANES, D_ATTN), F32))
        kernel_kwargs.update(guest_half=half, guest_tt=guest_tt)
    out, st, *guest_res = pl.pallas_call(
        functools.partial(_ssm_kernel, **kernel_kwargs),
        grid=(t_total // tt,),
        in_specs=in_specs,
        out_specs=out_specs,
        out_shape=out_shapes,
        scratch_shapes=scratch,
        compiler_params=pltpu.CompilerParams(dimension_semantics=("arbitrary",), vmem_limit_bytes=VMEM_LIMIT),
        name="ssm_prompt" if t_total > tt else "ssm_sample",
    )(*args)
    return ((out if out_proj is not None else out.reshape(4, nb * t_total, LANES)), st, *guest_res)


PART0_SIXTEENTHS = 14
GROUP_BLOCKS = 2


def _band_scores(qs, ks, biases):
    low = lax.broadcasted_iota(jnp.int32, (BAND, LANES), 1) < HEAD_DIM
    nt = (((1,), (1,)), ((), ()))
    stacked = []
    for q in qs:
        stacked.append(jnp.concatenate([jnp.where(low, q, 0.0), jnp.where(low, 0.0, q)], axis=0).astype(BF16))
    return [lax.dot_general(q, k.astype(BF16), nt, preferred_element_type=F32) + bias
            for q, k, bias in zip(stacked, ks, biases)]


def _band_softmax_pv(ss, vs):
    low = lax.broadcasted_iota(jnp.int32, (BAND, LANES), 1) < HEAD_DIM
    ms = [jnp.max(s, axis=-1, keepdims=True) for s in ss]
    ps = [jnp.exp(s - m).astype(BF16) for s, m in zip(ss, ms)]
    outs = [jnp.dot(p, jnp.concatenate([v.astype(BF16), jnp.ones(v.shape, BF16)], axis=1),
                    preferred_element_type=F32) for p, v in zip(ps, vs)]
    wide = lambda a: jnp.broadcast_to(a, (BAND, a.shape[1] if a.shape[1] > 1 else LANES))
    pick = lambda a: jnp.where(low, wide(a[0:BAND]), wide(a[BAND:2 * BAND]))
    return [(pick(o[:, 0:LANES]), pick(m), pick(o[:, LANES:2 * LANES])) for o, m in zip(outs, ms)]


def _prompt_attn_part(q_ref, k_ref, v_ref, gate_ref, o_ref, o_scr, m_scr, l_scr, band_scr, causal_scr, part):
    t_total = q_ref.shape[0]
    if part == 0:
        for scr, kwin in ((band_scr, 2 * BAND), (causal_scr, BAND)):
            qq = lax.broadcasted_iota(jnp.int32, (2 * BAND, kwin), 0) & (BAND - 1)
            kk = lax.broadcasted_iota(jnp.int32, (2 * BAND, kwin), 1)
            valid = ((kk >= qq) & (kk <= qq + BAND)) if kwin > BAND else (kk <= qq)
            scr[...] = jnp.where(valid, 0.0, NEG_INF)

    blocks = []
    for pidx, (window, dil) in enumerate(DILATED_PATTERNS):
        nblk = t_total // dil // BAND
        rows = (lambda r0, n, dil=dil: pl.ds(r0, n, stride=dil) if dil > 1 else pl.ds(r0, n))
        for rho in range(dil):
            blocks.append((pidx, rows(rho, BAND), rows(rho, BAND), causal_scr))
            blocks += [(pidx, rows(rho + dil * BAND * i, BAND), rows(rho + dil * BAND * (i - 1), 2 * BAND), band_scr)
                       for i in range(1, nblk)]

    def scores(group):
        return _band_scores([q_ref[qr, :] for _, qr, _, _ in group], [k_ref[kr, :] for _, _, kr, _ in group],
                            [bias[...] for _, _, _, bias in group])

    def finish(group, ss):
        res = _band_softmax_pv(ss, [v_ref[kr, :] for _, _, kr, _ in group])
        for (pidx, qr, _, _), (o, m, l) in zip(group, res):
            o_scr[pidx, qr, :] = o
            m_scr[pidx, qr, :] = m
            l_scr[pidx, qr, :] = l

    groups = [blocks[g:g + GROUP_BLOCKS] for g in range(0, len(blocks), GROUP_BLOCKS)]
    half = (len(groups) * PART0_SIXTEENTHS) // 16
    pending = None
    for group in (groups[:half] if part == 0 else groups[half:]):
        ss = scores(group)
        if pending is not None:
            finish(*pending)
        pending = (group, ss)
    finish(*pending)

    if part == 1:
        chunk = 256
        def combine(c, _):
            rows = pl.ds(pl.multiple_of(c * chunk, chunk), chunk)
            ms = [m_scr[p, rows, :] for p in range(3)]
            big = jnp.maximum(jnp.maximum(ms[0], ms[1]), ms[2])
            ws = [jnp.exp(m - big) for m in ms]
            num = ws[0] * o_scr[0, rows, :] + ws[1] * o_scr[1, rows, :] + ws[2] * o_scr[2, rows, :]
            den = ws[0] * l_scr[0, rows, :] + ws[1] * l_scr[1, rows, :] + ws[2] * l_scr[2, rows, :]
            o_ref[rows, :] = ((num / den) * gate_ref[rows, :].astype(F32)).astype(o_ref.dtype)
            return 0
        lax.fori_loop(0, t_total // chunk, combine, 0)


NEAR = 512


def _sample_attn_seqs(sources, tt):
    assert 2 * tt == SUBLANES
    nt = (((1,), (1,)), ((), ()))
    row8 = lax.broadcasted_iota(jnp.int32, (SUBLANES, LANES), 0)
    lane8 = lax.broadcasted_iota(jnp.int32, (SUBLANES, LANES), 1)
    top = row8 < tt
    own_head = top == (lane8 < HEAD_DIM)

    def dup(x, half):
        return jnp.where(row8 // tt == half, x, pltpu.roll(x, tt, axis=0))

    jq = lambda shape: lax.broadcasted_iota(jnp.int32, shape, 0) & (tt - 1)
    d_full = WINDOW_MAX + jq((SUBLANES, WINDOW_MAX)) - lax.broadcasted_iota(jnp.int32, (SUBLANES, WINDOW_MAX), 1)
    d_near = NEAR + jq((SUBLANES, NEAR)) - lax.broadcasted_iota(jnp.int32, (SUBLANES, NEAR), 1)
    d_new = jq((SUBLANES, SUBLANES)) - lax.broadcasted_iota(jnp.int32, (SUBLANES, SUBLANES), 1)
    new_key = lax.broadcasted_iota(jnp.int32, (SUBLANES, SUBLANES), 1) < tt
    ok_main, ok_new = [], []
    for window, dil in DILATED_PATTERNS:
        d = d_near if window <= NEAR else d_full
        ok_main.append(((d & (dil - 1)) == 0) & (d <= window))
        ok_new.append(new_key & (d_new >= 0) & ((d_new & (dil - 1)) == 0))

    insts = [(slice(pr * LANES, (pr + 1) * LANES), src) for src in sources for pr in range(D_ATTN // LANES)]

    scored = []
    for lanes, (q_ref, kn_ref, _, _, kc_ref, _, _, half) in insts:
        q = jnp.where(own_head, dup(q_ref[:, lanes], half), 0.0).astype(BF16)
        kn = dup(kn_ref[:, lanes], half).astype(BF16)
        s_main = jnp.dot(q, kc_ref[lanes, :].astype(BF16), preferred_element_type=F32)
        s_new = lax.dot_general(q, kn, nt, preferred_element_type=F32)
        scored.append((s_main, s_new))

    weighted = []
    for s_main, s_new in scored:
        stats = []
        for (window, dil), okm, okn in zip(DILATED_PATTERNS, ok_main, ok_new):
            sm = jnp.where(okm, s_main[:, WINDOW_MAX - NEAR:] if window <= NEAR else s_main, NEG_INF)
            sn = jnp.where(okn, s_new, NEG_INF)
            m = jnp.maximum(jnp.max(sm, axis=-1, keepdims=True), jnp.max(sn, axis=-1, keepdims=True))
            pm = jnp.exp(sm - m)
            pn = jnp.exp(sn - m)
            l = jnp.sum(pm, axis=-1, keepdims=True) + jnp.sum(pn, axis=-1, keepdims=True)
            stats.append((m, l, pm, pn))
        big = functools.reduce(jnp.maximum, [s[0] for s in stats])
        ws = [jnp.exp(s[0] - big) for s in stats]
        den = functools.reduce(jnp.add, [w * s[1] for w, s in zip(ws, stats)])
        p_new = functools.reduce(jnp.add, [w * s[3] for w, s in zip(ws, stats)])
        p_near = functools.reduce(jnp.add, [w * s[2] for w, s in zip(ws, stats) if s[2].shape[1] == NEAR])
        p_full = functools.reduce(jnp.add, [w * s[2] for w, s in zip(ws, stats) if s[2].shape[1] != NEAR])
        p_main = jnp.concatenate([p_full[:, 0:WINDOW_MAX - NEAR], p_full[:, WINDOW_MAX - NEAR:] + p_near], axis=1)
        weighted.append((p_main.astype(BF16), p_new.astype(BF16).astype(F32), den))

    results = []
    for (lanes, (_, _, vn_ref, gate_ref, _, vc_ref, _, half)), (p_main, p_new, den) in zip(insts, weighted):
        acc = lax.dot_general(p_main, vc_ref[lanes, :].astype(BF16), nt, preferred_element_type=F32)
        vn = dup(vn_ref[:, lanes], half).astype(BF16).astype(F32)
        for j in range(tt):
            acc = acc + p_new[:, j:j + 1] * vn[j:j + 1, :]
        acc = acc / den
        merged = jnp.where(lane8 < HEAD_DIM, acc, pltpu.roll(acc, tt, axis=0))
        results.append(merged * dup(gate_ref[:, lanes], half))

    n_pairs = D_ATTN // LANES
    for s, src in enumerate(sources):
        src[6][...] = jnp.concatenate(results[s * n_pairs:(s + 1) * n_pairs], axis=1)


def _sample_specs(first_seq, steps_per_seq, lag, tt):
    local = lambda i: jnp.maximum(i - lag, 0) // steps_per_seq
    seq = lambda i: first_seq + local(i)
    tok = pl.BlockSpec((SUBLANES, D_ATTN), lambda i: (seq(i) * tt // SUBLANES, 0))
    cache = pl.BlockSpec((None, D_ATTN, WINDOW_MAX), lambda i: (seq(i), 0, 0))
    out = pl.BlockSpec((None, SUBLANES, D_ATTN), lambda i: (local(i), 0, 0))
    half = lambda i: seq(i) % (SUBLANES // tt)
    return tok, cache, out, half


def _attention_kernel(*refs, tt, halves):
    tok_a, tok_b = refs[0:4], refs[4:8]
    (kc_a, vc_a, kc_b, vc_b, pq_ref, pk_ref, pv_ref, pgate_ref,
     so_a, so_b, po_ref, o_scr, m_scr, l_scr, band_scr, causal_scr) = refs[8:]
    i = pl.program_id(0)
    src_a = (*tok_a, kc_a, vc_a, so_a, halves[0](i))
    src_b = (*tok_b, kc_b, vc_b, so_b, halves[1](i))
    for part in range(2):
        @pl.when(i % 2 == part)
        def _(part=part):
            _sample_attn_seqs([src_a, src_b] if part == 1 else [src_a], tt)
            _prompt_attn_part(pq_ref, pk_ref, pv_ref, pgate_ref, po_ref, o_scr, m_scr, l_scr,
                              band_scr, causal_scr, part)


def _attention(stok, cache_kt, cache_vt, pq, pk, pv, pgate, *, n_seqs, tt, batch, t_total):
    pairs = D_ATTN // LANES
    steps = 2 * batch * pairs
    n_a, n_b = steps, steps // 2
    assert n_seqs == n_a + n_b
    tok_a, cache_a, out_a, half_a = _sample_specs(0, 1, 0, tt)
    tok_b, cache_b, out_b, half_b = _sample_specs(n_a, 2, 1, tt)
    shp = (batch, t_total, D_ATTN)
    unit = pl.BlockSpec((None, t_total, LANES), lambda i: (i // (2 * pairs), 0, (i // 2) % pairs))
    so_a, so_b, po = pl.pallas_call(
        functools.partial(_attention_kernel, tt=tt, halves=(half_a, half_b)),
        grid=(steps,),
        in_specs=[tok_a] * 4 + [tok_b] * 4 + [cache_a, cache_a, cache_b, cache_b] + [unit] * 4,
        out_specs=[out_a, out_b, unit],
        out_shape=[jax.ShapeDtypeStruct((n_a, SUBLANES, D_ATTN), F32),
                   jax.ShapeDtypeStruct((n_b, SUBLANES, D_ATTN), F32), jax.ShapeDtypeStruct(shp, BF16)],
        scratch_shapes=[pltpu.VMEM((3, t_total, LANES), F32)] * 3
                       + [pltpu.VMEM((2 * BAND, 2 * BAND), F32), pltpu.VMEM((2 * BAND, BAND), F32)],
        compiler_params=pltpu.CompilerParams(dimension_semantics=("arbitrary",), vmem_limit_bytes=VMEM_LIMIT),
        name="attention",
    )(*stok, *stok, cache_kt, cache_vt, cache_kt, cache_vt,
      pq.reshape(shp), pk.reshape(shp), pv.reshape(shp), pgate.reshape(shp))
    return jnp.concatenate([so_a, so_b], axis=0), po.reshape(batch * t_total, D_ATTN)


def _out_proj_kernel(x_ref, ys_ref, ya_ref, ga_ref, w_ref, gf_ref, y_ref):
    ys = jnp.concatenate([ys_ref[s] for s in range(4)], axis=1).astype(BF16)
    ya = (_rms_scale(ya_ref[...]) * ga_ref[...]).astype(BF16)
    mix = jnp.concatenate([ys, ya], axis=1)
    h = x_ref[...] + jnp.dot(mix, w_ref[...], preferred_element_type=F32)
    y_ref[...] = _rms_scale(h) * gf_ref[...]


def _out_proj(x, ys, ya, norm_attn_g, w_out_bf, final_g):
    rows = x.shape[0]
    row_spec = lambda width: pl.BlockSpec((ROW_TILE, width), lambda i: (i, 0))
    const = lambda shape: pl.BlockSpec(shape, lambda i: (0,) * len(shape))
    return pl.pallas_call(
        _out_proj_kernel,
        grid=(rows // ROW_TILE,),
        in_specs=[row_spec(D_MODEL), pl.BlockSpec((4, ROW_TILE, LANES), lambda i: (0, i, 0)), row_spec(D_ATTN),
                  const((1, D_ATTN)), const((D_MODEL, D_MODEL)), const((1, D_MODEL))],
        out_specs=row_spec(D_MODEL),
        out_shape=jax.ShapeDtypeStruct((rows, D_MODEL), F32),
        compiler_params=pltpu.CompilerParams(dimension_semantics=("parallel",), vmem_limit_bytes=VMEM_LIMIT),
        name="out_proj",
    )(x, ys, ya, norm_attn_g.reshape(1, D_ATTN), w_out_bf, final_g.reshape(1, D_MODEL))


def kernel(x_prompt, x_sample, cache_k, cache_v, state_ssm_re, state_ssm_im, norm_in_g, w_in,
           ssm_A_re, ssm_A_im, ssm_log_dt, ssm_B_re, ssm_B_im, ssm_C_re, ssm_C_im, ssm_D,
           w_glu, b_glu, norm_ssm_g, norm_attn_g, w_out, final_norm_g):
    depth = w_in.shape[0]
    assert depth == 1
    batch, seq, _ = x_prompt.shape
    dec_batch, dec_seq, _ = x_sample.shape
    assert batch == SUBLANES and seq == WINDOW_MAX and seq % SSM_STEPS == 0

    lam, bb_re, bb_im = _ssm_prep(ssm_A_re[0], ssm_A_im[0], ssm_log_dt[0], ssm_B_re[0], ssm_B_im[0])
    wb, cc = _block_diag_weights(bb_re, bb_im, ssm_C_re[0], ssm_C_im[0])
    w_glu_bf = w_glu[0].astype(BF16)
    w_out_bf = w_out[0].astype(BF16)

    ssm = functools.partial(_ssm, lam=lam, wb=wb, cc=cc, d_vec=ssm_D[0], w_glu_bf=w_glu_bf, b_glu=b_glu[0],
                            norm_g=norm_ssm_g[0])

    xp = x_prompt.reshape(batch * seq, D_MODEL)
    xzp, qp, kp, vp, gp, kpt, vpt = _in_proj(xp, norm_in_g[0], w_in[0], seq_len=seq)
    xs = x_sample.reshape(dec_batch * dec_seq, D_MODEL)
    xzs, qs, ksm, vsm, gs = _in_proj(xs, norm_in_g[0], w_in[0])

    chan_major = lambda c: c[0].transpose(0, 2, 3, 1).reshape(dec_batch, D_ATTN, WINDOW_MAX)
    cache_kt, cache_vt = chan_major(cache_k), chan_major(cache_v)
    stok = (qs, ksm, vsm, gs)
    n_guest = seq // SSM_STEPS
    yas_main, yap = _attention(stok, cache_kt, cache_vt, qp, kp, vp, gp,
                               n_seqs=dec_batch - n_guest, tt=dec_seq, batch=batch, t_total=seq)

    yp, stp, yas_guest = ssm(xzp, jnp.zeros((batch, 2 * D_STATE), F32), nb=batch, t_total=seq, tt=SSM_STEPS,
                             out_proj=(x_prompt, yap.reshape(batch, seq, D_ATTN), norm_attn_g[0], w_out_bf,
                                       final_norm_g),
                             guest=(stok, cache_kt, cache_vt, dec_batch - n_guest, dec_seq))
    yas = jnp.concatenate([yas_main, yas_guest], axis=0)[:, 0:dec_seq].reshape(dec_batch * dec_seq, D_ATTN)
    h0 = jnp.concatenate([state_ssm_re[0].reshape(dec_batch, D_STATE),
                          state_ssm_im[0].reshape(dec_batch, D_STATE)], axis=1)
    yss, sts = ssm(xzs, h0, nb=dec_batch, t_total=dec_seq, tt=dec_seq)
    ysm = _out_proj(xs, yss, yas, norm_attn_g[0], w_out_bf, final_norm_g)

    heads = lambda a, n, t: a.reshape(1, n, t, N_HEADS, HEAD_DIM)
    heads_t = lambda a, n, t: a.reshape(n, N_HEADS, HEAD_DIM, t).transpose(0, 3, 1, 2)[None]
    state = lambda s, n, part: s[:, part * D_STATE:(part + 1) * D_STATE].reshape(1, n, N_SSM_GROUPS, SSM_STATE)
    return (yp.reshape(batch, seq, D_MODEL), ysm.reshape(dec_batch, dec_seq, D_MODEL),
            heads_t(kpt, batch, seq), heads_t(vpt, batch, seq), state(stp, batch, 0), state(stp, batch, 1),
            heads(ksm, dec_batch, dec_seq), heads(vsm, dec_batch, dec_seq),
            state(sts, dec_batch, 0), state(sts, dec_batch, 1))
```

```python
import functools

import jax
import jax.numpy as jnp
from jax import lax
from jax.experimental import pallas as pl
from jax.experimental.pallas import tpu as pltpu

F32 = jnp.float32
BF16 = jnp.bfloat16

D_MODEL = 1024
D_SSM = 512
D_ATTN = 512
SSM_GROUP = 16
N_SSM_GROUPS = D_SSM // SSM_GROUP
SSM_STATE = 64
D_STATE = N_SSM_GROUPS * SSM_STATE
PROJ_CHUNK = 512
HEAD_DIM = 64
N_HEADS = D_ATTN // HEAD_DIM
D_IN_PROJ = 2 * D_SSM + 4 * D_ATTN
RMS_EPS = 1e-6
DILATED_PATTERNS = ((128, 1), (512, 4), (2048, 16))
BAND = 128
WINDOW_MAX = 2048
LANES = 128
SUBLANES = 8
N_CH_TILES = D_SSM // LANES
GROUPS_PER_TILE = LANES // SSM_GROUP
STATES_PER_TILE = GROUPS_PER_TILE * SSM_STATE
VMEM_LIMIT = 56 * 1024 * 1024
ROW_TILE = 512
IN_PROJ_ROW_TILE = 1024
SSM_STEPS = 64
SSM_SUB_STEPS = 32
GUEST_STAGE = 3
NEG_INF = float("-inf")


def _sigmoid(x):
    return 1.0 / (1.0 + jnp.exp(-x))


def _rms_scale(x):
    return x * lax.rsqrt(jnp.mean(x * x, axis=-1, keepdims=True) + RMS_EPS)


def _ssm_prep_kernel(are_ref, aim_ref, ldt_ref, bre_ref, bim_ref, lam_ref, bbre_ref, bbim_ref):
    a_re = are_ref[...]
    a_im = aim_ref[...]
    dt = jnp.exp(ldt_ref[...])
    mag = jnp.exp(a_re * dt)
    ang = a_im * dt
    lam_re = mag * jnp.cos(ang)
    lam_im = mag * jnp.sin(ang)
    den = a_re * a_re + a_im * a_im
    f_re = ((lam_re - 1.0) * a_re + lam_im * a_im) / den
    f_im = (lam_im * a_re - (lam_re - 1.0) * a_im) / den
    lam_ref[:, 0:D_STATE] = jnp.broadcast_to(lam_re, (SUBLANES, D_STATE))
    lam_ref[:, D_STATE:2 * D_STATE] = jnp.broadcast_to(lam_im, (SUBLANES, D_STATE))
    b_re = bre_ref[...]
    b_im = bim_ref[...]
    bbre_ref[...] = f_re * b_re - f_im * b_im
    bbim_ref[...] = f_re * b_im + f_im * b_re


def _ssm_prep(a_re, a_im, log_dt, b_re, b_im):
    flat = lambda a: a.reshape(1, D_STATE)
    ldt = jnp.repeat(log_dt, SSM_STATE).reshape(1, D_STATE)
    bt = lambda b: b.transpose(2, 0, 1).reshape(SSM_GROUP, D_STATE)
    return pl.pallas_call(
        _ssm_prep_kernel,
        out_shape=(jax.ShapeDtypeStruct((SUBLANES, 2 * D_STATE), F32),
                   jax.ShapeDtypeStruct((SSM_GROUP, D_STATE), F32),
                   jax.ShapeDtypeStruct((SSM_GROUP, D_STATE), F32)),
        name="ssm_prep",
    )(flat(a_re), flat(a_im), ldt, bt(b_re), bt(b_im))


def _block_diag_weights(bb_re, bb_im, c_re, c_im):
    eye = jnp.eye(GROUPS_PER_TILE, dtype=F32)
    def place_b(bb):
        b4 = bb.reshape(SSM_GROUP, N_CH_TILES, GROUPS_PER_TILE, SSM_STATE).transpose(1, 2, 0, 3)
        return jnp.einsum("kgcp,gh->kgchp", b4, eye).reshape(N_CH_TILES, LANES, STATES_PER_TILE)
    def place_c(c):
        c4 = c.reshape(N_CH_TILES, GROUPS_PER_TILE, SSM_GROUP, SSM_STATE)
        return jnp.einsum("kgcp,gh->khpgc", c4, eye).reshape(N_CH_TILES, STATES_PER_TILE, LANES)
    wb = jnp.concatenate([place_b(bb_re), place_b(bb_im)], axis=2).astype(BF16)
    cc = jnp.concatenate([place_c(c_re), -place_c(c_im)], axis=1).astype(BF16)
    return wb, cc


def _in_proj_kernel(x_ref, g_ref, w_ref, xz_ref, q_ref, k_ref, v_ref, za_ref, *t_refs):
    xn = (_rms_scale(x_ref[...]) * g_ref[...]).astype(BF16)
    outs = (None, None, q_ref, k_ref, v_ref, za_ref)
    for j in range(D_IN_PROJ // PROJ_CHUNK):
        w = w_ref[:, j * PROJ_CHUNK:(j + 1) * PROJ_CHUNK].astype(BF16)
        r = jnp.dot(xn, w, preferred_element_type=F32)
        if j in (1, 5):
            r = r * _sigmoid(r)
        if j == 2:
            r = r * (HEAD_DIM ** -0.5)
        if j < 2:
            for s in range(4):
                xz_ref[j * 4 + s] = r[:, s * LANES:(s + 1) * LANES]
        else:
            outs[j][...] = r.astype(outs[j].dtype)
            if t_refs and j in (3, 4):
                t_refs[j - 3][...] = r.T


def _in_proj(x, norm_g, w_in, *, seq_len=None):
    rows = x.shape[0]
    tile = min(rows, IN_PROJ_ROW_TILE)
    row_spec = lambda width: pl.BlockSpec((tile, width), lambda i: (i, 0))
    out_specs = [pl.BlockSpec((8, tile, LANES), lambda i: (0, i, 0))] + [row_spec(512)] * 4
    out_shape = ([jax.ShapeDtypeStruct((8, rows, LANES), F32)] + [jax.ShapeDtypeStruct((rows, 512), F32)] * 3
                 + [jax.ShapeDtypeStruct((rows, 512), F32 if seq_len is None else BF16)])
    if seq_len is not None:
        per_seq = seq_len // tile
        out_specs += [pl.BlockSpec((None, 512, tile), lambda i: (i // per_seq, 0, i % per_seq))] * 2
        out_shape += [jax.ShapeDtypeStruct((rows // seq_len, 512, seq_len), F32)] * 2
    return pl.pallas_call(
        _in_proj_kernel,
        grid=(rows // tile,),
        in_specs=[row_spec(D_MODEL),
                  pl.BlockSpec((1, D_MODEL), lambda i: (0, 0)),
                  pl.BlockSpec((D_MODEL, D_IN_PROJ), lambda i: (0, 0), pipeline_mode=pl.Buffered(1))],
        out_specs=out_specs,
        out_shape=out_shape,
        compiler_params=pltpu.CompilerParams(dimension_semantics=("parallel",), vmem_limit_bytes=VMEM_LIMIT),
        name="in_proj",
    )(x, norm_g.reshape(1, D_MODEL), w_in)


def _ssm_kernel(xz_ref, h0_ref, lam_ref, wb_ref, cc_ref, d_ref, wglu_ref, bglu_ref, g_ref, *rest, nb, tt,
                guest_half=None, guest_tt=None):
    rows = nb * tt
    prompt = xz_ref.ndim == 4
    with_out_proj = len(rest) > 5
    if guest_half is not None:
        (x_ref, ya_ref, ga_ref, wout_ref, gf_ref, *guest_in, y_ref, st_ref, guest_out,
         u_scr, bu_scr, ys_scr, mix_scr) = rest
        guest_src = (*guest_in, guest_out, guest_half(pl.program_id(0)))
    elif with_out_proj:
        x_ref, ya_ref, ga_ref, wout_ref, gf_ref, y_ref, st_ref, u_scr, bu_scr, ys_scr, mix_scr = rest
    else:
        ys_ref, st_ref, u_scr, bu_scr, ys_scr = rest

    @pl.when(pl.program_id(0) == 0)
    def _():
        st_ref[...] = h0_ref[...]

    for s in range(8):
        if prompt:
            for b in range(nb):
                u_scr[s, pl.ds(b, tt, stride=nb), :] = xz_ref[s, b]
        else:
            for t in range(tt):
                u_scr[s, t * nb:(t + 1) * nb, :] = xz_ref[s, pl.ds(t, nb, stride=tt), :]

    sub = min(tt, SSM_SUB_STEPS)
    sub_rows = [slice(j * sub * nb, (j + 1) * sub * nb) for j in range(tt // sub)]

    def expand(rs):
        w = STATES_PER_TILE
        for kt in range(N_CH_TILES):
            r = jnp.dot(u_scr[kt, rs, :].astype(BF16), wb_ref[kt], preferred_element_type=F32)
            bu_scr[rs, kt * w:(kt + 1) * w] = r[:, 0:w]
            bu_scr[rs, D_STATE + kt * w:D_STATE + (kt + 1) * w] = r[:, w:2 * w]

    width = (SUBLANES * LANES * 8) // nb if nb <= 64 else LANES
    chunks = [(slice(c * width, (c + 1) * width), slice(D_STATE + c * width, D_STATE + (c + 1) * width))
              for c in range(D_STATE // width)]

    def recur(j):
        for re_l, im_l in chunks:
            if nb == SUBLANES:
                lam_re, lam_im = lam_ref[:, re_l], lam_ref[:, im_l]
            else:
                lam_re = jnp.broadcast_to(lam_ref[0:1, re_l], (nb, width))
                lam_im = jnp.broadcast_to(lam_ref[0:1, im_l], (nb, width))
            s_re, s_im = st_ref[:, re_l], st_ref[:, im_l]
            for t in range(j * sub, (j + 1) * sub):
                tr = slice(t * nb, (t + 1) * nb)
                s_re, s_im = (lam_re * s_re - lam_im * s_im + bu_scr[tr, re_l],
                              lam_re * s_im + lam_im * s_re + bu_scr[tr, im_l])
                bu_scr[tr, re_l] = s_re
                bu_scr[tr, im_l] = s_im
            st_ref[:, re_l] = s_re
            st_ref[:, im_l] = s_im

    def project(rs):
        ys = []
        w = STATES_PER_TILE
        for kt in range(N_CH_TILES):
            xc = jnp.concatenate([bu_scr[rs, kt * w:(kt + 1) * w],
                                  bu_scr[rs, D_STATE + kt * w:D_STATE + (kt + 1) * w]], axis=1).astype(BF16)
            y = jnp.dot(xc, cc_ref[kt], preferred_element_type=F32)
            ys.append(y + d_ref[:, kt * LANES:(kt + 1) * LANES] * u_scr[kt, rs, :])
        y = jnp.concatenate(ys, axis=1)
        g = jax.nn.gelu(y)
        gl = jnp.dot(g.astype(BF16), wglu_ref[...], preferred_element_type=F32) + bglu_ref[...]
        gate = jnp.concatenate([u_scr[4 + s, rs, :] for s in range(4)], axis=1)
        y = gl[:, 0:D_SSM] * _sigmoid(gl[:, D_SSM:2 * D_SSM]) * gate
        y = _rms_scale(y) * g_ref[...]
        for s in range(4):
            ys_scr[s, rs, :] = y[:, s * LANES:(s + 1) * LANES]

    n_sub = len(sub_rows)
    for j in range(n_sub + 2):
        if j == GUEST_STAGE and guest_half is not None:
            _sample_attn_seqs([guest_src], guest_tt)
        if j < n_sub:
            expand(sub_rows[j])
        if 1 <= j <= n_sub:
            recur(j - 1)
        if j >= 2:
            project(sub_rows[j - 2])

    if with_out_proj:
        for b in range(nb):
            rb = slice(b * tt, (b + 1) * tt)
            for s in range(4):
                mix_scr[rb, s * LANES:(s + 1) * LANES] = ys_scr[s, pl.ds(b, tt, stride=nb), :].astype(BF16)
            mix_scr[rb, D_SSM:D_SSM + D_ATTN] = (_rms_scale(ya_ref[b].astype(F32)) * ga_ref[...]).astype(BF16)
        h = x_ref[...].reshape(rows, D_MODEL) + jnp.dot(mix_scr[...], wout_ref[...], preferred_element_type=F32)
        y_ref[...] = (_rms_scale(h) * gf_ref[...]).reshape(nb, tt, D_MODEL)
        return
    for s in range(4):
        if prompt:
            for b in range(nb):
                ys_ref[s, b] = ys_scr[s, pl.ds(b, tt, stride=nb), :].astype(ys_ref.dtype)
        else:
            for t in range(tt):
                ys_ref[s, pl.ds(t, nb, stride=tt), :] = ys_scr[s, t * nb:(t + 1) * nb, :]


def _ssm(xz, h0, lam, wb, cc, d_vec, w_glu_bf, b_glu, norm_g, *, nb, t_total, tt, out_proj=None, guest=None):
    rows = nb * tt
    const = lambda shape: pl.BlockSpec(shape, lambda i: (0,) * len(shape))
    seq_block = lambda width: pl.BlockSpec((nb, tt, width), lambda i: (0, i, 0))
    args = [xz, h0, lam, wb, cc, d_vec.reshape(1, D_SSM), w_glu_bf, b_glu.reshape(1, 2 * D_SSM),
            norm_g.reshape(1, D_SSM)]
    in_specs = [None, const((nb, 2 * D_STATE)), const((SUBLANES, 2 * D_STATE)),
                const((N_CH_TILES, LANES, 2 * STATES_PER_TILE)), const((N_CH_TILES, 2 * STATES_PER_TILE, LANES)),
                const((1, D_SSM)),
                const((D_SSM, 2 * D_SSM)), const((1, 2 * D_SSM)), const((1, D_SSM))]
    scratch = [pltpu.VMEM((8, rows, LANES), F32), pltpu.VMEM((rows, 2 * D_STATE), F32),
               pltpu.VMEM((4, rows, LANES), F32)]
    if t_total > tt:
        args[0] = xz.reshape(8, nb, t_total, LANES)
        in_specs[0] = pl.BlockSpec((8, nb, tt, LANES), lambda i: (0, 0, i, 0))
        out_spec = pl.BlockSpec((4, nb, tt, LANES), lambda i: (0, 0, i, 0))
        out_shape = jax.ShapeDtypeStruct((4, nb, t_total, LANES), BF16)
    else:
        in_specs[0] = const((8, rows, LANES))
        out_spec = const((4, rows, LANES))
        out_shape = jax.ShapeDtypeStruct((4, rows, LANES), F32)
    out_specs = [out_spec, const((nb, 2 * D_STATE))]
    out_shapes = [out_shape, jax.ShapeDtypeStruct((nb, 2 * D_STATE), F32)]
    kernel_kwargs = dict(nb=nb, tt=tt)
    if out_proj is not None:
        x, ya, norm_attn_g, w_out_bf, final_g = out_proj
        args += [x, ya, norm_attn_g.reshape(1, D_ATTN), w_out_bf, final_g.reshape(1, D_MODEL)]
        in_specs += [seq_block(D_MODEL), seq_block(D_ATTN), const((1, D_ATTN)), const((D_MODEL, D_MODEL)),
                     const((1, D_MODEL))]
        out_specs[0] = seq_block(D_MODEL)
        out_shapes[0] = jax.ShapeDtypeStruct((nb, t_total, D_MODEL), F32)
        scratch.append(pltpu.VMEM((rows, D_MODEL), BF16))
    if guest is not None:
        stok, cache_kt, cache_vt, first_seq, guest_tt = guest
        tok, cache, guest_out, half = _sample_specs(first_seq, 1, 0, guest_tt)
        args += [*stok, cache_kt, cache_vt]
        in_specs += [tok] * 4 + [cache] * 2
        out_specs.append(guest_out)
        out_shapes.append(jax.ShapeDtypeStruct((t_total // tt, SUBLANES, D_ATTN), F32))
        kernel_kwargs.update(guest_half=half, guest_tt=guest_tt)
    out, st, *guest_res = pl.pallas_call(
        functools.partial(_ssm_kernel, **kernel_kwargs),
        grid=(t_total // tt,),
        in_specs=in_specs,
        out_specs=out_specs,
        out_shape=out_shapes,
        scratch_shapes=scratch,
        compiler_params=pltpu.CompilerParams(dimension_semantics=("arbitrary",), vmem_limit_bytes=VMEM_LIMIT),
        name="ssm_prompt" if t_total > tt else "ssm_sample",
    )(*args)
    return ((out if out_proj is not None else out.reshape(4, nb * t_total, LANES)), st, *guest_res)


PART0_SIXTEENTHS = 14
GROUP_BLOCKS = 2


def _band_scores(qs, ks, biases):
    low = lax.broadcasted_iota(jnp.int32, (BAND, LANES), 1) < HEAD_DIM
    nt = (((1,), (1,)), ((), ()))
    stacked = []
    for q in qs:
        stacked.append(jnp.concatenate([jnp.where(low, q, 0.0), jnp.where(low, 0.0, q)], axis=0).astype(BF16))
    return [lax.dot_general(q, k.astype(BF16), nt, preferred_element_type=F32) + bias
            for q, k, bias in zip(stacked, ks, biases)]


def _band_softmax_pv(ss, vs):
    low = lax.broadcasted_iota(jnp.int32, (BAND, LANES), 1) < HEAD_DIM
    ms = [jnp.max(s, axis=-1, keepdims=True) for s in ss]
    ps = [jnp.exp(s - m).astype(BF16) for s, m in zip(ss, ms)]
    outs = [jnp.dot(p, jnp.concatenate([v.astype(BF16), jnp.ones(v.shape, BF16)], axis=1),
                    preferred_element_type=F32) for p, v in zip(ps, vs)]
    wide = lambda a: jnp.broadcast_to(a, (BAND, a.shape[1] if a.shape[1] > 1 else LANES))
    pick = lambda a: jnp.where(low, wide(a[0:BAND]), wide(a[BAND:2 * BAND]))
    return [(pick(o[:, 0:LANES]), pick(m), pick(o[:, LANES:2 * LANES])) for o, m in zip(outs, ms)]


def _prompt_attn_part(q_ref, k_ref, v_ref, gate_ref, o_ref, o_scr, m_scr, l_scr, band_scr, causal_scr, part):
    t_total = q_ref.shape[0]

    blocks = []
    for pidx, (window, dil) in enumerate(DILATED_PATTERNS):
        nblk = t_total // dil // BAND
        rows = (lambda r0, n, dil=dil: pl.ds(r0, n, stride=dil) if dil > 1 else pl.ds(r0, n))
        for rho in range(dil):
            blocks.append((pidx, rows(rho, BAND), rows(rho, BAND), causal_scr))
            blocks += [(pidx, rows(rho + dil * BAND * i, BAND), rows(rho + dil * BAND * (i - 1), 2 * BAND), band_scr)
                       for i in range(1, nblk)]

    def scores(group):
        return _band_scores([q_ref[qr, :] for _, qr, _, _ in group], [k_ref[kr, :] for _, _, kr, _ in group],
                            [bias[...] for _, _, _, bias in group])

    def finish(group, ss):
        res = _band_softmax_pv(ss, [v_ref[kr, :] for _, _, kr, _ in group])
        for (pidx, qr, _, _), (o, m, l) in zip(group, res):
            o_scr[pidx, qr, :] = o
            m_scr[pidx, qr, :] = m
            l_scr[pidx, qr, :] = l

    groups = [blocks[g:g + GROUP_BLOCKS] for g in range(0, len(blocks), GROUP_BLOCKS)]
    half = (len(groups) * PART0_SIXTEENTHS) // 16
    pending = None
    for group in (groups[:half] if part == 0 else groups[half:]):
        ss = scores(group)
        if pending is not None:
            finish(*pending)
        pending = (group, ss)
    finish(*pending)

    if part == 1:
        chunk = 256
        for c in range(t_total // chunk):
            rows = pl.ds(c * chunk, chunk)
            ms = [m_scr[p, rows, :] for p in range(3)]
            big = jnp.maximum(jnp.maximum(ms[0], ms[1]), ms[2])
            ws = [jnp.exp(m - big) for m in ms]
            num = ws[0] * o_scr[0, rows, :] + ws[1] * o_scr[1, rows, :] + ws[2] * o_scr[2, rows, :]
            den = ws[0] * l_scr[0, rows, :] + ws[1] * l_scr[1, rows, :] + ws[2] * l_scr[2, rows, :]
            o_ref[rows, :] = ((num / den) * gate_ref[rows, :].astype(F32)).astype(o_ref.dtype)


NEAR = 512


def _sample_attn_seqs(sources, tt):
    assert 2 * tt == SUBLANES
    nt = (((1,), (1,)), ((), ()))
    row8 = lax.broadcasted_iota(jnp.int32, (SUBLANES, LANES), 0)
    lane8 = lax.broadcasted_iota(jnp.int32, (SUBLANES, LANES), 1)
    top = row8 < tt
    own_head = top == (lane8 < HEAD_DIM)

    def dup(x, half):
        return jnp.where(row8 // tt == half, x, pltpu.roll(x, tt, axis=0))

    jq = lambda shape: lax.broadcasted_iota(jnp.int32, shape, 0) & (tt - 1)
    d_full = WINDOW_MAX + jq((SUBLANES, WINDOW_MAX)) - lax.broadcasted_iota(jnp.int32, (SUBLANES, WINDOW_MAX), 1)
    d_near = NEAR + jq((SUBLANES, NEAR)) - lax.broadcasted_iota(jnp.int32, (SUBLANES, NEAR), 1)
    d_new = jq((SUBLANES, SUBLANES)) - lax.broadcasted_iota(jnp.int32, (SUBLANES, SUBLANES), 1)
    new_key = lax.broadcasted_iota(jnp.int32, (SUBLANES, SUBLANES), 1) < tt
    ok_main, ok_new = [], []
    for window, dil in DILATED_PATTERNS:
        d = d_near if window <= NEAR else d_full
        ok_main.append(((d & (dil - 1)) == 0) & (d <= window))
        ok_new.append(new_key & (d_new >= 0) & ((d_new & (dil - 1)) == 0))

    insts = [(slice(pr * LANES, (pr + 1) * LANES), src) for src in sources for pr in range(D_ATTN // LANES)]

    scored = []
    for lanes, (q_ref, kn_ref, _, _, kc_ref, _, _, half) in insts:
        q = jnp.where(own_head, dup(q_ref[:, lanes], half), 0.0).astype(BF16)
        kn = dup(kn_ref[:, lanes], half).astype(BF16)
        s_main = jnp.dot(q, kc_ref[lanes, :].astype(BF16), preferred_element_type=F32)
        s_new = lax.dot_general(q, kn, nt, preferred_element_type=F32)
        scored.append((s_main, s_new))

    weighted = []
    for s_main, s_new in scored:
        stats = []
        for (window, dil), okm, okn in zip(DILATED_PATTERNS, ok_main, ok_new):
            sm = jnp.where(okm, s_main[:, WINDOW_MAX - NEAR:] if window <= NEAR else s_main, NEG_INF)
            sn = jnp.where(okn, s_new, NEG_INF)
            m = jnp.maximum(jnp.max(sm, axis=-1, keepdims=True), jnp.max(sn, axis=-1, keepdims=True))
            pm = jnp.exp(sm - m)
            pn = jnp.exp(sn - m)
            l = jnp.sum(pm, axis=-1, keepdims=True) + jnp.sum(pn, axis=-1, keepdims=True)
            stats.append((m, l, pm, pn))
        big = functools.reduce(jnp.maximum, [s[0] for s in stats])
        ws = [jnp.exp(s[0] - big) for s in stats]
        den = functools.reduce(jnp.add, [w * s[1] for w, s in zip(ws, stats)])
        p_new = functools.reduce(jnp.add, [w * s[3] for w, s in zip(ws, stats)])
        p_near = functools.reduce(jnp.add, [w * s[2] for w, s in zip(ws, stats) if s[2].shape[1] == NEAR])
        p_full = functools.reduce(jnp.add, [w * s[2] for w, s in zip(ws, stats) if s[2].shape[1] != NEAR])
        p_main = jnp.concatenate([p_full[:, 0:WINDOW_MAX - NEAR], p_full[:, WINDOW_MAX - NEAR:] + p_near], axis=1)
        weighted.append((p_main.astype(BF16), p_new.astype(BF16).astype(F32), den))

    results = []
    for (lanes, (_, _, vn_ref, gate_ref, _, vc_ref, _, half)), (p_main, p_new, den) in zip(insts, weighted):
        acc = lax.dot_general(p_main, vc_ref[lanes, :].astype(BF16), nt, preferred_element_type=F32)
        vn = dup(vn_ref[:, lanes], half).astype(BF16).astype(F32)
        for j in range(tt):
            acc = acc + p_new[:, j:j + 1] * vn[j:j + 1, :]
        acc = acc / den
        merged = jnp.where(lane8 < HEAD_DIM, acc, pltpu.roll(acc, tt, axis=0))
        results.append(merged * dup(gate_ref[:, lanes], half))

    n_pairs = D_ATTN // LANES
    for s, src in enumerate(sources):
        src[6][...] = jnp.concatenate(results[s * n_pairs:(s + 1) * n_pairs], axis=1)


def _sample_specs(first_seq, steps_per_seq, lag, tt):
    local = lambda i: jnp.maximum(i - lag, 0) // steps_per_seq
    seq = lambda i: first_seq + local(i)
    tok = pl.BlockSpec((SUBLANES, D_ATTN), lambda i: (seq(i) * tt // SUBLANES, 0))
    cache = pl.BlockSpec((None, D_ATTN, WINDOW_MAX), lambda i: (seq(i), 0, 0))
    out = pl.BlockSpec((None, SUBLANES, D_ATTN), lambda i: (local(i), 0, 0))
    half = lambda i: seq(i) % (SUBLANES // tt)
    return tok, cache, out, half


def _attention_kernel(*refs, tt, halves):
    tok_a, tok_b = refs[0:4], refs[4:8]
    (kc_a, vc_a, kc_b, vc_b, pq_ref, pk_ref, pv_ref, pgate_ref,
     so_a, so_b, po_ref, o_scr, m_scr, l_scr, band_scr, causal_scr) = refs[8:]
    i = pl.program_id(0)

    @pl.when(i == 0)
    def _():
        for scr, kwin in ((band_scr, 2 * BAND), (causal_scr, BAND)):
            qq = lax.broadcasted_iota(jnp.int32, (2 * BAND, kwin), 0) & (BAND - 1)
            kk = lax.broadcasted_iota(jnp.int32, (2 * BAND, kwin), 1)
            valid = ((kk >= qq) & (kk <= qq + BAND)) if kwin > BAND else (kk <= qq)
            scr[...] = jnp.where(valid, 0.0, NEG_INF)

    src_a = (*tok_a, kc_a, vc_a, so_a, halves[0](i))
    src_b = (*tok_b, kc_b, vc_b, so_b, halves[1](i))
    for part in range(2):
        @pl.when(i % 2 == part)
        def _(part=part):
            _sample_attn_seqs([src_a, src_b] if part == 1 else [src_a], tt)
            _prompt_attn_part(pq_ref, pk_ref, pv_ref, pgate_ref, po_ref, o_scr, m_scr, l_scr,
                              band_scr, causal_scr, part)


def _attention(stok, cache_kt, cache_vt, pq, pk, pv, pgate, *, n_seqs, tt, batch, t_total):
    pairs = D_ATTN // LANES
    steps = 2 * batch * pairs
    n_a, n_b = steps, steps // 2
    assert n_seqs == n_a + n_b
    tok_a, cache_a, out_a, half_a = _sample_specs(0, 1, 0, tt)
    tok_b, cache_b, out_b, half_b = _sample_specs(n_a, 2, 1, tt)
    shp = (batch, t_total, D_ATTN)
    unit = pl.BlockSpec((None, t_total, LANES), lambda i: (i // (2 * pairs), 0, (i // 2) % pairs))
    so_a, so_b, po = pl.pallas_call(
        functools.partial(_attention_kernel, tt=tt, halves=(half_a, half_b)),
        grid=(steps,),
        in_specs=[tok_a] * 4 + [tok_b] * 4 + [cache_a, cache_a, cache_b, cache_b] + [unit] * 4,
        out_specs=[out_a, out_b, unit],
        out_shape=[jax.ShapeDtypeStruct((n_a, SUBLANES, D_ATTN), F32),
                   jax.ShapeDtypeStruct((n_b, SUBLANES, D_ATTN), F32), jax.ShapeDtypeStruct(shp, BF16)],
        scratch_shapes=[pltpu.VMEM((3, t_total, LANES), F32)] * 3
                       + [pltpu.VMEM((2 * BAND, 2 * BAND), F32), pltpu.VMEM((2 * BAND, BAND), F32)],
        compiler_params=pltpu.CompilerParams(dimension_semantics=("arbitrary",), vmem_limit_bytes=VMEM_LIMIT),
        name="attention",
    )(*stok, *stok, cache_kt, cache_vt, cache_kt, cache_vt,
      pq.reshape(shp), pk.reshape(shp), pv.reshape(shp), pgate.reshape(shp))
    return jnp.concatenate([so_a, so_b], axis=0), po.reshape(batch * t_total, D_ATTN)


def _out_proj_kernel(x_ref, ys_ref, ya_ref, ga_ref, w_ref, gf_ref, y_ref):
    ys = jnp.concatenate([ys_ref[s] for s in range(4)], axis=1).astype(BF16)
    ya = (_rms_scale(ya_ref[...]) * ga_ref[...]).astype(BF16)
    mix = jnp.concatenate([ys, ya], axis=1)
    h = x_ref[...] + jnp.dot(mix, w_ref[...], preferred_element_type=F32)
    y_ref[...] = _rms_scale(h) * gf_ref[...]


def _out_proj(x, ys, ya, norm_attn_g, w_out_bf, final_g):
    rows = x.shape[0]
    row_spec = lambda width: pl.BlockSpec((ROW_TILE, width), lambda i: (i, 0))
    const = lambda shape: pl.BlockSpec(shape, lambda i: (0,) * len(shape))
    return pl.pallas_call(
        _out_proj_kernel,
        grid=(rows // ROW_TILE,),
        in_specs=[row_spec(D_MODEL), pl.BlockSpec((4, ROW_TILE, LANES), lambda i: (0, i, 0)), row_spec(D_ATTN),
                  const((1, D_ATTN)), const((D_MODEL, D_MODEL)), const((1, D_MODEL))],
        out_specs=row_spec(D_MODEL),
        out_shape=jax.ShapeDtypeStruct((rows, D_MODEL), F32),
        compiler_params=pltpu.CompilerParams(dimension_semantics=("parallel",), vmem_limit_bytes=VMEM_LIMIT),
        name="out_proj",
    )(x, ys, ya, norm_attn_g.reshape(1, D_ATTN), w_out_bf, final_g.reshape(1, D_MODEL))


def kernel(x_prompt, x_sample, cache_k, cache_v, state_ssm_re, state_ssm_im, norm_in_g, w_in,
           ssm_A_re, ssm_A_im, ssm_log_dt, ssm_B_re, ssm_B_im, ssm_C_re, ssm_C_im, ssm_D,
           w_glu, b_glu, norm_ssm_g, norm_attn_g, w_out, final_norm_g):
    depth = w_in.shape[0]
    assert depth == 1
    batch, seq, _ = x_prompt.shape
    dec_batch, dec_seq, _ = x_sample.shape
    assert batch == SUBLANES and seq == WINDOW_MAX and seq % SSM_STEPS == 0

    lam, bb_re, bb_im = _ssm_prep(ssm_A_re[0], ssm_A_im[0], ssm_log_dt[0], ssm_B_re[0], ssm_B_im[0])
    wb, cc = _block_diag_weights(bb_re, bb_im, ssm_C_re[0], ssm_C_im[0])
    w_glu_bf = w_glu[0].astype(BF16)
    w_out_bf = w_out[0].astype(BF16)

    ssm = functools.partial(_ssm, lam=lam, wb=wb, cc=cc, d_vec=ssm_D[0], w_glu_bf=w_glu_bf, b_glu=b_glu[0],
                            norm_g=norm_ssm_g[0])

    xp = x_prompt.reshape(batch * seq, D_MODEL)
    xzp, qp, kp, vp, gp, kpt, vpt = _in_proj(xp, norm_in_g[0], w_in[0], seq_len=seq)
    xs = x_sample.reshape(dec_batch * dec_seq, D_MODEL)
    xzs, qs, ksm, vsm, gs = _in_proj(xs, norm_in_g[0], w_in[0])

    chan_major = lambda c: c[0].transpose(0, 2, 3, 1).reshape(dec_batch, D_ATTN, WINDOW_MAX)
    cache_kt, cache_vt = chan_major(cache_k), chan_major(cache_v)
    stok = (qs, ksm, vsm, gs)
    n_guest = seq // SSM_STEPS
    yas_main, yap = _attention(stok, cache_kt, cache_vt, qp, kp, vp, gp,
                               n_seqs=dec_batch - n_guest, tt=dec_seq, batch=batch, t_total=seq)

    yp, stp, yas_guest = ssm(xzp, jnp.zeros((batch, 2 * D_STATE), F32), nb=batch, t_total=seq, tt=SSM_STEPS,
                             out_proj=(x_prompt, yap.reshape(batch, seq, D_ATTN), norm_attn_g[0], w_out_bf,
                                       final_norm_g),
                             guest=(stok, cache_kt, cache_vt, dec_batch - n_guest, dec_seq))
    yas = jnp.concatenate([yas_main, yas_guest], axis=0)[:, 0:dec_seq].reshape(dec_batch * dec_seq, D_ATTN)
    h0 = jnp.concatenate([state_ssm_re[0].reshape(dec_batch, D_STATE),
                          state_ssm_im[0].reshape(dec_batch, D_STATE)], axis=1)
    yss, sts = ssm(xzs, h0, nb=dec_batch, t_total=dec_seq, tt=dec_seq)
    ysm = _out_proj(xs, yss, yas, norm_attn_g[0], w_out_bf, final_norm_g)

    heads = lambda a, n, t: a.reshape(1, n, t, N_HEADS, HEAD_DIM)
    heads_t = lambda a, n, t: a.reshape(n, N_HEADS, HEAD_DIM, t).transpose(0, 3, 1, 2)[None]
    state = lambda s, n, part: s[:, part * D_STATE:(part + 1) * D_STATE].reshape(1, n, N_SSM_GROUPS, SSM_STATE)
    return (yp.reshape(batch, seq, D_MODEL), ysm.reshape(dec_batch, dec_seq, D_MODEL),
            heads_t(kpt, batch, seq), heads_t(vpt, batch, seq), state(stp, batch, 0), state(stp, batch, 1),
            heads(ksm, dec_batch, dec_seq), heads(vsm, dec_batch, dec_seq),
            state(sts, dec_batch, 0), state(sts, dec_batch, 1))
```

```python
import functools

import jax
import jax.numpy as jnp
from jax import lax
from jax.experimental import pallas as pl
from jax.experimental.pallas import tpu as pltpu

F32 = jnp.float32
BF16 = jnp.bfloat16

D_MODEL = 1024
D_SSM = 512
D_ATTN = 512
SSM_GROUP = 16
N_SSM_GROUPS = D_SSM // SSM_GROUP
SSM_STATE = 64
D_STATE = N_SSM_GROUPS * SSM_STATE
PROJ_CHUNK = 512
HEAD_DIM = 64
N_HEADS = D_ATTN // HEAD_DIM
D_IN_PROJ = 2 * D_SSM + 4 * D_ATTN
RMS_EPS = 1e-6
DILATED_PATTERNS = ((128, 1), (512, 4), (2048, 16))
BAND = 128
WINDOW_MAX = 2048
LANES = 128
SUBLANES = 8
N_CH_TILES = D_SSM // LANES
GROUPS_PER_TILE = LANES // SSM_GROUP
STATES_PER_TILE = GROUPS_PER_TILE * SSM_STATE
VMEM_LIMIT = 56 * 1024 * 1024
ROW_TILE = 512
IN_PROJ_ROW_TILE = 1024
SSM_STEPS = 64
SSM_SUB_STEPS = 32
GUEST_STAGE = 3
NEG_INF = float("-inf")


def _sigmoid(x):
    return 1.0 / (1.0 + jnp.exp(-x))


def _rms_scale(x):
    return x * lax.rsqrt(jnp.mean(x * x, axis=-1, keepdims=True) + RMS_EPS)


def _ssm_prep_kernel(are_ref, aim_ref, ldt_ref, bre_ref, bim_ref, lam_ref, bbre_ref, bbim_ref):
    a_re = are_ref[...]
    a_im = aim_ref[...]
    dt = jnp.exp(ldt_ref[...])
    mag = jnp.exp(a_re * dt)
    ang = a_im * dt
    lam_re = mag * jnp.cos(ang)
    lam_im = mag * jnp.sin(ang)
    den = a_re * a_re + a_im * a_im
    f_re = ((lam_re - 1.0) * a_re + lam_im * a_im) / den
    f_im = (lam_im * a_re - (lam_re - 1.0) * a_im) / den
    lam_ref[:, 0:D_STATE] = jnp.broadcast_to(lam_re, (SUBLANES, D_STATE))
    lam_ref[:, D_STATE:2 * D_STATE] = jnp.broadcast_to(lam_im, (SUBLANES, D_STATE))
    b_re = bre_ref[...]
    b_im = bim_ref[...]
    bbre_ref[...] = f_re * b_re - f_im * b_im
    bbim_ref[...] = f_re * b_im + f_im * b_re


def _ssm_prep(a_re, a_im, log_dt, b_re, b_im):
    flat = lambda a: a.reshape(1, D_STATE)
    ldt = jnp.repeat(log_dt, SSM_STATE).reshape(1, D_STATE)
    bt = lambda b: b.transpose(2, 0, 1).reshape(SSM_GROUP, D_STATE)
    return pl.pallas_call(
        _ssm_prep_kernel,
        out_shape=(jax.ShapeDtypeStruct((SUBLANES, 2 * D_STATE), F32),
                   jax.ShapeDtypeStruct((SSM_GROUP, D_STATE), F32),
                   jax.ShapeDtypeStruct((SSM_GROUP, D_STATE), F32)),
        name="ssm_prep",
    )(flat(a_re), flat(a_im), ldt, bt(b_re), bt(b_im))


def _block_diag_weights(bb_re, bb_im, c_re, c_im):
    eye = jnp.eye(GROUPS_PER_TILE, dtype=F32)
    def place_b(bb):
        b4 = bb.reshape(SSM_GROUP, N_CH_TILES, GROUPS_PER_TILE, SSM_STATE).transpose(1, 2, 0, 3)
        return jnp.einsum("kgcp,gh->kgchp", b4, eye).reshape(N_CH_TILES, LANES, STATES_PER_TILE)
    def place_c(c):
        c4 = c.reshape(N_CH_TILES, GROUPS_PER_TILE, SSM_GROUP, SSM_STATE)
        return jnp.einsum("kgcp,gh->khpgc", c4, eye).reshape(N_CH_TILES, STATES_PER_TILE, LANES)
    wb = jnp.concatenate([place_b(bb_re), place_b(bb_im)], axis=2).astype(BF16)
    cc = jnp.concatenate([place_c(c_re), -place_c(c_im)], axis=1).astype(BF16)
    return wb, cc


def _in_proj_kernel(x_ref, g_ref, w_ref, xz_ref, q_ref, k_ref, v_ref, za_ref, *t_refs):
    xn = (_rms_scale(x_ref[...]) * g_ref[...]).astype(BF16)
    outs = (None, None, q_ref, k_ref, v_ref, za_ref)
    for j in range(D_IN_PROJ // PROJ_CHUNK):
        w = w_ref[:, j * PROJ_CHUNK:(j + 1) * PROJ_CHUNK].astype(BF16)
        r = jnp.dot(xn, w, preferred_element_type=F32)
        if j in (1, 5):
            r = r * _sigmoid(r)
        if j == 2:
            r = r * (HEAD_DIM ** -0.5)
        if j < 2:
            for s in range(4):
                xz_ref[j * 4 + s] = r[:, s * LANES:(s + 1) * LANES]
        else:
            outs[j][...] = r.astype(outs[j].dtype)
            if t_refs and j in (3, 4):
                t_refs[j - 3][...] = r.T


def _in_proj(x, norm_g, w_in, *, seq_len=None):
    rows = x.shape[0]
    tile = min(rows, IN_PROJ_ROW_TILE)
    row_spec = lambda width: pl.BlockSpec((tile, width), lambda i: (i, 0))
    out_specs = [pl.BlockSpec((8, tile, LANES), lambda i: (0, i, 0))] + [row_spec(512)] * 4
    out_shape = ([jax.ShapeDtypeStruct((8, rows, LANES), F32)] + [jax.ShapeDtypeStruct((rows, 512), F32)] * 3
                 + [jax.ShapeDtypeStruct((rows, 512), F32 if seq_len is None else BF16)])
    if seq_len is not None:
        per_seq = seq_len // tile
        out_specs += [pl.BlockSpec((None, 512, tile), lambda i: (i // per_seq, 0, i % per_seq))] * 2
        out_shape += [jax.ShapeDtypeStruct((rows // seq_len, 512, seq_len), F32)] * 2
    return pl.pallas_call(
        _in_proj_kernel,
        grid=(rows // tile,),
        in_specs=[row_spec(D_MODEL),
                  pl.BlockSpec((1, D_MODEL), lambda i: (0, 0)),
                  pl.BlockSpec((D_MODEL, D_IN_PROJ), lambda i: (0, 0), pipeline_mode=pl.Buffered(1))],
        out_specs=out_specs,
        out_shape=out_shape,
        compiler_params=pltpu.CompilerParams(dimension_semantics=("parallel",), vmem_limit_bytes=VMEM_LIMIT),
        name="in_proj",
    )(x, norm_g.reshape(1, D_MODEL), w_in)


def _in_proj_cols_kernel(x_ref, g_ref, w_ref, xz_ref, q_ref, k_ref, v_ref, za_ref, xn_scr):
    j = pl.program_id(0)

    @pl.when(j == 0)
    def _():
        xn_scr[...] = (_rms_scale(x_ref[...]) * g_ref[...]).astype(BF16)

    r = jnp.dot(xn_scr[...], w_ref[...].astype(BF16), preferred_element_type=F32)
    silu = r * _sigmoid(r)
    slabs = lambda a: [a[:, s * LANES:(s + 1) * LANES] for s in range(4)]
    writes = ((xz_ref, slabs(r), 0), (xz_ref, slabs(silu), 4), (q_ref, r * (HEAD_DIM ** -0.5), None),
              (k_ref, r, None), (v_ref, r, None), (za_ref, silu, None))
    for col, (ref, val, base) in enumerate(writes):
        @pl.when(j == col)
        def _(ref=ref, val=val, base=base):
            if base is None:
                ref[...] = val
            else:
                for s in range(4):
                    ref[base + s] = val[s]


def _in_proj_small(x, norm_g, w_in):
    rows = x.shape[0]
    full = lambda shape: pl.BlockSpec(shape, lambda j: (0,) * len(shape))
    return pl.pallas_call(
        _in_proj_cols_kernel,
        grid=(D_IN_PROJ // PROJ_CHUNK,),
        in_specs=[full((rows, D_MODEL)), full((1, D_MODEL)), pl.BlockSpec((D_MODEL, PROJ_CHUNK), lambda j: (0, j))],
        out_specs=[full((8, rows, LANES))] + [full((rows, PROJ_CHUNK))] * 4,
        out_shape=[jax.ShapeDtypeStruct((8, rows, LANES), F32)] + [jax.ShapeDtypeStruct((rows, PROJ_CHUNK), F32)] * 4,
        scratch_shapes=[pltpu.VMEM((rows, D_MODEL), BF16)],
        compiler_params=pltpu.CompilerParams(dimension_semantics=("arbitrary",), vmem_limit_bytes=VMEM_LIMIT),
        name="in_proj_sample",
    )(x, norm_g.reshape(1, D_MODEL), w_in)


def _ssm_kernel(xz_ref, h0_ref, lam_ref, wb_ref, cc_ref, d_ref, wglu_ref, bglu_ref, g_ref, *rest, nb, tt,
                guest_half=None, guest_tt=None):
    rows = nb * tt
    prompt = xz_ref.ndim == 4
    with_out_proj = len(rest) > 5
    if guest_half is not None:
        (x_ref, ya_ref, ga_ref, wout_ref, gf_ref, *guest_in, y_ref, st_ref, guest_out,
         u_scr, bu_scr, ys_scr, mix_scr) = rest
        guest_src = (*guest_in, guest_out, guest_half(pl.program_id(0)))
    elif with_out_proj:
        x_ref, ya_ref, ga_ref, wout_ref, gf_ref, y_ref, st_ref, u_scr, bu_scr, ys_scr, mix_scr = rest
    else:
        ys_ref, st_ref, u_scr, bu_scr, ys_scr = rest

    @pl.when(pl.program_id(0) == 0)
    def _():
        st_ref[...] = h0_ref[...]

    for s in range(8):
        if prompt:
            for b in range(nb):
                u_scr[s, pl.ds(b, tt, stride=nb), :] = xz_ref[s, b]
        else:
            for t in range(tt):
                u_scr[s, t * nb:(t + 1) * nb, :] = xz_ref[s, pl.ds(t, nb, stride=tt), :]

    sub = min(tt, SSM_SUB_STEPS)
    sub_rows = [slice(j * sub * nb, (j + 1) * sub * nb) for j in range(tt // sub)]

    def expand(rs):
        w = STATES_PER_TILE
        for kt in range(N_CH_TILES):
            r = jnp.dot(u_scr[kt, rs, :].astype(BF16), wb_ref[kt], preferred_element_type=F32)
            bu_scr[rs, kt * w:(kt + 1) * w] = r[:, 0:w]
            bu_scr[rs, D_STATE + kt * w:D_STATE + (kt + 1) * w] = r[:, w:2 * w]

    width = (SUBLANES * LANES * 8) // nb if nb <= 64 else LANES
    chunks = [(slice(c * width, (c + 1) * width), slice(D_STATE + c * width, D_STATE + (c + 1) * width))
              for c in range(D_STATE // width)]

    def recur(j):
        for re_l, im_l in chunks:
            if nb == SUBLANES:
                lam_re, lam_im = lam_ref[:, re_l], lam_ref[:, im_l]
            else:
                lam_re = jnp.broadcast_to(lam_ref[0:1, re_l], (nb, width))
                lam_im = jnp.broadcast_to(lam_ref[0:1, im_l], (nb, width))
            s_re, s_im = st_ref[:, re_l], st_ref[:, im_l]
            for t in range(j * sub, (j + 1) * sub):
                tr = slice(t * nb, (t + 1) * nb)
                s_re, s_im = (lam_re * s_re - lam_im * s_im + bu_scr[tr, re_l],
                              lam_re * s_im + lam_im * s_re + bu_scr[tr, im_l])
                bu_scr[tr, re_l] = s_re
                bu_scr[tr, im_l] = s_im
            st_ref[:, re_l] = s_re
            st_ref[:, im_l] = s_im

    def project(rs):
        ys = []
        w = STATES_PER_TILE
        for kt in range(N_CH_TILES):
            xc = jnp.concatenate([bu_scr[rs, kt * w:(kt + 1) * w],
                                  bu_scr[rs, D_STATE + kt * w:D_STATE + (kt + 1) * w]], axis=1).astype(BF16)
            y = jnp.dot(xc, cc_ref[kt], preferred_element_type=F32)
            ys.append(y + d_ref[:, kt * LANES:(kt + 1) * LANES] * u_scr[kt, rs, :])
        y = jnp.concatenate(ys, axis=1)
        g = jax.nn.gelu(y)
        gl = jnp.dot(g.astype(BF16), wglu_ref[...], preferred_element_type=F32) + bglu_ref[...]
        gate = jnp.concatenate([u_scr[4 + s, rs, :] for s in range(4)], axis=1)
        y = gl[:, 0:D_SSM] * _sigmoid(gl[:, D_SSM:2 * D_SSM]) * gate
        y = _rms_scale(y) * g_ref[...]
        for s in range(4):
            ys_scr[s, rs, :] = y[:, s * LANES:(s + 1) * LANES]

    n_sub = len(sub_rows)
    for j in range(n_sub + 2):
        if j == GUEST_STAGE and guest_half is not None:
            _sample_attn_seqs([guest_src], guest_tt)
        if j < n_sub:
            expand(sub_rows[j])
        if 1 <= j <= n_sub:
            recur(j - 1)
        if j >= 2:
            project(sub_rows[j - 2])

    if with_out_proj:
        for b in range(nb):
            rb = slice(b * tt, (b + 1) * tt)
            for s in range(4):
                mix_scr[rb, s * LANES:(s + 1) * LANES] = ys_scr[s, pl.ds(b, tt, stride=nb), :].astype(BF16)
            mix_scr[rb, D_SSM:D_SSM + D_ATTN] = (_rms_scale(ya_ref[b].astype(F32)) * ga_ref[...]).astype(BF16)
        h = x_ref[...].reshape(rows, D_MODEL) + jnp.dot(mix_scr[...], wout_ref[...], preferred_element_type=F32)
        y_ref[...] = (_rms_scale(h) * gf_ref[...]).reshape(nb, tt, D_MODEL)
        return
    for s in range(4):
        if prompt:
            for b in range(nb):
                ys_ref[s, b] = ys_scr[s, pl.ds(b, tt, stride=nb), :].astype(ys_ref.dtype)
        else:
            for t in range(tt):
                ys_ref[s, pl.ds(t, nb, stride=tt), :] = ys_scr[s, t * nb:(t + 1) * nb, :]


def _ssm(xz, h0, lam, wb, cc, d_vec, w_glu_bf, b_glu, norm_g, *, nb, t_total, tt, out_proj=None, guest=None):
    rows = nb * tt
    const = lambda shape: pl.BlockSpec(shape, lambda i: (0,) * len(shape))
    seq_block = lambda width: pl.BlockSpec((nb, tt, width), lambda i: (0, i, 0))
    args = [xz, h0, lam, wb, cc, d_vec.reshape(1, D_SSM), w_glu_bf, b_glu.reshape(1, 2 * D_SSM),
            norm_g.reshape(1, D_SSM)]
    in_specs = [None, const((nb, 2 * D_STATE)), const((SUBLANES, 2 * D_STATE)),
                const((N_CH_TILES, LANES, 2 * STATES_PER_TILE)), const((N_CH_TILES, 2 * STATES_PER_TILE, LANES)),
                const((1, D_SSM)),
                const((D_SSM, 2 * D_SSM)), const((1, 2 * D_SSM)), const((1, D_SSM))]
    scratch = [pltpu.VMEM((8, rows, LANES), F32), pltpu.VMEM((rows, 2 * D_STATE), F32),
               pltpu.VMEM((4, rows, LANES), F32)]
    if t_total > tt:
        args[0] = xz.reshape(8, nb, t_total, LANES)
        in_specs[0] = pl.BlockSpec((8, nb, tt, LANES), lambda i: (0, 0, i, 0))
        out_spec = pl.BlockSpec((4, nb, tt, LANES), lambda i: (0, 0, i, 0))
        out_shape = jax.ShapeDtypeStruct((4, nb, t_total, LANES), BF16)
    else:
        in_specs[0] = const((8, rows, LANES))
        out_spec = const((4, rows, LANES))
        out_shape = jax.ShapeDtypeStruct((4, rows, LANES), F32)
    out_specs = [out_spec, const((nb, 2 * D_STATE))]
    out_shapes = [out_shape, jax.ShapeDtypeStruct((nb, 2 * D_STATE), F32)]
    kernel_kwargs = dict(nb=nb, tt=tt)
    if out_proj is not None:
        x, ya, norm_attn_g, w_out_bf, final_g = out_proj
        args += [x, ya, norm_attn_g.reshape(1, D_ATTN), w_out_bf, final_g.reshape(1, D_MODEL)]
        in_specs += [seq_block(D_MODEL), seq_block(D_ATTN), const((1, D_ATTN)), const((D_MODEL, D_MODEL)),
                     const((1, D_MODEL))]
        out_specs[0] = seq_block(D_MODEL)
        out_shapes[0] = jax.ShapeDtypeStruct((nb, t_total, D_MODEL), F32)
        scratch.append(pltpu.VMEM((rows, D_MODEL), BF16))
    if guest is not None:
        stok, cache_kt, cache_vt, first_seq, guest_tt = guest
        tok, cache, guest_out, half = _sample_specs(first_seq, 1, 0, guest_tt)
        args += [*stok, cache_kt, cache_vt]
        in_specs += [tok] * 4 + [cache] * 2
        out_specs.append(guest_out)
        out_shapes.append(jax.ShapeDtypeStruct((t_total // tt, SUBLANES, D_ATTN), F32))
        kernel_kwargs.update(guest_half=half, guest_tt=guest_tt)
    out, st, *guest_res = pl.pallas_call(
        functools.partial(_ssm_kernel, **kernel_kwargs),
        grid=(t_total // tt,),
        in_specs=in_specs,
        out_specs=out_specs,
        out_shape=out_shapes,
        scratch_shapes=scratch,
        compiler_params=pltpu.CompilerParams(dimension_semantics=("arbitrary",), vmem_limit_bytes=VMEM_LIMIT),
        name="ssm_prompt" if t_total > tt else "ssm_sample",
    )(*args)
    return ((out if out_proj is not None else out.reshape(4, nb * t_total, LANES)), st, *guest_res)


PART0_SIXTEENTHS = 14
GROUP_BLOCKS = 2


def _band_scores(qs, ks, biases):
    low = lax.broadcasted_iota(jnp.int32, (BAND, LANES), 1) < HEAD_DIM
    nt = (((1,), (1,)), ((), ()))
    stacked = []
    for q in qs:
        stacked.append(jnp.concatenate([jnp.where(low, q, 0.0), jnp.where(low, 0.0, q)], axis=0).astype(BF16))
    return [lax.dot_general(q, k.astype(BF16), nt, preferred_element_type=F32) + bias
            for q, k, bias in zip(stacked, ks, biases)]


def _band_softmax_pv(ss, vs):
    low = lax.broadcasted_iota(jnp.int32, (BAND, LANES), 1) < HEAD_DIM
    ms = [jnp.max(s, axis=-1, keepdims=True) for s in ss]
    ps = [jnp.exp(s - m).astype(BF16) for s, m in zip(ss, ms)]
    outs = [jnp.dot(p, jnp.concatenate([v.astype(BF16), jnp.ones(v.shape, BF16)], axis=1),
                    preferred_element_type=F32) for p, v in zip(ps, vs)]
    wide = lambda a: jnp.broadcast_to(a, (BAND, a.shape[1] if a.shape[1] > 1 else LANES))
    pick = lambda a: jnp.where(low, wide(a[0:BAND]), wide(a[BAND:2 * BAND]))
    return [(pick(o[:, 0:LANES]), pick(m), pick(o[:, LANES:2 * LANES])) for o, m in zip(outs, ms)]


def _prompt_attn_part(q_ref, k_ref, v_ref, gate_ref, o_ref, o_scr, m_scr, l_scr, band_scr, causal_scr, part):
    t_total = q_ref.shape[0]

    blocks = []
    for pidx, (window, dil) in enumerate(DILATED_PATTERNS):
        nblk = t_total // dil // BAND
        rows = (lambda r0, n, dil=dil: pl.ds(r0, n, stride=dil) if dil > 1 else pl.ds(r0, n))
        for rho in range(dil):
            blocks.append((pidx, rows(rho, BAND), rows(rho, BAND), causal_scr))
            blocks += [(pidx, rows(rho + dil * BAND * i, BAND), rows(rho + dil * BAND * (i - 1), 2 * BAND), band_scr)
                       for i in range(1, nblk)]

    def scores(group):
        return _band_scores([q_ref[qr, :] for _, qr, _, _ in group], [k_ref[kr, :] for _, _, kr, _ in group],
                            [bias[...] for _, _, _, bias in group])

    def finish(group, ss):
        res = _band_softmax_pv(ss, [v_ref[kr, :] for _, _, kr, _ in group])
        for (pidx, qr, _, _), (o, m, l) in zip(group, res):
            o_scr[pidx, qr, :] = o
            m_scr[pidx, qr, :] = m
            l_scr[pidx, qr, :] = l

    groups = [blocks[g:g + GROUP_BLOCKS] for g in range(0, len(blocks), GROUP_BLOCKS)]
    half = (len(groups) * PART0_SIXTEENTHS) // 16
    pending = None
    for group in (groups[:half] if part == 0 else groups[half:]):
        ss = scores(group)
        if pending is not None:
            finish(*pending)
        pending = (group, ss)
    finish(*pending)

    if part == 1:
        chunk = 256
        for c in range(t_total // chunk):
            rows = pl.ds(c * chunk, chunk)
            ms = [m_scr[p, rows, :] for p in range(3)]
            big = jnp.maximum(jnp.maximum(ms[0], ms[1]), ms[2])
            ws = [jnp.exp(m - big) for m in ms]
            num = ws[0] * o_scr[0, rows, :] + ws[1] * o_scr[1, rows, :] + ws[2] * o_scr[2, rows, :]
            den = ws[0] * l_scr[0, rows, :] + ws[1] * l_scr[1, rows, :] + ws[2] * l_scr[2, rows, :]
            o_ref[rows, :] = ((num / den) * gate_ref[rows, :].astype(F32)).astype(o_ref.dtype)


NEAR = 512


def _sample_attn_seqs(sources, tt):
    assert 2 * tt == SUBLANES
    nt = (((1,), (1,)), ((), ()))
    row8 = lax.broadcasted_iota(jnp.int32, (SUBLANES, LANES), 0)
    lane8 = lax.broadcasted_iota(jnp.int32, (SUBLANES, LANES), 1)
    top = row8 < tt
    own_head = top == (lane8 < HEAD_DIM)

    def dup(x, half):
        return jnp.where(row8 // tt == half, x, pltpu.roll(x, tt, axis=0))

    jq = lambda shape: lax.broadcasted_iota(jnp.int32, shape, 0) & (tt - 1)
    d_full = WINDOW_MAX + jq((SUBLANES, WINDOW_MAX)) - lax.broadcasted_iota(jnp.int32, (SUBLANES, WINDOW_MAX), 1)
    d_near = NEAR + jq((SUBLANES, NEAR)) - lax.broadcasted_iota(jnp.int32, (SUBLANES, NEAR), 1)
    d_new = jq((SUBLANES, SUBLANES)) - lax.broadcasted_iota(jnp.int32, (SUBLANES, SUBLANES), 1)
    new_key = lax.broadcasted_iota(jnp.int32, (SUBLANES, SUBLANES), 1) < tt
    ok_main, ok_new = [], []
    for window, dil in DILATED_PATTERNS:
        d = d_near if window <= NEAR else d_full
        ok_main.append(((d & (dil - 1)) == 0) & (d <= window))
        ok_new.append(new_key & (d_new >= 0) & ((d_new & (dil - 1)) == 0))

    insts = [(slice(pr * LANES, (pr + 1) * LANES), src) for src in sources for pr in range(D_ATTN // LANES)]

    scored = []
    for lanes, (q_ref, kn_ref, _, _, kc_ref, _, _, half) in insts:
        q = jnp.where(own_head, dup(q_ref[:, lanes], half), 0.0).astype(BF16)
        kn = dup(kn_ref[:, lanes], half).astype(BF16)
        s_main = jnp.dot(q, kc_ref[lanes, :].astype(BF16), preferred_element_type=F32)
        s_new = lax.dot_general(q, kn, nt, preferred_element_type=F32)
        scored.append((s_main, s_new))

    weighted = []
    for s_main, s_new in scored:
        stats = []
        for (window, dil), okm, okn in zip(DILATED_PATTERNS, ok_main, ok_new):
            sm = jnp.where(okm, s_main[:, WINDOW_MAX - NEAR:] if window <= NEAR else s_main, NEG_INF)
            sn = jnp.where(okn, s_new, NEG_INF)
            m = jnp.maximum(jnp.max(sm, axis=-1, keepdims=True), jnp.max(sn, axis=-1, keepdims=True))
            pm = jnp.exp(sm - m)
            pn = jnp.exp(sn - m)
            l = jnp.sum(pm, axis=-1, keepdims=True) + jnp.sum(pn, axis=-1, keepdims=True)
            stats.append((m, l, pm, pn))
        big = functools.reduce(jnp.maximum, [s[0] for s in stats])
        ws = [jnp.exp(s[0] - big) for s in stats]
        den = functools.reduce(jnp.add, [w * s[1] for w, s in zip(ws, stats)])
        p_new = functools.reduce(jnp.add, [w * s[3] for w, s in zip(ws, stats)])
        p_near = functools.reduce(jnp.add, [w * s[2] for w, s in zip(ws, stats) if s[2].shape[1] == NEAR])
        p_full = functools.reduce(jnp.add, [w * s[2] for w, s in zip(ws, stats) if s[2].shape[1] != NEAR])
        p_main = jnp.concatenate([p_full[:, 0:WINDOW_MAX - NEAR], p_full[:, WINDOW_MAX - NEAR:] + p_near], axis=1)
        weighted.append((p_main.astype(BF16), p_new.astype(BF16).astype(F32), den))

    results = []
    for (lanes, (_, _, vn_ref, gate_ref, _, vc_ref, _, half)), (p_main, p_new, den) in zip(insts, weighted):
        acc = lax.dot_general(p_main, vc_ref[lanes, :].astype(BF16), nt, preferred_element_type=F32)
        vn = dup(vn_ref[:, lanes], half).astype(BF16).astype(F32)
        for j in range(tt):
            acc = acc + p_new[:, j:j + 1] * vn[j:j + 1, :]
        acc = acc / den
        merged = jnp.where(lane8 < HEAD_DIM, acc, pltpu.roll(acc, tt, axis=0))
        results.append(merged * dup(gate_ref[:, lanes], half))

    n_pairs = D_ATTN // LANES
    for s, src in enumerate(sources):
        src[6][...] = jnp.concatenate(results[s * n_pairs:(s + 1) * n_pairs], axis=1)


def _sample_specs(first_seq, steps_per_seq, lag, tt):
    local = lambda i: jnp.maximum(i - lag, 0) // steps_per_seq
    seq = lambda i: first_seq + local(i)
    tok = pl.BlockSpec((SUBLANES, D_ATTN), lambda i: (seq(i) * tt // SUBLANES, 0))
    cache = pl.BlockSpec((None, D_ATTN, WINDOW_MAX), lambda i: (seq(i), 0, 0))
    out = pl.BlockSpec((None, SUBLANES, D_ATTN), lambda i: (local(i), 0, 0))
    half = lambda i: seq(i) % (SUBLANES // tt)
    return tok, cache, out, half


def _attention_kernel(*refs, tt, halves):
    tok_a, tok_b = refs[0:4], refs[4:8]
    (kc_a, vc_a, kc_b, vc_b, pq_ref, pk_ref, pv_ref, pgate_ref,
     so_a, so_b, po_ref, o_scr, m_scr, l_scr, band_scr, causal_scr) = refs[8:]
    i = pl.program_id(0)

    @pl.when(i == 0)
    def _():
        for scr, kwin in ((band_scr, 2 * BAND), (causal_scr, BAND)):
            qq = lax.broadcasted_iota(jnp.int32, (2 * BAND, kwin), 0) & (BAND - 1)
            kk = lax.broadcasted_iota(jnp.int32, (2 * BAND, kwin), 1)
            valid = ((kk >= qq) & (kk <= qq + BAND)) if kwin > BAND else (kk <= qq)
            scr[...] = jnp.where(valid, 0.0, NEG_INF)

    src_a = (*tok_a, kc_a, vc_a, so_a, halves[0](i))
    src_b = (*tok_b, kc_b, vc_b, so_b, halves[1](i))
    for part in range(2):
        @pl.when(i % 2 == part)
        def _(part=part):
            _sample_attn_seqs([src_a, src_b] if part == 1 else [src_a], tt)
            _prompt_attn_part(pq_ref, pk_ref, pv_ref, pgate_ref, po_ref, o_scr, m_scr, l_scr,
                              band_scr, causal_scr, part)


def _attention(stok, cache_kt, cache_vt, pq, pk, pv, pgate, *, n_seqs, tt, batch, t_total):
    pairs = D_ATTN // LANES
    steps = 2 * batch * pairs
    n_a, n_b = steps, steps // 2
    assert n_seqs == n_a + n_b
    tok_a, cache_a, out_a, half_a = _sample_specs(0, 1, 0, tt)
    tok_b, cache_b, out_b, half_b = _sample_specs(n_a, 2, 1, tt)
    shp = (batch, t_total, D_ATTN)
    unit = pl.BlockSpec((None, t_total, LANES), lambda i: (i // (2 * pairs), 0, (i // 2) % pairs))
    so_a, so_b, po = pl.pallas_call(
        functools.partial(_attention_kernel, tt=tt, halves=(half_a, half_b)),
        grid=(steps,),
        in_specs=[tok_a] * 4 + [tok_b] * 4 + [cache_a, cache_a, cache_b, cache_b] + [unit] * 4,
        out_specs=[out_a, out_b, unit],
        out_shape=[jax.ShapeDtypeStruct((n_a, SUBLANES, D_ATTN), F32),
                   jax.ShapeDtypeStruct((n_b, SUBLANES, D_ATTN), F32), jax.ShapeDtypeStruct(shp, BF16)],
        scratch_shapes=[pltpu.VMEM((3, t_total, LANES), F32)] * 3
                       + [pltpu.VMEM((2 * BAND, 2 * BAND), F32), pltpu.VMEM((2 * BAND, BAND), F32)],
        compiler_params=pltpu.CompilerParams(dimension_semantics=("arbitrary",), vmem_limit_bytes=VMEM_LIMIT),
        name="attention",
    )(*stok, *stok, cache_kt, cache_vt, cache_kt, cache_vt,
      pq.reshape(shp), pk.reshape(shp), pv.reshape(shp), pgate.reshape(shp))
    return jnp.concatenate([so_a, so_b], axis=0), po.reshape(batch * t_total, D_ATTN)


def _out_proj_kernel(x_ref, ys_ref, ya_ref, ga_ref, w_ref, gf_ref, y_ref):
    ys = jnp.concatenate([ys_ref[s] for s in range(4)], axis=1).astype(BF16)
    ya = (_rms_scale(ya_ref[...]) * ga_ref[...]).astype(BF16)
    mix = jnp.concatenate([ys, ya], axis=1)
    h = x_ref[...] + jnp.dot(mix, w_ref[...], preferred_element_type=F32)
    y_ref[...] = _rms_scale(h) * gf_ref[...]


def _out_proj(x, ys, ya, norm_attn_g, w_out_bf, final_g):
    rows = x.shape[0]
    row_spec = lambda width: pl.BlockSpec((ROW_TILE, width), lambda i: (i, 0))
    const = lambda shape: pl.BlockSpec(shape, lambda i: (0,) * len(shape))
    return pl.pallas_call(
        _out_proj_kernel,
        grid=(rows // ROW_TILE,),
        in_specs=[row_spec(D_MODEL), pl.BlockSpec((4, ROW_TILE, LANES), lambda i: (0, i, 0)), row_spec(D_ATTN),
                  const((1, D_ATTN)), const((D_MODEL, D_MODEL)), const((1, D_MODEL))],
        out_specs=row_spec(D_MODEL),
        out_shape=jax.ShapeDtypeStruct((rows, D_MODEL), F32),
        compiler_params=pltpu.CompilerParams(dimension_semantics=("parallel",), vmem_limit_bytes=VMEM_LIMIT),
        name="out_proj",
    )(x, ys, ya, norm_attn_g.reshape(1, D_ATTN), w_out_bf, final_g.reshape(1, D_MODEL))


def kernel(x_prompt, x_sample, cache_k, cache_v, state_ssm_re, state_ssm_im, norm_in_g, w_in,
           ssm_A_re, ssm_A_im, ssm_log_dt, ssm_B_re, ssm_B_im, ssm_C_re, ssm_C_im, ssm_D,
           w_glu, b_glu, norm_ssm_g, norm_attn_g, w_out, final_norm_g):
    depth = w_in.shape[0]
    assert depth == 1
    batch, seq, _ = x_prompt.shape
    dec_batch, dec_seq, _ = x_sample.shape
    assert batch == SUBLANES and seq == WINDOW_MAX and seq % SSM_STEPS == 0

    lam, bb_re, bb_im = _ssm_prep(ssm_A_re[0], ssm_A_im[0], ssm_log_dt[0], ssm_B_re[0], ssm_B_im[0])
    wb, cc = _block_diag_weights(bb_re, bb_im, ssm_C_re[0], ssm_C_im[0])
    w_glu_bf = w_glu[0].astype(BF16)
    w_out_bf = w_out[0].astype(BF16)

    ssm = functools.partial(_ssm, lam=lam, wb=wb, cc=cc, d_vec=ssm_D[0], w_glu_bf=w_glu_bf, b_glu=b_glu[0],
                            norm_g=norm_ssm_g[0])

    xp = x_prompt.reshape(batch * seq, D_MODEL)
    xzp, qp, kp, vp, gp, kpt, vpt = _in_proj(xp, norm_in_g[0], w_in[0], seq_len=seq)
    xs = x_sample.reshape(dec_batch * dec_seq, D_MODEL)
    xzs, qs, ksm, vsm, gs = _in_proj_small(xs, norm_in_g[0], w_in[0])

    chan_major = lambda c: c[0].transpose(0, 2, 3, 1).reshape(dec_batch, D_ATTN, WINDOW_MAX)
    cache_kt, cache_vt = chan_major(cache_k), chan_major(cache_v)
    stok = (qs, ksm, vsm, gs)
    n_guest = seq // SSM_STEPS
    yas_main, yap = _attention(stok, cache_kt, cache_vt, qp, kp, vp, gp,
                               n_seqs=dec_batch - n_guest, tt=dec_seq, batch=batch, t_total=seq)

    yp, stp, yas_guest = ssm(xzp, jnp.zeros((batch, 2 * D_STATE), F32), nb=batch, t_total=seq, tt=SSM_STEPS,
                             out_proj=(x_prompt, yap.reshape(batch, seq, D_ATTN), norm_attn_g[0], w_out_bf,
                                       final_norm_g),
                             guest=(stok, cache_kt, cache_vt, dec_batch - n_guest, dec_seq))
    yas = jnp.concatenate([yas_main, yas_guest], axis=0)[:, 0:dec_seq].reshape(dec_batch * dec_seq, D_ATTN)
    h0 = jnp.concatenate([state_ssm_re[0].reshape(dec_batch, D_STATE),
                          state_ssm_im[0].reshape(dec_batch, D_STATE)], axis=1)
    yss, sts = ssm(xzs, h0, nb=dec_batch, t_total=dec_seq, tt=dec_seq)
    ysm = _out_proj(xs, yss, yas, norm_attn_g[0], w_out_bf, final_norm_g)

    heads = lambda a, n, t: a.reshape(1, n, t, N_HEADS, HEAD_DIM)
    heads_t = lambda a, n, t: a.reshape(n, N_HEADS, HEAD_DIM, t).transpose(0, 3, 1, 2)[None]
    state = lambda s, n, part: s[:, part * D_STATE:(part + 1) * D_STATE].reshape(1, n, N_SSM_GROUPS, SSM_STATE)
    return (yp.reshape(batch, seq, D_MODEL), ysm.reshape(dec_batch, dec_seq, D_MODEL),
            heads_t(kpt, batch, seq), heads_t(vpt, batch, seq), state(stp, batch, 0), state(stp, batch, 1),
            heads(ksm, dec_batch, dec_seq), heads(vsm, dec_batch, dec_seq),
            state(sts, dec_batch, 0), state(sts, dec_batch, 1))
```
